```python
import math
import jax, jax.numpy as jnp
from jax import lax
import numpy as np

D_MODEL = 1024
BATCH = 2
SEQ = 16384
DEPTH = 1
DEC_BATCH = 8
DEC_SEQ = 8192
PAST_LEN = 128

GRID_W = 64
EPS = 1e-6
N_MOD = 6
D_HYENA = 512
SHORT_CONV = 3
FILTER_EMB = 33
FILTER_BANDS = (FILTER_EMB - 1) // 2
FILTER_HIDDEN = 64
DECAY_TARGET = 1e-2
FAST_DECAY_PCT = 0.3
SLOW_DECAY_PCT = 1.5
MIN_DECAY = math.log(DECAY_TARGET) / SLOW_DECAY_PCT
MAX_DECAY = math.log(DECAY_TARGET) / FAST_DECAY_PCT
HEAD_DIM = 128
N_Q_HEADS = 4
N_KV_HEADS = 2
Q_PER_KV = N_Q_HEADS // N_KV_HEADS
D_ATTN = N_Q_HEADS * HEAD_DIM
D_KV = N_KV_HEADS * HEAD_DIM
ROPE_THETA = 10000.0
Q_BLOCK = 128
D_IN = 3 * D_HYENA + D_ATTN + 2 * D_KV + 2 * D_MODEL
N_EXPERTS = 32
TOP_K = 4
D_FF = 256
SWIGLU_ALPHA = 1.702
SWIGLU_LIMIT = 7.0

kernel_name = "hyena_gqa_axial_moe_adaln_encoder"


def rms_norm(x, g):
    xf = x.astype(jnp.float32)
    y = xf * lax.rsqrt(jnp.mean(xf * xf, axis=-1, keepdims=True) + EPS)
    return (y * g.astype(jnp.float32)).astype(x.dtype)


def short_conv(u, w, b):
    L = u.shape[1]
    up = jnp.pad(u, ((0, 0), (1, 1), (0, 0)))
    return up[:, :L] * w[0] + up[:, 1:L + 1] * w[1] + up[:, 2:] * w[2] + b


def hyena_filter_freq(L, w1, b1, f1, w2, b2, f2, w3, b3):
    f32 = jnp.float32
    t = jnp.linspace(0.0, 1.0, L, dtype=f32)[:, None]
    w = (2.0 * math.pi / L) * jnp.arange(L, dtype=f32)[:, None]
    bands = jnp.linspace(1e-4, FILTER_BANDS - 1, FILTER_BANDS, dtype=f32)[None, :]
    z = jnp.concatenate([t, jnp.cos(bands * w), -jnp.sin(bands * w)], axis=-1)
    h = jnp.sin(f1.astype(f32) * (z @ w1.astype(f32) + b1.astype(f32)))
    h = jnp.sin(f2.astype(f32) * (h @ w2.astype(f32) + b2.astype(f32)))
    k = (h @ w3.astype(f32) + b3.astype(f32)).reshape(L, 2, D_HYENA)
    deltas = jnp.abs(jnp.linspace(MIN_DECAY, MAX_DECAY, D_HYENA, dtype=f32))
    k = k * jnp.exp(-t * deltas)[:, None, :]
    k_fwd, k_bwd = k[:, 0], k[:, 1]
    k_circ = jnp.concatenate([k_fwd, jnp.zeros((1, D_HYENA), f32), k_bwd[:0:-1]], axis=0)
    return jnp.fft.rfft(k_circ, axis=0)


def fft_conv(u, k_freq, bias):
    L = u.shape[1]
    uf = u.astype(jnp.float32)
    y = jnp.fft.irfft(jnp.fft.rfft(uf, n=2 * L, axis=1) * k_freq[None], n=2 * L, axis=1)[:, :L]
    return (y + uf * bias.astype(jnp.float32)).astype(u.dtype)


def axial_rope_tables(L):
    rows = L // GRID_W
    row = jnp.repeat(jnp.arange(rows, dtype=jnp.float32), GRID_W)
    col = jnp.tile(jnp.arange(GRID_W, dtype=jnp.float32), rows)
    n_freq = HEAD_DIM // 4
    freqs = ROPE_THETA ** (-jnp.arange(n_freq, dtype=jnp.float32) / n_freq)
    ang_r = row[:, None] * freqs
    ang_c = col[:, None] * freqs
    cos = jnp.concatenate([jnp.cos(ang_r), jnp.cos(ang_r), jnp.cos(ang_c), jnp.cos(ang_c)], axis=-1)
    sin = jnp.concatenate([jnp.sin(ang_r), jnp.sin(ang_r), jnp.sin(ang_c), jnp.sin(ang_c)], axis=-1)
    return cos, sin


def apply_rope(x, cos, sin):
    xs = x.reshape(*x.shape[:-1], 2, 2, HEAD_DIM // 4)
    rot = jnp.stack([-xs[..., 1, :], xs[..., 0, :]], axis=-2).reshape(x.shape)
    return x * cos[None, :, None, :] + rot * sin[None, :, None, :]


def gqa_attention(q, k, v, q_norm, k_norm):
    B, L, _ = q.shape
    cos, sin = axial_rope_tables(L)
    q = q.reshape(B, L, N_Q_HEADS, HEAD_DIM)
    k = k.reshape(B, L, N_KV_HEADS, HEAD_DIM)
    v = v.reshape(B, L, N_KV_HEADS, HEAD_DIM)
    q = apply_rope(rms_norm(q, q_norm).astype(jnp.float32), cos, sin).astype(v.dtype)
    k = apply_rope(rms_norm(k, k_norm).astype(jnp.float32), cos, sin).astype(v.dtype)
    qb = q.reshape(B, L // Q_BLOCK, Q_BLOCK, N_KV_HEADS, Q_PER_KV, HEAD_DIM).swapaxes(0, 1)
    scale = 1.0 / math.sqrt(HEAD_DIM)

    def block(q_blk):
        s = jnp.einsum('bqhgd,bkhd->bhgqk', q_blk, k).astype(jnp.float32) * scale
        p = jnp.exp(s - jnp.max(s, axis=-1, keepdims=True))
        denom = jnp.transpose(jnp.sum(p, axis=-1), (0, 3, 1, 2))[..., None]
        o = jnp.einsum('bhgqk,bkhd->bqhgd', p.astype(v.dtype), v)
        return (o.astype(jnp.float32) / denom).astype(v.dtype)

    o = lax.map(block, qb)
    return o.swapaxes(0, 1).reshape(B, L, D_ATTN)


def clamped_swiglu(h):
    x_glu = jnp.minimum(h[..., ::2], SWIGLU_LIMIT)
    x_lin = jnp.clip(h[..., 1::2], -SWIGLU_LIMIT, SWIGLU_LIMIT)
    return x_glu * jax.nn.sigmoid(SWIGLU_ALPHA * x_glu) * (x_lin + 1)


def moe_tokens(h, router_w, router_b, w_up, b_up, w_down, b_down):
    logits = (h @ router_w + router_b).astype(jnp.float32)
    top_val, top_idx = lax.top_k(logits, TOP_K)
    probs = jax.nn.softmax(top_val, axis=-1)
    gates = jnp.sum(jax.nn.one_hot(top_idx, N_EXPERTS, dtype=jnp.float32) * probs[..., None], axis=-2)
    gates = gates.astype(h.dtype)
    up = jnp.einsum('td,edf->tef', h, w_up) + b_up
    act = clamped_swiglu(up) * gates[..., None]
    return jnp.einsum('tef,efd->td', act, w_down) + gates @ b_down


def encoder_layer(x, c, p):
    B, L, _ = x.shape
    mod = jax.nn.silu(c) @ p['w_ada'] + p['b_ada']
    shift_m, scale_m, gate_m, shift_f, scale_f, gate_f = jnp.split(mod[:, None, :], N_MOD, axis=-1)

    n = rms_norm(x, p['norm_mix']) * (1 + scale_m) + shift_m
    proj = n @ p['w_in']
    s0 = 3 * D_HYENA
    s1 = s0 + D_ATTN
    s2 = s1 + D_KV
    s3 = s2 + D_KV
    s4 = s3 + D_MODEL
    u_h, q, k, v, g_h, g_a = jnp.split(proj, [s0, s1, s2, s3, s4], axis=-1)

    u_h = short_conv(u_h, p['conv_w'], p['conv_b'])
    x0, x1, vh = jnp.split(u_h, 3, axis=-1)
    k_freq = hyena_filter_freq(L, p['filt_w1'], p['filt_b1'], p['filt_freq1'], p['filt_w2'],
                               p['filt_b2'], p['filt_freq2'], p['filt_w3'], p['filt_b3'])
    y_h = x0 * fft_conv(x1 * vh, k_freq, p['filt_bias'])

    y_a = gqa_attention(q, k, v, p['q_norm'], p['k_norm'])

    mix = (jax.nn.sigmoid(g_h) * (y_h @ p['w_out_h'])
           + jax.nn.sigmoid(g_a) * (y_a @ p['w_out_a'])) @ p['w_o']
    x = x + gate_m * mix

    n2 = rms_norm(x, p['norm_ffn']) * (1 + scale_f) + shift_f
    moe_out = lax.map(lambda h: moe_tokens(h, p['router_w'], p['router_b'], p['w_up'], p['b_up'],
                                           p['w_down'], p['b_down']), n2)
    return x + gate_f * moe_out


def setup_inputs(seed: int = 0) -> dict:
    key = jax.random.key(seed)
    ks = jax.random.split(key, 32)

    def nrm(k, shape, scale):
        return jax.random.normal(k, shape, jnp.float32) * scale

    def gain(k, shape):
        return 1.0 + nrm(k, shape, 0.1)

    return {
        'x_prompt': nrm(ks[0], (BATCH, SEQ, D_MODEL), 1.0),
        'x_sample': nrm(ks[1], (DEC_BATCH, DEC_SEQ, D_MODEL), 1.0),
        'c_prompt': nrm(ks[2], (BATCH, D_MODEL), 1.0),
        'c_sample': nrm(ks[3], (DEC_BATCH, D_MODEL), 1.0),
        'w_ada': nrm(ks[4], (DEPTH, D_MODEL, N_MOD * D_MODEL), 0.5 * D_MODEL ** -0.5),
        'b_ada': nrm(ks[5], (DEPTH, N_MOD * D_MODEL), 0.02),
        'norm_mix': gain(ks[6], (DEPTH, D_MODEL)),
        'w_in': nrm(ks[7], (DEPTH, D_MODEL, D_IN), D_MODEL ** -0.5),
        'conv_w': nrm(ks[8], (DEPTH, SHORT_CONV, 3 * D_HYENA), 0.5),
        'conv_b': nrm(ks[9], (DEPTH, 3 * D_HYENA), 0.02),
        'filt_w1': nrm(ks[10], (DEPTH, FILTER_EMB, FILTER_HIDDEN), FILTER_EMB ** -0.5),
        'filt_b1': nrm(ks[11], (DEPTH, FILTER_HIDDEN), 0.02),
        'filt_freq1': gain(ks[12], (DEPTH, FILTER_HIDDEN)),
        'filt_w2': nrm(ks[13], (DEPTH, FILTER_HIDDEN, FILTER_HIDDEN), FILTER_HIDDEN ** -0.5),
        'filt_b2': nrm(ks[14], (DEPTH, FILTER_HIDDEN), 0.02),
        'filt_freq2': gain(ks[15], (DEPTH, FILTER_HIDDEN)),
        'filt_w3': nrm(ks[16], (DEPTH, FILTER_HIDDEN, 2 * D_HYENA), 0.015),
        'filt_b3': nrm(ks[17], (DEPTH, 2 * D_HYENA), 0.002),
        'filt_bias': nrm(ks[18], (DEPTH, D_HYENA), 1.0),
        'q_norm': gain(ks[19], (DEPTH, HEAD_DIM)),
        'k_norm': gain(ks[20], (DEPTH, HEAD_DIM)),
        'w_out_h': nrm(ks[21], (DEPTH, D_HYENA, D_MODEL), D_HYENA ** -0.5),
        'w_out_a': nrm(ks[22], (DEPTH, D_ATTN, D_MODEL), D_ATTN ** -0.5),
        'w_o': nrm(ks[23], (DEPTH, D_MODEL, D_MODEL), D_MODEL ** -0.5),
        'norm_ffn': gain(ks[24], (DEPTH, D_MODEL)),
        'router_w': nrm(ks[25], (DEPTH, D_MODEL, N_EXPERTS), D_MODEL ** -0.5),
        'router_b': nrm(ks[26], (DEPTH, N_EXPERTS), 0.01),
        'w_up': nrm(ks[27], (DEPTH, N_EXPERTS, D_MODEL, 2 * D_FF), D_MODEL ** -0.5),
        'b_up': nrm(ks[28], (DEPTH, N_EXPERTS, 2 * D_FF), 0.02),
        'w_down': nrm(ks[29], (DEPTH, N_EXPERTS, D_FF, D_MODEL), D_FF ** -0.5),
        'b_down': nrm(ks[30], (DEPTH, N_EXPERTS, D_MODEL), 0.02),
    }


def reference(x_prompt, x_sample, c_prompt, c_sample, w_ada, b_ada, norm_mix, w_in, conv_w, conv_b,
              filt_w1, filt_b1, filt_freq1, filt_w2, filt_b2, filt_freq2, filt_w3, filt_b3, filt_bias,
              q_norm, k_norm, w_out_h, w_out_a, w_o, norm_ffn, router_w, router_b,
              w_up, b_up, w_down, b_down):
    stacked = {
        'w_ada': w_ada, 'b_ada': b_ada, 'norm_mix': norm_mix, 'w_in': w_in,
        'conv_w': conv_w, 'conv_b': conv_b,
        'filt_w1': filt_w1, 'filt_b1': filt_b1, 'filt_freq1': filt_freq1,
        'filt_w2': filt_w2, 'filt_b2': filt_b2, 'filt_freq2': filt_freq2,
        'filt_w3': filt_w3, 'filt_b3': filt_b3, 'filt_bias': filt_bias,
        'q_norm': q_norm, 'k_norm': k_norm,
        'w_out_h': w_out_h, 'w_out_a': w_out_a, 'w_o': w_o, 'norm_ffn': norm_ffn,
        'router_w': router_w, 'router_b': router_b,
        'w_up': w_up, 'b_up': b_up, 'w_down': w_down, 'b_down': b_down,
    }
    y_prompt = x_prompt
    y_sample = x_sample
    for l in range(DEPTH):
        p = {name: arr[l] for name, arr in stacked.items()}
        y_prompt = encoder_layer(y_prompt, c_prompt, p)
        y_sample = encoder_layer(y_sample, c_sample, p)
    return (y_prompt, y_sample)
```

```python
import functools
import math

import jax
import jax.numpy as jnp
from jax import lax
from jax.experimental import pallas as pl
from jax.experimental.pallas import tpu as pltpu

F32 = jnp.float32
BF16 = jnp.bfloat16

EPS = 1e-6
N_MOD = 6
GRID_W = 64
HEAD_DIM = 128
N_Q_HEADS = 4
N_KV_HEADS = 2
ROPE_THETA = 10000.0
TOP_K = 4
SWIGLU_ALPHA = 1.702
SWIGLU_LIMIT = 7.0
FILTER_EMB = 33
FILTER_BANDS = (FILTER_EMB - 1) // 2
DECAY_TARGET = 1e-2
MIN_DECAY = math.log(DECAY_TARGET) / 1.5
MAX_DECAY = math.log(DECAY_TARGET) / 0.3

LANES = 128
NEG_BIG = -1e30
VMEM_LIMIT = 56 * 1024 * 1024


def _cparams(*sem):
    return pltpu.CompilerParams(dimension_semantics=sem, vmem_limit_bytes=VMEM_LIMIT)


def _dot(a, b):
    return jnp.dot(a, b, preferred_element_type=F32)


def _dot_hi(a, b):
    return jnp.dot(a, b, preferred_element_type=F32, precision=lax.Precision.HIGHEST)


def _tile(n, want):
    t = min(n, want)
    assert n % t == 0, (n, want)
    return t


def _adaln_kernel(c_ref, w_ref, b_ref, o_ref):
    c = c_ref[...]
    o_ref[...] = _dot_hi(c * jax.nn.sigmoid(c), w_ref[...]) + b_ref[...]


def adaln(c, w_ada, b_ada):
    r, d = c.shape
    n = w_ada.shape[1]
    tn = _tile(n, 1536)
    return pl.pallas_call(
        _adaln_kernel,
        grid=(n // tn,),
        in_specs=[pl.BlockSpec((r, d), lambda j: (0, 0)),
                  pl.BlockSpec((d, tn), lambda j: (0, j)),
                  pl.BlockSpec((1, tn), lambda j: (0, j))],
        out_specs=pl.BlockSpec((r, tn), lambda j: (0, j)),
        out_shape=jax.ShapeDtypeStruct((r, n), F32),
        compiler_params=_cparams("arbitrary"),
        name="adaln",
    )(c, w_ada, b_ada.reshape(1, n))


def _rope(xn, cos, sin_signed):
    lane = lax.broadcasted_iota(jnp.int32, xn.shape, 1)
    first_half = (lane % 64) < 32
    rot = jnp.where(first_half, pltpu.roll(xn, 96, 1), pltpu.roll(xn, 32, 1))
    return xn * cos + rot * sin_signed


def _inproj_kernel(x_ref, mod_ref, g_ref, w_ref, cos_ref, sin_ref, qg_ref, kg_ref,
                   uh_ref, q_ref, k_ref, v_ref, gh_ref, ga_ref, *, d_hy3, d_attn, d_kv, d_model, q_scale):
    x = x_ref[...]
    y = x * lax.rsqrt(jnp.mean(x * x, axis=-1, keepdims=True) + EPS) * g_ref[...]
    n = (y * (1.0 + mod_ref[1:2, :]) + mod_ref[0:1, :]).astype(BF16)

    s0 = d_hy3
    s1 = s0 + d_attn
    s2 = s1 + d_kv
    s3 = s2 + d_kv
    s4 = s3 + d_model
    uh_ref[...] = _dot(n, w_ref[:, 0:s0]).astype(BF16)
    v_ref[...] = _dot(n, w_ref[:, s2:s3]).astype(BF16)
    gh_ref[...] = _dot(n, w_ref[:, s3:s4]).astype(BF16)
    ga_ref[...] = _dot(n, w_ref[:, s4:s4 + d_model]).astype(BF16)

    cos = cos_ref[...]
    sin = sin_ref[...]

    def norm_rope(z, gain):
        zn = z * lax.rsqrt(jnp.mean(z * z, axis=-1, keepdims=True) + EPS) * gain
        return _rope(zn, cos, sin)

    qf = _dot(n, w_ref[:, s0:s1])
    for h in range(d_attn // HEAD_DIM):
        sl = slice(h * HEAD_DIM, (h + 1) * HEAD_DIM)
        q_ref[:, sl] = (norm_rope(qf[:, sl], qg_ref[...]) * q_scale).astype(BF16)
    kf = _dot(n, w_ref[:, s1:s2])
    for h in range(d_kv // HEAD_DIM):
        sl = slice(h * HEAD_DIM, (h + 1) * HEAD_DIM)
        k_ref[:, sl] = norm_rope(kf[:, sl], kg_ref[...]).astype(BF16)


def inproj(x, mod, norm_g, w_in_bf, cos, sin_signed, q_norm, k_norm, d_hy3, d_attn, d_kv):
    b, l, d = x.shape
    tm = _tile(l, 512)
    d_in = w_in_bf.shape[1]
    q_scale = math.log2(math.e) / math.sqrt(HEAD_DIM)
    kern = functools.partial(_inproj_kernel, d_hy3=d_hy3, d_attn=d_attn, d_kv=d_kv, d_model=d, q_scale=q_scale)

    def tok(w):
        return pl.BlockSpec((None, tm, w), lambda bi, i: (bi, i, 0))

    def const(shape):
        return pl.BlockSpec(shape, lambda bi, i: (0,) * len(shape))

    outs = [d_hy3, d_attn, d_kv, d_kv, d, d]
    return pl.pallas_call(
        kern,
        grid=(b, l // tm),
        in_specs=[tok(d),
                  pl.BlockSpec((None, N_MOD, d), lambda bi, i: (bi, 0, 0)),
                  const((1, d)),
                  const((d, d_in)),
                  pl.BlockSpec((tm, HEAD_DIM), lambda bi, i: (i, 0)),
                  pl.BlockSpec((tm, HEAD_DIM), lambda bi, i: (i, 0)),
                  const((1, HEAD_DIM)),
                  const((1, HEAD_DIM))],
        out_specs=[tok(w) for w in outs],
        out_shape=[jax.ShapeDtypeStruct((b, l, w), BF16) for w in outs],
        compiler_params=_cparams("parallel", "parallel"),
        name="inproj",
    )(x, mod, norm_g.reshape(1, d), w_in_bf, cos, sin_signed, q_norm.reshape(1, HEAD_DIM), k_norm.reshape(1, HEAD_DIM))


def _hyena_pre_kernel(u_ref, prev_ref, next_ref, w_ref, b_ref, x0_ref, uu_ref, *, c):
    i = pl.program_id(1)
    nblk = pl.num_programs(1)
    u = u_ref[...].astype(F32)
    tl = u.shape[0]
    row = lax.broadcasted_iota(jnp.int32, u.shape, 0)
    prev_row = prev_ref[7:8, :].astype(F32) * jnp.where(i > 0, 1.0, 0.0)
    next_row = next_ref[0:1, :].astype(F32) * jnp.where(i < nblk - 1, 1.0, 0.0)
    up = jnp.where(row == 0, prev_row, pltpu.roll(u, 1, 0))
    un = jnp.where(row == tl - 1, next_row, pltpu.roll(u, tl - 1, 0))
    y = up * w_ref[0:1, :] + u * w_ref[1:2, :] + un * w_ref[2:3, :] + b_ref[...]
    x0_ref[...] = y[:, 0:c].astype(BF16)
    uu_ref[...] = (y[:, c:2 * c] * y[:, 2 * c:3 * c]).astype(BF16)


def hyena_pre(uh, conv_w, conv_b):
    b, l, c3 = uh.shape
    c = c3 // 3
    tl = _tile(l, 512)
    r = tl // 8
    nrow8 = l // 8
    kern = functools.partial(_hyena_pre_kernel, c=c)
    return pl.pallas_call(
        kern,
        grid=(b, l // tl),
        in_specs=[pl.BlockSpec((None, tl, c3), lambda bi, i: (bi, i, 0)),
                  pl.BlockSpec((None, 8, c3), lambda bi, i: (bi, jnp.maximum(i * r - 1, 0), 0)),
                  pl.BlockSpec((None, 8, c3), lambda bi, i: (bi, jnp.minimum((i + 1) * r, nrow8 - 1), 0)),
                  pl.BlockSpec((3, c3), lambda bi, i: (0, 0)),
                  pl.BlockSpec((1, c3), lambda bi, i: (0, 0))],
        out_specs=[pl.BlockSpec((None, tl, c), lambda bi, i: (bi, i, 0)),
                   pl.BlockSpec((None, tl, c), lambda bi, i: (bi, i, 0))],
        out_shape=[jax.ShapeDtypeStruct((b, l, c), BF16)] * 2,
        compiler_params=_cparams("parallel", "parallel"),
        name="hyena_pre",
    )(uh, uh, uh, conv_w, conv_b.reshape(1, c3))


def _filter_kernel(z_ref, t_ref, w1_ref, b1_ref, f1_ref, w2_ref, b2_ref, f2_ref, w3_ref, b3_ref, dl_ref, o_ref):
    h = jnp.sin(f1_ref[...] * (_dot_hi(z_ref[...], w1_ref[...]) + b1_ref[...]))
    h = jnp.sin(f2_ref[...] * (_dot_hi(h, w2_ref[...]) + b2_ref[...]))
    k = _dot_hi(h, w3_ref[...]) + b3_ref[...]
    o_ref[...] = k * jnp.exp(-t_ref[:, 0:1] * dl_ref[...])


def hyena_filter(l, w1, b1, f1, w2, b2, f2, w3, b3):
    c2 = w3.shape[1]
    c = c2 // 2
    hid = w1.shape[1]
    t = jnp.linspace(0.0, 1.0, l, dtype=F32)[:, None]
    w = (2.0 * math.pi / l) * jnp.arange(l, dtype=F32)[:, None]
    bands = jnp.linspace(1e-4, FILTER_BANDS - 1, FILTER_BANDS, dtype=F32)[None, :]
    z = jnp.concatenate([t, jnp.cos(bands * w), -jnp.sin(bands * w)], axis=-1)
    z = jnp.pad(z, ((0, 0), (0, LANES - FILTER_EMB)))
    tlane = jnp.broadcast_to(t, (l, LANES))

    def padv(v):
        return jnp.pad(v.astype(F32), (0, LANES - hid)).reshape(1, LANES)

    w1p = jnp.pad(w1.astype(F32), ((0, LANES - FILTER_EMB), (0, LANES - hid)))
    w2p = jnp.pad(w2.astype(F32), ((0, LANES - hid), (0, LANES - hid)))
    w3p = jnp.pad(w3.astype(F32), ((0, LANES - hid), (0, 0)))
    deltas = jnp.abs(jnp.linspace(MIN_DECAY, MAX_DECAY, c, dtype=F32))
    dl = jnp.concatenate([deltas, deltas]).reshape(1, c2)
    tl = _tile(l, 1024)

    def const(shape):
        return pl.BlockSpec(shape, lambda i: (0, 0))

    return pl.pallas_call(
        _filter_kernel,
        grid=(l // tl,),
        in_specs=[pl.BlockSpec((tl, LANES), lambda i: (i, 0)),
                  pl.BlockSpec((tl, LANES), lambda i: (i, 0)),
                  const((LANES, LANES)), const((1, LANES)), const((1, LANES)),
                  const((LANES, LANES)), const((1, LANES)), const((1, LANES)),
                  const((LANES, c2)), const((1, c2)), const((1, c2))],
        out_specs=pl.BlockSpec((tl, c2), lambda i: (i, 0)),
        out_shape=jax.ShapeDtypeStruct((l, c2), F32),
        compiler_params=_cparams("parallel"),
        name="hyena_filter",
    )(z, tlane, w1p, padv(b1), padv(f1), w2p, padv(b2), padv(f2), w3p, b3.astype(F32).reshape(1, c2), dl)


def _fft_split(n):
    n2 = 128
    n1 = n // n2
    assert n1 * n2 == n and n1 % 16 == 0, n
    return n1, n2


def _angles(rows, cols, period):
    prod = (jnp.arange(rows, dtype=jnp.int32)[:, None] * jnp.arange(cols, dtype=jnp.int32)[None, :]) % period
    return prod.astype(F32) * (2.0 * math.pi / period)


def _dft_tables(n1, n2):
    n = n1 * n2
    ang1 = _angles(n1, n1, n1)
    c1, s1 = jnp.cos(ang1), jnp.sin(ang1)
    fa = jnp.concatenate([c1, -s1], axis=0).astype(BF16)
    fc = (jnp.concatenate([c1, -s1], axis=1) * (1.0 / n)).astype(BF16)
    ang2 = _angles(n2, n2, n2)
    c2, s2 = jnp.cos(ang2), jnp.sin(ang2)
    g_fwd = jnp.block([[c2, s2], [-s2, c2]]).astype(BF16)
    g_inv = jnp.block([[c2, -s2], [s2, c2]]).astype(BF16)
    angt = _angles(n1, n2, n)
    twr = jnp.broadcast_to(jnp.cos(angt)[:, :, None], (n1, n2, LANES))
    twi = jnp.broadcast_to(jnp.sin(angt)[:, :, None], (n1, n2, LANES))
    return fa, fc, g_fwd, g_inv, twr, twi


def _fft_a_kernel(f_ref, u_ref, o_ref):
    o_ref[...] = _dot(f_ref[...], u_ref[...]).astype(BF16)


def fft_a(fa, u):
    b, k, w = u.shape
    m = fa.shape[0]
    tn = _tile(w, 2048)
    return pl.pallas_call(
        _fft_a_kernel,
        grid=(b, w // tn),
        in_specs=[pl.BlockSpec((m, k), lambda bi, j: (0, 0)),
                  pl.BlockSpec((None, k, tn), lambda bi, j: (bi, 0, j))],
        out_specs=pl.BlockSpec((None, m, tn), lambda bi, j: (bi, 0, j)),
        out_shape=jax.ShapeDtypeStruct((b, m, w), BF16),
        compiler_params=_cparams("parallel", "parallel"),
        name="fft_a",
    )(fa, u)


def _lane_tile(t, c):
    return jnp.concatenate([t] * (c // LANES), axis=1) if c > LANES else t


def _fft_mf_kernel(a_ref, twr_ref, twi_ref, g_ref, o_ref, *, tk1):
    for j in range(tk1):
        ar = a_ref[0, j].astype(F32)
        ai = a_ref[1, j].astype(F32)
        c = ar.shape[1]
        twr = _lane_tile(twr_ref[j], c)
        twi = _lane_tile(twi_ref[j], c)
        br = ar * twr + ai * twi
        bi = ai * twr - ar * twi
        x = _dot(g_ref[...], jnp.concatenate([br, bi], axis=0).astype(BF16))
        n2 = ar.shape[0]
        o_ref[0, j] = x[:n2].astype(BF16)
        o_ref[1, j] = x[n2:].astype(BF16)


def fft_mf(a, twr, twi, g_fwd):
    _, n1, n2, c = a.shape
    tk1 = 8
    kern = functools.partial(_fft_mf_kernel, tk1=tk1)
    return pl.pallas_call(
        kern,
        grid=(n1 // tk1,),
        in_specs=[pl.BlockSpec((2, tk1, n2, c), lambda i: (0, i, 0, 0)),
                  pl.BlockSpec((tk1, n2, LANES), lambda i: (i, 0, 0)),
                  pl.BlockSpec((tk1, n2, LANES), lambda i: (i, 0, 0)),
                  pl.BlockSpec((2 * n2, 2 * n2), lambda i: (0, 0))],
        out_specs=pl.BlockSpec((2, tk1, n2, c), lambda i: (0, i, 0, 0)),
        out_shape=jax.ShapeDtypeStruct((2, n1, n2, c), BF16),
        compiler_params=_cparams("parallel"),
        name="fft_mf",
    )(a, twr, twi, g_fwd)


def _fft_m_kernel(a_ref, kf_ref, twr_ref, twi_ref, gf_ref, gi_ref, o_ref, *, tk1):
    for j in range(tk1):
        ar = a_ref[0, j].astype(F32)
        ai = a_ref[1, j].astype(F32)
        n2, c = ar.shape
        twr = _lane_tile(twr_ref[j], c)
        twi = _lane_tile(twi_ref[j], c)
        br = ar * twr + ai * twi
        bi = ai * twr - ar * twi
        x = _dot(gf_ref[...], jnp.concatenate([br, bi], axis=0).astype(BF16))
        xr, xi = x[:n2], x[n2:]
        kr = kf_ref[0, j].astype(F32)
        ki = kf_ref[1, j].astype(F32)
        zr = xr * kr - xi * ki
        zi = xr * ki + xi * kr
        y = _dot(gi_ref[...], jnp.concatenate([zr, zi], axis=0).astype(BF16))
        yr, yi = y[:n2], y[n2:]
        o_ref[0, j] = (yr * twr - yi * twi).astype(BF16)
        o_ref[1, j] = (yi * twr + yr * twi).astype(BF16)


def fft_m(a, kf, twr, twi, g_fwd, g_inv):
    b, _, n1, n2, c = a.shape
    tk1 = 8
    kern = functools.partial(_fft_m_kernel, tk1=tk1)
    return pl.pallas_call(
        kern,
        grid=(n1 // tk1, b),
        in_specs=[pl.BlockSpec((None, 2, tk1, n2, c), lambda i, bi: (bi, 0, i, 0, 0)),
                  pl.BlockSpec((2, tk1, n2, c), lambda i, bi: (0, i, 0, 0)),
                  pl.BlockSpec((tk1, n2, LANES), lambda i, bi: (i, 0, 0)),
                  pl.BlockSpec((tk1, n2, LANES), lambda i, bi: (i, 0, 0)),
                  pl.BlockSpec((2 * n2, 2 * n2), lambda i, bi: (0, 0)),
                  pl.BlockSpec((2 * n2, 2 * n2), lambda i, bi: (0, 0))],
        out_specs=pl.BlockSpec((None, 2, tk1, n2, c), lambda i, bi: (bi, 0, i, 0, 0)),
        out_shape=jax.ShapeDtypeStruct(a.shape, BF16),
        compiler_params=_cparams("parallel", "parallel"),
        name="fft_m",
    )(a, kf, twr, twi, g_fwd, g_inv)


def _fft_c_kernel(f_ref, a_ref, x0_ref, uu_ref, bias_ref, o_ref):
    y = _dot(f_ref[...], a_ref[...])
    y = y + uu_ref[...].astype(F32) * bias_ref[...]
    o_ref[...] = (x0_ref[...].astype(F32) * y).astype(BF16)


def fft_c(fc_half, a2, x0, uu, bias_tiled):
    b, k, w = a2.shape
    m = fc_half.shape[0]
    tn = bias_tiled.shape[1]
    return pl.pallas_call(
        _fft_c_kernel,
        grid=(b, w // tn),
        in_specs=[pl.BlockSpec((m, k), lambda bi, j: (0, 0)),
                  pl.BlockSpec((None, k, tn), lambda bi, j: (bi, 0, j)),
                  pl.BlockSpec((None, m, tn), lambda bi, j: (bi, 0, j)),
                  pl.BlockSpec((None, m, tn), lambda bi, j: (bi, 0, j)),
                  pl.BlockSpec((1, tn), lambda bi, j: (0, 0))],
        out_specs=pl.BlockSpec((None, m, tn), lambda bi, j: (bi, 0, j)),
        out_shape=jax.ShapeDtypeStruct((b, m, w), BF16),
        compiler_params=_cparams("parallel", "parallel"),
        name="fft_c",
    )(fc_half, a2, x0, uu, bias_tiled)


def hyena_long_conv(x0, uu, k_filt, filt_bias):
    b, l, c = uu.shape
    n = 2 * l
    n1, n2 = _fft_split(n)
    fa, fc, g_fwd, g_inv, twr, twi = _dft_tables(n1, n2)
    w = n2 * c
    k_circ = jnp.concatenate([k_filt[:, :c], jnp.zeros((1, c), F32), jnp.flip(k_filt[1:, c:], axis=0)], axis=0)
    ka = fft_a(fa, k_circ.astype(BF16).reshape(1, n1, w))
    kf = fft_mf(ka.reshape(2, n1, n2, c), twr, twi, g_fwd)
    a = fft_a(fa[:, :n1 // 2], uu.reshape(b, n1 // 2, w))
    a2 = fft_m(a.reshape(b, 2, n1, n2, c), kf, twr, twi, g_fwd, g_inv)
    tn = _tile(w, 2048)
    bias_tiled = jnp.tile(filt_bias.astype(F32), tn // c).reshape(1, tn)
    yh = fft_c(fc[:n1 // 2], a2.reshape(b, 2 * n1, w), x0.reshape(b, n1 // 2, w), uu.reshape(b, n1 // 2, w),
               bias_tiled)
    return yh.reshape(b, l, c)


def _attn_kernel(q_ref, k_ref, v_ref, o_ref, qs_ref, m_ref, l_ref, acc_ref, *, tq):
    ki = pl.program_id(3)

    @pl.when(ki == 0)
    def _():
        qs_ref[0:tq, :] = q_ref[:, 0:HEAD_DIM]
        qs_ref[tq:2 * tq, :] = q_ref[:, HEAD_DIM:2 * HEAD_DIM]
        m_ref[...] = jnp.full(m_ref.shape, -jnp.inf, F32)
        l_ref[...] = jnp.zeros(l_ref.shape, F32)
        acc_ref[...] = jnp.zeros(acc_ref.shape, F32)

    s = lax.dot_general(qs_ref[...], k_ref[...], (((1,), (1,)), ((), ())), preferred_element_type=F32)
    m_prev = m_ref[...]
    m_new = jnp.maximum(m_prev, jnp.max(s, axis=-1, keepdims=True))
    alpha = jnp.exp2(m_prev - m_new)
    p = jnp.exp2(s - m_new)
    l_ref[...] = alpha * l_ref[...] + jnp.sum(p, axis=-1, keepdims=True)
    acc_ref[...] = alpha * acc_ref[...] + _dot(p.astype(BF16), v_ref[...])
    m_ref[...] = m_new

    @pl.when(ki == pl.num_programs(3) - 1)
    def _():
        o = acc_ref[...] / l_ref[...]
        o_ref[:, 0:HEAD_DIM] = o[0:tq].astype(BF16)
        o_ref[:, HEAD_DIM:2 * HEAD_DIM] = o[tq:2 * tq].astype(BF16)


def attention(q, k, v):
    b, l, _ = q.shape
    tq = _tile(l, 512)
    tk = _tile(l, 512)
    g = (N_Q_HEADS // N_KV_HEADS) * HEAD_DIM
    kern = functools.partial(_attn_kernel, tq=tq)
    return pl.pallas_call(
        kern,
        grid=(b, N_KV_HEADS, l // tq, l // tk),
        in_specs=[pl.BlockSpec((None, tq, g), lambda bi, h, i, j: (bi, i, h)),
                  pl.BlockSpec((None, tk, HEAD_DIM), lambda bi, h, i, j: (bi, j, h)),
                  pl.BlockSpec((None, tk, HEAD_DIM), lambda bi, h, i, j: (bi, j, h))],
        out_specs=pl.BlockSpec((None, tq, g), lambda bi, h, i, j: (bi, i, h)),
        out_shape=jax.ShapeDtypeStruct(q.shape, BF16),
        scratch_shapes=[pltpu.VMEM((2 * tq, HEAD_DIM), BF16),
                        pltpu.VMEM((2 * tq, 1), F32),
                        pltpu.VMEM((2 * tq, 1), F32),
                        pltpu.VMEM((2 * tq, HEAD_DIM), F32)],
        compiler_params=_cparams("parallel", "parallel", "parallel", "arbitrary"),
        name="attention",
    )(q, k, v)


def _mix_kernel(x_ref, yh_ref, ya_ref, gh_ref, ga_ref, mod_ref, woh_ref, woa_ref, wo_ref, g_ref, rw_ref, rb_ref,
                xn_ref, n2_ref, gates_ref):
    th = _dot(yh_ref[...], woh_ref[...])
    ta = _dot(ya_ref[...], woa_ref[...])
    mixed = (jax.nn.sigmoid(gh_ref[...].astype(F32)) * th + jax.nn.sigmoid(ga_ref[...].astype(F32)) * ta)
    mix = _dot(mixed.astype(BF16), wo_ref[...])
    x = x_ref[...] + mod_ref[2:3, :] * mix
    xn_ref[...] = x

    y = x * lax.rsqrt(jnp.mean(x * x, axis=-1, keepdims=True) + EPS) * g_ref[...]
    n2 = y * (1.0 + mod_ref[4:5, :]) + mod_ref[3:4, :]
    n2_hi = n2.astype(BF16)
    n2_ref[...] = n2_hi

    n2_lo = (n2 - n2_hi.astype(F32)).astype(BF16)
    rw = rw_ref[...]
    rw_hi = rw.astype(BF16)
    rw_lo = (rw - rw_hi.astype(F32)).astype(BF16)
    logits = _dot(n2_hi, rw_hi) + (_dot(n2_hi, rw_lo) + _dot(n2_lo, rw_hi)) + rb_ref[...]

    lane = lax.broadcasted_iota(jnp.int32, logits.shape, 1)
    work = logits
    vals, hots = [], []
    for _ in range(TOP_K):
        m = jnp.max(work, axis=-1, keepdims=True)
        idx = jnp.min(jnp.where(work == m, lane, LANES), axis=-1, keepdims=True)
        hot = lane == idx
        vals.append(m)
        hots.append(hot)
        work = jnp.where(hot, -jnp.inf, work)
    exps = [jnp.exp(v - vals[0]) for v in vals]
    den = exps[0] + exps[1] + exps[2] + exps[3]
    gates = jnp.zeros(logits.shape, F32)
    for hot, e in zip(hots, exps):
        gates = jnp.where(hot, e / den, gates)
    gates_ref[...] = gates


def mix_router(x, yh, ya, gh, ga, mod, w_out_h, w_out_a, w_o, norm_g, router_w_p, router_b_p):
    b, l, d = x.shape
    tm = _tile(l, 512)
    c = yh.shape[2]
    da = ya.shape[2]

    def tok(w):
        return pl.BlockSpec((None, tm, w), lambda bi, i: (bi, i, 0))

    def const(shape):
        return pl.BlockSpec(shape, lambda bi, i: (0,) * len(shape))

    return pl.pallas_call(
        _mix_kernel,
        grid=(b, l // tm),
        in_specs=[tok(d), tok(c), tok(da), tok(d), tok(d),
                  pl.BlockSpec((None, N_MOD, d), lambda bi, i: (bi, 0, 0)),
                  const((c, d)), const((da, d)), const((d, d)), const((1, d)),
                  const((d, LANES)), const((1, LANES))],
        out_specs=[tok(d), tok(d), tok(LANES)],
        out_shape=[jax.ShapeDtypeStruct((b, l, d), F32),
                   jax.ShapeDtypeStruct((b, l, d), BF16),
                   jax.ShapeDtypeStruct((b, l, LANES), F32)],
        compiler_params=_cparams("parallel", "parallel"),
        name="mix_router",
    )(x, yh, ya, gh, ga, mod, w_out_h, w_out_a, w_o, norm_g.reshape(1, d), router_w_p, router_b_p)


def _moe_kernel(n2_ref, gates_ref, xn_ref, mod_ref, wg_ref, wl_ref, bg_ref, bl_ref, wd_ref, bd_ref, o_ref, acc_ref):
    e = pl.program_id(2)
    gates = gates_ref[...]

    @pl.when(e == 0)
    def _():
        acc_ref[...] = _dot_hi(gates, bd_ref[...])

    n2 = n2_ref[...]
    hg = _dot(n2, wg_ref[...]) + bg_ref[...]
    hl = _dot(n2, wl_ref[...]) + bl_ref[...]
    xg = jnp.minimum(hg, SWIGLU_LIMIT)
    xl = jnp.clip(hl, -SWIGLU_LIMIT, SWIGLU_LIMIT)
    act = xg * jax.nn.sigmoid(SWIGLU_ALPHA * xg) * (xl + 1.0)
    f = act.shape[1]
    sel = (lax.broadcasted_iota(jnp.int32, (LANES, f), 0) == e).astype(BF16)
    ge = _dot(gates.astype(BF16), sel)
    acc_ref[...] += _dot((act * ge).astype(BF16), wd_ref[...])

    @pl.when(e == pl.num_programs(2) - 1)
    def _():
        o_ref[...] = xn_ref[...] + mod_ref[5:6, :] * acc_ref[...]


def moe(n2, gates, xn, mod, wg, wl, bg, bl, wd, bd_p):
    b, l, d = xn.shape
    ne, _, f = wg.shape
    tm = _tile(l, 1024)

    def tok(w):
        return pl.BlockSpec((None, tm, w), lambda bi, i, e: (bi, i, 0))

    return pl.pallas_call(
        _moe_kernel,
        grid=(b, l // tm, ne),
        in_specs=[tok(d), tok(LANES), tok(d),
                  pl.BlockSpec((None, N_MOD, d), lambda bi, i, e: (bi, 0, 0)),
                  pl.BlockSpec((None, d, f), lambda bi, i, e: (e, 0, 0)),
                  pl.BlockSpec((None, d, f), lambda bi, i, e: (e, 0, 0)),
                  pl.BlockSpec((None, 1, f), lambda bi, i, e: (e, 0, 0)),
                  pl.BlockSpec((None, 1, f), lambda bi, i, e: (e, 0, 0)),
                  pl.BlockSpec((None, f, d), lambda bi, i, e: (e, 0, 0)),
                  pl.BlockSpec((LANES, d), lambda bi, i, e: (0, 0))],
        out_specs=tok(d),
        out_shape=jax.ShapeDtypeStruct((b, l, d), F32),
        scratch_shapes=[pltpu.VMEM((tm, d), F32)],
        compiler_params=_cparams("parallel", "parallel", "arbitrary"),
        name="moe",
    )(n2, gates, xn, mod, wg, wl, bg, bl, wd, bd_p)


def _rope_tables(l):
    rows = l // GRID_W
    row = jnp.repeat(jnp.arange(rows, dtype=F32), GRID_W)
    col = jnp.tile(jnp.arange(GRID_W, dtype=F32), rows)
    n_freq = HEAD_DIM // 4
    freqs = ROPE_THETA ** (-jnp.arange(n_freq, dtype=F32) / n_freq)
    ang_r = row[:, None] * freqs
    ang_c = col[:, None] * freqs
    cos = jnp.concatenate([jnp.cos(ang_r), jnp.cos(ang_r), jnp.cos(ang_c), jnp.cos(ang_c)], axis=-1)
    sin = jnp.concatenate([-jnp.sin(ang_r), jnp.sin(ang_r), -jnp.sin(ang_c), jnp.sin(ang_c)], axis=-1)
    return cos, sin


def _encoder_layer(x, mod, p):
    b, l, d = x.shape
    c = p['filt_bias'].shape[0]
    cos, sin_signed = _rope_tables(l)
    uh, q, k, v, gh, ga = inproj(x, mod, p['norm_mix'], p['w_in'], cos, sin_signed, p['q_norm'], p['k_norm'],
                                 3 * c, N_Q_HEADS * HEAD_DIM, N_KV_HEADS * HEAD_DIM)
    x0, uu = hyena_pre(uh, p['conv_w'], p['conv_b'])
    k_filt = hyena_filter(l, p['filt_w1'], p['filt_b1'], p['filt_freq1'], p['filt_w2'], p['filt_b2'],
                          p['filt_freq2'], p['filt_w3'], p['filt_b3'])
    yh = hyena_long_conv(x0, uu, k_filt, p['filt_bias'])
    ya = attention(q, k, v)
    xn, n2, gates = mix_router(x, yh, ya, gh, ga, mod, p['w_out_h'], p['w_out_a'], p['w_o'], p['norm_ffn'],
                               p['router_w'], p['router_b'])
    return moe(n2, gates, xn, mod, p['wg'], p['wl'], p['bg'], p['bl'], p['wd'], p['bd'])


def kernel(x_prompt, x_sample, c_prompt, c_sample, w_ada, b_ada, norm_mix, w_in, conv_w, conv_b, filt_w1, filt_b1, filt_freq1, filt_w2, filt_b2, filt_freq2, filt_w3, filt_b3, filt_bias, q_norm, k_norm, w_out_h, w_out_a, w_o, norm_ffn, router_w, router_b, w_up, b_up, w_down, b_down):
    depth = w_ada.shape[0]
    d = x_prompt.shape[-1]
    bp = c_prompt.shape[0]
    bs = c_sample.shape[0]
    ne = router_w.shape[-1]
    y_prompt, y_sample = x_prompt, x_sample
    for i in range(depth):
        rows = -(-(bp + bs) // 8) * 8
        c_all = jnp.pad(jnp.concatenate([c_prompt, c_sample], axis=0), ((0, rows - bp - bs), (0, 0)))
        mod = adaln(c_all, w_ada[i], b_ada[i]).reshape(rows, N_MOD, d)
        p = {
            'norm_mix': norm_mix[i], 'w_in': w_in[i].astype(BF16),
            'conv_w': conv_w[i], 'conv_b': conv_b[i],
            'filt_w1': filt_w1[i], 'filt_b1': filt_b1[i], 'filt_freq1': filt_freq1[i],
            'filt_w2': filt_w2[i], 'filt_b2': filt_b2[i], 'filt_freq2': filt_freq2[i],
            'filt_w3': filt_w3[i], 'filt_b3': filt_b3[i], 'filt_bias': filt_bias[i],
            'q_norm': q_norm[i], 'k_norm': k_norm[i],
            'w_out_h': w_out_h[i].astype(BF16), 'w_out_a': w_out_a[i].astype(BF16), 'w_o': w_o[i].astype(BF16),
            'norm_ffn': norm_ffn[i],
            'router_w': jnp.pad(router_w[i], ((0, 0), (0, LANES - ne))),
            'router_b': jnp.pad(router_b[i], (0, LANES - ne), constant_values=NEG_BIG).reshape(1, LANES),
            'wg': w_up[i][:, :, 0::2].astype(BF16), 'wl': w_up[i][:, :, 1::2].astype(BF16),
            'bg': b_up[i][:, None, 0::2], 'bl': b_up[i][:, None, 1::2],
            'wd': w_down[i].astype(BF16),
            'bd': jnp.pad(b_down[i], ((0, LANES - ne), (0, 0))),
        }
        y_prompt = _encoder_layer(y_prompt, mod[:bp], p)
        y_sample = _encoder_layer(y_sample, mod[bp:bp + bs], p)
    return (y_prompt, y_sample)
```

```python
import functools
import math

import jax
import jax.numpy as jnp
from jax import lax
from jax.experimental import pallas as pl
from jax.experimental.pallas import tpu as pltpu

F32 = jnp.float32
BF16 = jnp.bfloat16

EPS = 1e-6
N_MOD = 6
GRID_W = 64
HEAD_DIM = 128
N_Q_HEADS = 4
N_KV_HEADS = 2
ROPE_THETA = 10000.0
TOP_K = 4
SWIGLU_ALPHA = 1.702
SWIGLU_LIMIT = 7.0
FILTER_EMB = 33
FILTER_BANDS = (FILTER_EMB - 1) // 2
DECAY_TARGET = 1e-2
MIN_DECAY = math.log(DECAY_TARGET) / 1.5
MAX_DECAY = math.log(DECAY_TARGET) / 0.3

LANES = 128
NEG_BIG = -1e30
VMEM_LIMIT = 56 * 1024 * 1024


def _cparams(*sem):
    return pltpu.CompilerParams(dimension_semantics=sem, vmem_limit_bytes=VMEM_LIMIT)


def _dot(a, b):
    return jnp.dot(a, b, preferred_element_type=F32)


def _dot_hi(a, b):
    return jnp.dot(a, b, preferred_element_type=F32, precision=lax.Precision.HIGHEST)


def _tile(n, want):
    t = min(n, want)
    assert n % t == 0, (n, want)
    return t


def _adaln_kernel(c_ref, w_ref, b_ref, o_ref):
    c = c_ref[...]
    o_ref[...] = _dot_hi(c * jax.nn.sigmoid(c), w_ref[...]) + b_ref[...]


def adaln(c, w_ada, b_ada):
    r, d = c.shape
    n = w_ada.shape[1]
    tn = _tile(n, 1536)
    return pl.pallas_call(
        _adaln_kernel,
        grid=(n // tn,),
        in_specs=[pl.BlockSpec((r, d), lambda j: (0, 0)),
                  pl.BlockSpec((d, tn), lambda j: (0, j)),
                  pl.BlockSpec((1, tn), lambda j: (0, j))],
        out_specs=pl.BlockSpec((r, tn), lambda j: (0, j)),
        out_shape=jax.ShapeDtypeStruct((r, n), F32),
        compiler_params=_cparams("arbitrary"),
        name="adaln",
    )(c, w_ada, b_ada.reshape(1, n))


def _rope(xn, cos, sin_signed):
    lane = lax.broadcasted_iota(jnp.int32, xn.shape, 1)
    first_half = (lane % 64) < 32
    rot = jnp.where(first_half, pltpu.roll(xn, 96, 1), pltpu.roll(xn, 32, 1))
    return xn * cos + rot * sin_signed


def _inproj_kernel(x_ref, mod_ref, g_ref, w_ref, cos_ref, sin_ref, qg_ref, kg_ref,
                   uh_ref, q_ref, k_ref, v_ref, gh_ref, ga_ref, *, d_hy3, d_attn, d_kv, d_model, q_scale):
    x = x_ref[...]
    y = x * lax.rsqrt(jnp.mean(x * x, axis=-1, keepdims=True) + EPS) * g_ref[...]
    n = (y * (1.0 + mod_ref[1:2, :]) + mod_ref[0:1, :]).astype(BF16)

    s0 = d_hy3
    s1 = s0 + d_attn
    s2 = s1 + d_kv
    s3 = s2 + d_kv
    s4 = s3 + d_model
    uh_ref[...] = _dot(n, w_ref[:, 0:s0]).astype(BF16)
    v_ref[...] = _dot(n, w_ref[:, s2:s3]).astype(BF16)
    gh_ref[...] = _dot(n, w_ref[:, s3:s4]).astype(BF16)
    ga_ref[...] = _dot(n, w_ref[:, s4:s4 + d_model]).astype(BF16)

    cos = cos_ref[...]
    sin = sin_ref[...]

    def norm_rope(z, gain):
        zn = z * lax.rsqrt(jnp.mean(z * z, axis=-1, keepdims=True) + EPS) * gain
        return _rope(zn, cos, sin)

    qf = _dot(n, w_ref[:, s0:s1])
    for h in range(d_attn // HEAD_DIM):
        sl = slice(h * HEAD_DIM, (h + 1) * HEAD_DIM)
        q_ref[:, sl] = (norm_rope(qf[:, sl], qg_ref[...]) * q_scale).astype(BF16)
    kf = _dot(n, w_ref[:, s1:s2])
    for h in range(d_kv // HEAD_DIM):
        sl = slice(h * HEAD_DIM, (h + 1) * HEAD_DIM)
        k_ref[:, sl] = norm_rope(kf[:, sl], kg_ref[...]).astype(BF16)


def inproj(x, mod, norm_g, w_in_bf, cos, sin_signed, q_norm, k_norm, d_hy3, d_attn, d_kv):
    b, l, d = x.shape
    tm = _tile(l, 512)
    d_in = w_in_bf.shape[1]
    q_scale = math.log2(math.e) / math.sqrt(HEAD_DIM)
    kern = functools.partial(_inproj_kernel, d_hy3=d_hy3, d_attn=d_attn, d_kv=d_kv, d_model=d, q_scale=q_scale)

    def tok(w):
        return pl.BlockSpec((None, tm, w), lambda bi, i: (bi, i, 0))

    def const(shape):
        return pl.BlockSpec(shape, lambda bi, i: (0,) * len(shape))

    outs = [d_hy3, d_attn, d_kv, d_kv, d, d]
    return pl.pallas_call(
        kern,
        grid=(b, l // tm),
        in_specs=[tok(d),
                  pl.BlockSpec((None, N_MOD, d), lambda bi, i: (bi, 0, 0)),
                  const((1, d)),
                  const((d, d_in)),
                  pl.BlockSpec((tm, HEAD_DIM), lambda bi, i: (i, 0)),
                  pl.BlockSpec((tm, HEAD_DIM), lambda bi, i: (i, 0)),
                  const((1, HEAD_DIM)),
                  const((1, HEAD_DIM))],
        out_specs=[tok(w) for w in outs],
        out_shape=[jax.ShapeDtypeStruct((b, l, w), BF16) for w in outs],
        compiler_params=_cparams("parallel", "parallel"),
        name="inproj",
    )(x, mod, norm_g.reshape(1, d), w_in_bf, cos, sin_signed, q_norm.reshape(1, HEAD_DIM), k_norm.reshape(1, HEAD_DIM))


def _hyena_pre_kernel(u_ref, prev_ref, next_ref, w_ref, b_ref, x0_ref, uu_ref, *, c):
    i = pl.program_id(1)
    nblk = pl.num_programs(1)
    u = u_ref[...].astype(F32)
    tl = u.shape[0]
    row = lax.broadcasted_iota(jnp.int32, u.shape, 0)
    prev_row = prev_ref[7:8, :].astype(F32) * jnp.where(i > 0, 1.0, 0.0)
    next_row = next_ref[0:1, :].astype(F32) * jnp.where(i < nblk - 1, 1.0, 0.0)
    up = jnp.where(row == 0, prev_row, pltpu.roll(u, 1, 0))
    un = jnp.where(row == tl - 1, next_row, pltpu.roll(u, tl - 1, 0))
    y = up * w_ref[0:1, :] + u * w_ref[1:2, :] + un * w_ref[2:3, :] + b_ref[...]
    x0_ref[...] = y[:, 0:c].astype(BF16)
    uu_ref[...] = (y[:, c:2 * c] * y[:, 2 * c:3 * c]).astype(BF16)


def hyena_pre(uh, conv_w, conv_b):
    b, l, c3 = uh.shape
    c = c3 // 3
    tl = _tile(l, 512)
    r = tl // 8
    nrow8 = l // 8
    kern = functools.partial(_hyena_pre_kernel, c=c)
    return pl.pallas_call(
        kern,
        grid=(b, l // tl),
        in_specs=[pl.BlockSpec((None, tl, c3), lambda bi, i: (bi, i, 0)),
                  pl.BlockSpec((None, 8, c3), lambda bi, i: (bi, jnp.maximum(i * r - 1, 0), 0)),
                  pl.BlockSpec((None, 8, c3), lambda bi, i: (bi, jnp.minimum((i + 1) * r, nrow8 - 1), 0)),
                  pl.BlockSpec((3, c3), lambda bi, i: (0, 0)),
                  pl.BlockSpec((1, c3), lambda bi, i: (0, 0))],
        out_specs=[pl.BlockSpec((None, tl, c), lambda bi, i: (bi, i, 0)),
                   pl.BlockSpec((None, tl, c), lambda bi, i: (bi, i, 0))],
        out_shape=[jax.ShapeDtypeStruct((b, l, c), BF16)] * 2,
        compiler_params=_cparams("parallel", "parallel"),
        name="hyena_pre",
    )(uh, uh, uh, conv_w, conv_b.reshape(1, c3))


def _filter_kernel(z_ref, t_ref, w1_ref, b1_ref, f1_ref, w2_ref, b2_ref, f2_ref, w3_ref, b3_ref, dl_ref, o_ref):
    h = jnp.sin(f1_ref[...] * (_dot_hi(z_ref[...], w1_ref[...]) + b1_ref[...]))
    h = jnp.sin(f2_ref[...] * (_dot_hi(h, w2_ref[...]) + b2_ref[...]))
    k = _dot_hi(h, w3_ref[...]) + b3_ref[...]
    o_ref[...] = k * jnp.exp(-t_ref[:, 0:1] * dl_ref[...])


def hyena_filter(l, w1, b1, f1, w2, b2, f2, w3, b3):
    c2 = w3.shape[1]
    c = c2 // 2
    hid = w1.shape[1]
    t = jnp.linspace(0.0, 1.0, l, dtype=F32)[:, None]
    w = (2.0 * math.pi / l) * jnp.arange(l, dtype=F32)[:, None]
    bands = jnp.linspace(1e-4, FILTER_BANDS - 1, FILTER_BANDS, dtype=F32)[None, :]
    z = jnp.concatenate([t, jnp.cos(bands * w), -jnp.sin(bands * w)], axis=-1)
    z = jnp.pad(z, ((0, 0), (0, LANES - FILTER_EMB)))
    tlane = jnp.broadcast_to(t, (l, LANES))

    def padv(v):
        return jnp.pad(v.astype(F32), (0, LANES - hid)).reshape(1, LANES)

    w1p = jnp.pad(w1.astype(F32), ((0, LANES - FILTER_EMB), (0, LANES - hid)))
    w2p = jnp.pad(w2.astype(F32), ((0, LANES - hid), (0, LANES - hid)))
    w3p = jnp.pad(w3.astype(F32), ((0, LANES - hid), (0, 0)))
    deltas = jnp.abs(jnp.linspace(MIN_DECAY, MAX_DECAY, c, dtype=F32))
    dl = jnp.concatenate([deltas, deltas]).reshape(1, c2)
    tl = _tile(l, 1024)

    def const(shape):
        return pl.BlockSpec(shape, lambda i: (0, 0))

    return pl.pallas_call(
        _filter_kernel,
        grid=(l // tl,),
        in_specs=[pl.BlockSpec((tl, LANES), lambda i: (i, 0)),
                  pl.BlockSpec((tl, LANES), lambda i: (i, 0)),
                  const((LANES, LANES)), const((1, LANES)), const((1, LANES)),
                  const((LANES, LANES)), const((1, LANES)), const((1, LANES)),
                  const((LANES, c2)), const((1, c2)), const((1, c2))],
        out_specs=pl.BlockSpec((tl, c2), lambda i: (i, 0)),
        out_shape=jax.ShapeDtypeStruct((l, c2), F32),
        compiler_params=_cparams("parallel"),
        name="hyena_filter",
    )(z, tlane, w1p, padv(b1), padv(f1), w2p, padv(b2), padv(f2), w3p, b3.astype(F32).reshape(1, c2), dl)


def _fft_split(n):
    n2 = 128
    n1 = n // n2
    assert n1 * n2 == n and n1 % 16 == 0, n
    return n1, n2


def _angles(rows, cols, period):
    prod = (jnp.arange(rows, dtype=jnp.int32)[:, None] * jnp.arange(cols, dtype=jnp.int32)[None, :]) % period
    return prod.astype(F32) * (2.0 * math.pi / period)


def _dft_tables(n1, n2):
    n = n1 * n2
    ang1 = _angles(n1, n1, n1)
    c1, s1 = jnp.cos(ang1), jnp.sin(ang1)
    fa = jnp.concatenate([c1, -s1], axis=0).astype(BF16)
    fc = (jnp.concatenate([c1, -s1], axis=1) * (1.0 / n)).astype(BF16)
    ang2 = _angles(n2, n2, n2)
    c2, s2 = jnp.cos(ang2), jnp.sin(ang2)
    g_fwd = jnp.block([[c2, s2], [-s2, c2]]).astype(BF16)
    g_inv = jnp.block([[c2, -s2], [s2, c2]]).astype(BF16)
    angt = _angles(n1, n2, n)
    twr = jnp.broadcast_to(jnp.cos(angt)[:, :, None], (n1, n2, LANES))
    twi = jnp.broadcast_to(jnp.sin(angt)[:, :, None], (n1, n2, LANES))
    return fa, fc, g_fwd, g_inv, twr, twi


def _fft_a_kernel(f_ref, u_ref, o_ref):
    o_ref[...] = _dot(f_ref[...], u_ref[...]).astype(BF16)


def fft_a(fa, u):
    b, k, w = u.shape
    m = fa.shape[0]
    tn = _tile(w, 2048)
    return pl.pallas_call(
        _fft_a_kernel,
        grid=(b, w // tn),
        in_specs=[pl.BlockSpec((m, k), lambda bi, j: (0, 0)),
                  pl.BlockSpec((None, k, tn), lambda bi, j: (bi, 0, j))],
        out_specs=pl.BlockSpec((None, m, tn), lambda bi, j: (bi, 0, j)),
        out_shape=jax.ShapeDtypeStruct((b, m, w), BF16),
        compiler_params=_cparams("parallel", "parallel"),
        name="fft_a",
    )(fa, u)


def _lane_tile(t, c):
    return jnp.concatenate([t] * (c // LANES), axis=1) if c > LANES else t


def _fft_mf_kernel(a_ref, twr_ref, twi_ref, g_ref, o_ref, *, tk1):
    for j in range(tk1):
        ar = a_ref[0, j].astype(F32)
        ai = a_ref[1, j].astype(F32)
        c = ar.shape[1]
        twr = _lane_tile(twr_ref[j], c)
        twi = _lane_tile(twi_ref[j], c)
        br = ar * twr + ai * twi
        bi = ai * twr - ar * twi
        x = _dot(g_ref[...], jnp.concatenate([br, bi], axis=0).astype(BF16))
        n2 = ar.shape[0]
        o_ref[0, j] = x[:n2].astype(BF16)
        o_ref[1, j] = x[n2:].astype(BF16)


def fft_mf(a, twr, twi, g_fwd):
    _, n1, n2, c = a.shape
    tk1 = 8
    kern = functools.partial(_fft_mf_kernel, tk1=tk1)
    return pl.pallas_call(
        kern,
        grid=(n1 // tk1,),
        in_specs=[pl.BlockSpec((2, tk1, n2, c), lambda i: (0, i, 0, 0)),
                  pl.BlockSpec((tk1, n2, LANES), lambda i: (i, 0, 0)),
                  pl.BlockSpec((tk1, n2, LANES), lambda i: (i, 0, 0)),
                  pl.BlockSpec((2 * n2, 2 * n2), lambda i: (0, 0))],
        out_specs=pl.BlockSpec((2, tk1, n2, c), lambda i: (0, i, 0, 0)),
        out_shape=jax.ShapeDtypeStruct((2, n1, n2, c), BF16),
        compiler_params=_cparams("parallel"),
        name="fft_mf",
    )(a, twr, twi, g_fwd)


def _fft_m_kernel(a_ref, kf_ref, twr_ref, twi_ref, gf_ref, gi_ref, o_ref, *, tk1):
    for j in range(tk1):
        ar = a_ref[0, j].astype(F32)
        ai = a_ref[1, j].astype(F32)
        n2, c = ar.shape
        twr = _lane_tile(twr_ref[j], c)
        twi = _lane_tile(twi_ref[j], c)
        br = ar * twr + ai * twi
        bi = ai * twr - ar * twi
        x = _dot(gf_ref[...], jnp.concatenate([br, bi], axis=0).astype(BF16))
        xr, xi = x[:n2], x[n2:]
        kr = kf_ref[0, j].astype(F32)
        ki = kf_ref[1, j].astype(F32)
        zr = xr * kr - xi * ki
        zi = xr * ki + xi * kr
        y = _dot(gi_ref[...], jnp.concatenate([zr, zi], axis=0).astype(BF16))
        yr, yi = y[:n2], y[n2:]
        o_ref[0, j] = (yr * twr - yi * twi).astype(BF16)
        o_ref[1, j] = (yi * twr + yr * twi).astype(BF16)


def fft_m(a, kf, twr, twi, g_fwd, g_inv):
    b, _, n1, n2, c = a.shape
    tk1 = 8
    kern = functools.partial(_fft_m_kernel, tk1=tk1)
    return pl.pallas_call(
        kern,
        grid=(n1 // tk1, b),
        in_specs=[pl.BlockSpec((None, 2, tk1, n2, c), lambda i, bi: (bi, 0, i, 0, 0)),
                  pl.BlockSpec((2, tk1, n2, c), lambda i, bi: (0, i, 0, 0)),
                  pl.BlockSpec((tk1, n2, LANES), lambda i, bi: (i, 0, 0)),
                  pl.BlockSpec((tk1, n2, LANES), lambda i, bi: (i, 0, 0)),
                  pl.BlockSpec((2 * n2, 2 * n2), lambda i, bi: (0, 0)),
                  pl.BlockSpec((2 * n2, 2 * n2), lambda i, bi: (0, 0))],
        out_specs=pl.BlockSpec((None, 2, tk1, n2, c), lambda i, bi: (bi, 0, i, 0, 0)),
        out_shape=jax.ShapeDtypeStruct(a.shape, BF16),
        compiler_params=_cparams("parallel", "parallel"),
        name="fft_m",
    )(a, kf, twr, twi, g_fwd, g_inv)


def _fft_c_kernel(f_ref, a_ref, x0_ref, uu_ref, bias_ref, o_ref):
    y = _dot(f_ref[...], a_ref[...])
    y = y + uu_ref[...].astype(F32) * bias_ref[...]
    o_ref[...] = (x0_ref[...].astype(F32) * y).astype(BF16)


def fft_c(fc_half, a2, x0, uu, bias_tiled):
    b, k, w = a2.shape
    m = fc_half.shape[0]
    tn = bias_tiled.shape[1]
    return pl.pallas_call(
        _fft_c_kernel,
        grid=(b, w // tn),
        in_specs=[pl.BlockSpec((m, k), lambda bi, j: (0, 0)),
                  pl.BlockSpec((None, k, tn), lambda bi, j: (bi, 0, j)),
                  pl.BlockSpec((None, m, tn), lambda bi, j: (bi, 0, j)),
                  pl.BlockSpec((None, m, tn), lambda bi, j: (bi, 0, j)),
                  pl.BlockSpec((1, tn), lambda bi, j: (0, 0))],
        out_specs=pl.BlockSpec((None, m, tn), lambda bi, j: (bi, 0, j)),
        out_shape=jax.ShapeDtypeStruct((b, m, w), BF16),
        compiler_params=_cparams("parallel", "parallel"),
        name="fft_c",
    )(fc_half, a2, x0, uu, bias_tiled)


def hyena_long_conv(x0, uu, k_filt, filt_bias):
    b, l, c = uu.shape
    n = 2 * l
    n1, n2 = _fft_split(n)
    fa, fc, g_fwd, g_inv, twr, twi = _dft_tables(n1, n2)
    w = n2 * c
    k_circ = jnp.concatenate([k_filt[:, :c], jnp.zeros((1, c), F32), jnp.flip(k_filt[1:, c:], axis=0)], axis=0)
    ka = fft_a(fa, k_circ.astype(BF16).reshape(1, n1, w))
    kf = fft_mf(ka.reshape(2, n1, n2, c), twr, twi, g_fwd)
    a = fft_a(fa[:, :n1 // 2], uu.reshape(b, n1 // 2, w))
    a2 = fft_m(a.reshape(b, 2, n1, n2, c), kf, twr, twi, g_fwd, g_inv)
    tn = _tile(w, 2048)
    bias_tiled = jnp.tile(filt_bias.astype(F32), tn // c).reshape(1, tn)
    yh = fft_c(fc[:n1 // 2], a2.reshape(b, 2 * n1, w), x0.reshape(b, n1 // 2, w), uu.reshape(b, n1 // 2, w),
               bias_tiled)
    return yh.reshape(b, l, c)


def _attn_kernel(q_ref, k_ref, v_ref, o_ref, qs_ref, m_ref, acc_ref, *, tq, tk, nsub):
    ki = pl.program_id(3)

    @pl.when(ki == 0)
    def _():
        qs_ref[0:tq, :] = q_ref[:, 0:HEAD_DIM]
        qs_ref[tq:2 * tq, :] = q_ref[:, HEAD_DIM:2 * HEAD_DIM]
        m_ref[...] = jnp.full(m_ref.shape, -jnp.inf, F32)
        acc_ref[...] = jnp.zeros(acc_ref.shape, F32)

    qs = qs_ref[...]
    for j in range(nsub):
        k = k_ref[j * tk:(j + 1) * tk, :]
        v = v_ref[j * tk:(j + 1) * tk, :]
        v_aug = jnp.concatenate([v, jnp.ones_like(v)], axis=1)
        s = lax.dot_general(qs, k, (((1,), (1,)), ((), ())), preferred_element_type=F32)
        m_prev = m_ref[...]
        m_new = jnp.maximum(m_prev, jnp.max(s, axis=-1, keepdims=True))
        alpha = jnp.exp2(m_prev - m_new)
        p = jnp.exp2((s - pltpu.repeat(m_new, tk // LANES, 1)).astype(BF16))
        acc_ref[...] = jnp.concatenate([alpha, alpha], axis=1) * acc_ref[...] + _dot(p, v_aug)
        m_ref[...] = m_new

    @pl.when(ki == pl.num_programs(3) - 1)
    def _():
        o = acc_ref[:, 0:HEAD_DIM] / acc_ref[:, HEAD_DIM:2 * HEAD_DIM]
        o_ref[:, 0:HEAD_DIM] = o[0:tq].astype(BF16)
        o_ref[:, HEAD_DIM:2 * HEAD_DIM] = o[tq:2 * tq].astype(BF16)


def attention(q, k, v):
    b, l, _ = q.shape
    tq = _tile(l, 512)
    tk = _tile(l, 512)
    nsub = _tile(l // tk, 4)
    g = (N_Q_HEADS // N_KV_HEADS) * HEAD_DIM
    kern = functools.partial(_attn_kernel, tq=tq, tk=tk, nsub=nsub)
    tkb = tk * nsub
    return pl.pallas_call(
        kern,
        grid=(b, N_KV_HEADS, l // tq, l // tkb),
        in_specs=[pl.BlockSpec((None, tq, g), lambda bi, h, i, j: (bi, i, h)),
                  pl.BlockSpec((None, tkb, HEAD_DIM), lambda bi, h, i, j: (bi, j, h)),
                  pl.BlockSpec((None, tkb, HEAD_DIM), lambda bi, h, i, j: (bi, j, h))],
        out_specs=pl.BlockSpec((None, tq, g), lambda bi, h, i, j: (bi, i, h)),
        out_shape=jax.ShapeDtypeStruct(q.shape, BF16),
        scratch_shapes=[pltpu.VMEM((2 * tq, HEAD_DIM), BF16),
                        pltpu.VMEM((2 * tq, LANES), F32),
                        pltpu.VMEM((2 * tq, 2 * HEAD_DIM), F32)],
        compiler_params=_cparams("parallel", "parallel", "parallel", "arbitrary"),
        name="attention",
    )(q, k, v)


def _mix_kernel(x_ref, yh_ref, ya_ref, gh_ref, ga_ref, mod_ref, woh_ref, woa_ref, wo_ref, g_ref, rw_ref, rb_ref,
                xn_ref, n2_ref, gates_ref):
    th = _dot(yh_ref[...], woh_ref[...])
    ta = _dot(ya_ref[...], woa_ref[...])
    mixed = (jax.nn.sigmoid(gh_ref[...].astype(F32)) * th + jax.nn.sigmoid(ga_ref[...].astype(F32)) * ta)
    mix = _dot(mixed.astype(BF16), wo_ref[...])
    x = x_ref[...] + mod_ref[2:3, :] * mix
    xn_ref[...] = x

    y = x * lax.rsqrt(jnp.mean(x * x, axis=-1, keepdims=True) + EPS) * g_ref[...]
    n2 = y * (1.0 + mod_ref[4:5, :]) + mod_ref[3:4, :]
    n2_hi = n2.astype(BF16)
    n2_ref[...] = n2_hi

    n2_lo = (n2 - n2_hi.astype(F32)).astype(BF16)
    rw = rw_ref[...]
    rw_hi = rw.astype(BF16)
    rw_lo = (rw - rw_hi.astype(F32)).astype(BF16)
    logits = _dot(n2_hi, rw_hi) + (_dot(n2_hi, rw_lo) + _dot(n2_lo, rw_hi)) + rb_ref[...]

    lane = lax.broadcasted_iota(jnp.int32, logits.shape, 1)
    work = logits
    vals, hots = [], []
    for _ in range(TOP_K):
        m = jnp.max(work, axis=-1, keepdims=True)
        idx = jnp.min(jnp.where(work == m, lane, LANES), axis=-1, keepdims=True)
        hot = lane == idx
        vals.append(m)
        hots.append(hot)
        work = jnp.where(hot, -jnp.inf, work)
    exps = [jnp.exp(v - vals[0]) for v in vals]
    den = exps[0] + exps[1] + exps[2] + exps[3]
    gates = jnp.zeros(logits.shape, F32)
    for hot, e in zip(hots, exps):
        gates = jnp.where(hot, e / den, gates)
    gates_ref[...] = gates


def mix_router(x, yh, ya, gh, ga, mod, w_out_h, w_out_a, w_o, norm_g, router_w_p, router_b_p):
    b, l, d = x.shape
    tm = _tile(l, 512)
    c = yh.shape[2]
    da = ya.shape[2]

    def tok(w):
        return pl.BlockSpec((None, tm, w), lambda bi, i: (bi, i, 0))

    def const(shape):
        return pl.BlockSpec(shape, lambda bi, i: (0,) * len(shape))

    return pl.pallas_call(
        _mix_kernel,
        grid=(b, l // tm),
        in_specs=[tok(d), tok(c), tok(da), tok(d), tok(d),
                  pl.BlockSpec((None, N_MOD, d), lambda bi, i: (bi, 0, 0)),
                  const((c, d)), const((da, d)), const((d, d)), const((1, d)),
                  const((d, LANES)), const((1, LANES))],
        out_specs=[tok(d), tok(d), tok(LANES)],
        out_shape=[jax.ShapeDtypeStruct((b, l, d), F32),
                   jax.ShapeDtypeStruct((b, l, d), BF16),
                   jax.ShapeDtypeStruct((b, l, LANES), F32)],
        compiler_params=_cparams("parallel", "parallel"),
        name="mix_router",
    )(x, yh, ya, gh, ga, mod, w_out_h, w_out_a, w_o, norm_g.reshape(1, d), router_w_p, router_b_p)


def _moe_kernel(n2_ref, gates_ref, xn_ref, mod_ref, wg_ref, wl_ref, bg_ref, bl_ref, wd_ref, bd_ref, o_ref, acc_ref):
    e = pl.program_id(2)
    gates = gates_ref[...]

    @pl.when(e == 0)
    def _():
        acc_ref[...] = _dot_hi(gates, bd_ref[...])

    n2 = n2_ref[...]
    hg = _dot(n2, wg_ref[...]) + bg_ref[...]
    hl = _dot(n2, wl_ref[...]) + bl_ref[...]
    xg = jnp.minimum(hg, SWIGLU_LIMIT)
    xl = jnp.clip(hl, -SWIGLU_LIMIT, SWIGLU_LIMIT)
    act = xg * jax.nn.sigmoid(SWIGLU_ALPHA * xg) * (xl + 1.0)
    f = act.shape[1]
    sel = (lax.broadcasted_iota(jnp.int32, (LANES, f), 0) == e).astype(BF16)
    ge = _dot(gates.astype(BF16), sel)
    acc_ref[...] += _dot((act * ge).astype(BF16), wd_ref[...])

    @pl.when(e == pl.num_programs(2) - 1)
    def _():
        o_ref[...] = xn_ref[...] + mod_ref[5:6, :] * acc_ref[...]


def moe(n2, gates, xn, mod, wg, wl, bg, bl, wd, bd_p):
    b, l, d = xn.shape
    ne, _, f = wg.shape
    tm = _tile(l, 1024)

    def tok(w):
        return pl.BlockSpec((None, tm, w), lambda bi, i, e: (bi, i, 0))

    return pl.pallas_call(
        _moe_kernel,
        grid=(b, l // tm, ne),
        in_specs=[tok(d), tok(LANES), tok(d),
                  pl.BlockSpec((None, N_MOD, d), lambda bi, i, e: (bi, 0, 0)),
                  pl.BlockSpec((None, d, f), lambda bi, i, e: (e, 0, 0)),
                  pl.BlockSpec((None, d, f), lambda bi, i, e: (e, 0, 0)),
                  pl.BlockSpec((None, 1, f), lambda bi, i, e: (e, 0, 0)),
                  pl.BlockSpec((None, 1, f), lambda bi, i, e: (e, 0, 0)),
                  pl.BlockSpec((None, f, d), lambda bi, i, e: (e, 0, 0)),
                  pl.BlockSpec((LANES, d), lambda bi, i, e: (0, 0))],
        out_specs=tok(d),
        out_shape=jax.ShapeDtypeStruct((b, l, d), F32),
        scratch_shapes=[pltpu.VMEM((tm, d), F32)],
        compiler_params=_cparams("parallel", "parallel", "arbitrary"),
        name="moe",
    )(n2, gates, xn, mod, wg, wl, bg, bl, wd, bd_p)


def _rope_tables(l):
    rows = l // GRID_W
    row = jnp.repeat(jnp.arange(rows, dtype=F32), GRID_W)
    col = jnp.tile(jnp.arange(GRID_W, dtype=F32), rows)
    n_freq = HEAD_DIM // 4
    freqs = ROPE_THETA ** (-jnp.arange(n_freq, dtype=F32) / n_freq)
    ang_r = row[:, None] * freqs
    ang_c = col[:, None] * freqs
    cos = jnp.concatenate([jnp.cos(ang_r), jnp.cos(ang_r), jnp.cos(ang_c), jnp.cos(ang_c)], axis=-1)
    sin = jnp.concatenate([-jnp.sin(ang_r), jnp.sin(ang_r), -jnp.sin(ang_c), jnp.sin(ang_c)], axis=-1)
    return cos, sin


def _encoder_layer(x, mod, p):
    b, l, d = x.shape
    c = p['filt_bias'].shape[0]
    cos, sin_signed = _rope_tables(l)
    uh, q, k, v, gh, ga = inproj(x, mod, p['norm_mix'], p['w_in'], cos, sin_signed, p['q_norm'], p['k_norm'],
                                 3 * c, N_Q_HEADS * HEAD_DIM, N_KV_HEADS * HEAD_DIM)
    x0, uu = hyena_pre(uh, p['conv_w'], p['conv_b'])
    k_filt = hyena_filter(l, p['filt_w1'], p['filt_b1'], p['filt_freq1'], p['filt_w2'], p['filt_b2'],
                          p['filt_freq2'], p['filt_w3'], p['filt_b3'])
    yh = hyena_long_conv(x0, uu, k_filt, p['filt_bias'])
    ya = attention(q, k, v)
    xn, n2, gates = mix_router(x, yh, ya, gh, ga, mod, p['w_out_h'], p['w_out_a'], p['w_o'], p['norm_ffn'],
                               p['router_w'], p['router_b'])
    return moe(n2, gates, xn, mod, p['wg'], p['wl'], p['bg'], p['bl'], p['wd'], p['bd'])


def kernel(x_prompt, x_sample, c_prompt, c_sample, w_ada, b_ada, norm_mix, w_in, conv_w, conv_b, filt_w1, filt_b1, filt_freq1, filt_w2, filt_b2, filt_freq2, filt_w3, filt_b3, filt_bias, q_norm, k_norm, w_out_h, w_out_a, w_o, norm_ffn, router_w, router_b, w_up, b_up, w_down, b_down):
    depth = w_ada.shape[0]
    d = x_prompt.shape[-1]
    bp = c_prompt.shape[0]
    bs = c_sample.shape[0]
    ne = router_w.shape[-1]
    y_prompt, y_sample = x_prompt, x_sample
    for i in range(depth):
        rows = -(-(bp + bs) // 8) * 8
        c_all = jnp.pad(jnp.concatenate([c_prompt, c_sample], axis=0), ((0, rows - bp - bs), (0, 0)))
        mod = adaln(c_all, w_ada[i], b_ada[i]).reshape(rows, N_MOD, d)
        p = {
            'norm_mix': norm_mix[i], 'w_in': w_in[i].astype(BF16),
            'conv_w': conv_w[i], 'conv_b': conv_b[i],
            'filt_w1': filt_w1[i], 'filt_b1': filt_b1[i], 'filt_freq1': filt_freq1[i],
            'filt_w2': filt_w2[i], 'filt_b2': filt_b2[i], 'filt_freq2': filt_freq2[i],
            'filt_w3': filt_w3[i], 'filt_b3': filt_b3[i], 'filt_bias': filt_bias[i],
            'q_norm': q_norm[i], 'k_norm': k_norm[i],
            'w_out_h': w_out_h[i].astype(BF16), 'w_out_a': w_out_a[i].astype(BF16), 'w_o': w_o[i].astype(BF16),
            'norm_ffn': norm_ffn[i],
            'router_w': jnp.pad(router_w[i], ((0, 0), (0, LANES - ne))),
            'router_b': jnp.pad(router_b[i], (0, LANES - ne), constant_values=NEG_BIG).reshape(1, LANES),
            'wg': w_up[i][:, :, 0::2].astype(BF16), 'wl': w_up[i][:, :, 1::2].astype(BF16),
            'bg': b_up[i][:, None, 0::2], 'bl': b_up[i][:, None, 1::2],
            'wd': w_down[i].astype(BF16),
            'bd': jnp.pad(b_down[i], ((0, LANES - ne), (0, 0))),
        }
        y_prompt = _encoder_layer(y_prompt, mod[:bp], p)
        y_sample = _encoder_layer(y_sample, mod[bp:bp + bs], p)
    return (y_prompt, y_sample)
```

```python
import functools
import math

import jax
import jax.numpy as jnp
from jax import lax
from jax.experimental import pallas as pl
from jax.experimental.pallas import tpu as pltpu

F32 = jnp.float32
BF16 = jnp.bfloat16

EPS = 1e-6
N_MOD = 6
GRID_W = 64
HEAD_DIM = 128
N_Q_HEADS = 4
N_KV_HEADS = 2
ROPE_THETA = 10000.0
TOP_K = 4
SWIGLU_ALPHA = 1.702
SWIGLU_LIMIT = 7.0
FILTER_EMB = 33
FILTER_BANDS = (FILTER_EMB - 1) // 2
DECAY_TARGET = 1e-2
MIN_DECAY = math.log(DECAY_TARGET) / 1.5
MAX_DECAY = math.log(DECAY_TARGET) / 0.3

LANES = 128
NEG_BIG = -1e30
VMEM_LIMIT = 56 * 1024 * 1024


def _cparams(*sem):
    return pltpu.CompilerParams(dimension_semantics=sem, vmem_limit_bytes=VMEM_LIMIT)


def _dot(a, b):
    return jnp.dot(a, b, preferred_element_type=F32)


def _dot_hi(a, b):
    return jnp.dot(a, b, preferred_element_type=F32, precision=lax.Precision.HIGHEST)


def _tile(n, want):
    t = min(n, want)
    assert n % t == 0, (n, want)
    return t


def _adaln_kernel(c_ref, w_ref, b_ref, o_ref):
    c = c_ref[...]
    o_ref[...] = _dot_hi(c * jax.nn.sigmoid(c), w_ref[...]) + b_ref[...]


def adaln(c, w_ada, b_ada):
    r, d = c.shape
    n = w_ada.shape[1]
    tn = _tile(n, 1536)
    return pl.pallas_call(
        _adaln_kernel,
        grid=(n // tn,),
        in_specs=[pl.BlockSpec((r, d), lambda j: (0, 0)),
                  pl.BlockSpec((d, tn), lambda j: (0, j)),
                  pl.BlockSpec((1, tn), lambda j: (0, j))],
        out_specs=pl.BlockSpec((r, tn), lambda j: (0, j)),
        out_shape=jax.ShapeDtypeStruct((r, n), F32),
        compiler_params=_cparams("arbitrary"),
        name="adaln",
    )(c, w_ada, b_ada.reshape(1, n))


def _rope(xn, cos, sin_signed):
    lane = lax.broadcasted_iota(jnp.int32, xn.shape, 1)
    first_half = (lane % 64) < 32
    rot = jnp.where(first_half, pltpu.roll(xn, 96, 1), pltpu.roll(xn, 32, 1))
    return xn * cos + rot * sin_signed


def _inproj_kernel(x_ref, mod_ref, g_ref, w_ref, cos_ref, sin_ref, qg_ref, kg_ref,
                   uh_ref, q_ref, k_ref, v_ref, gh_ref, ga_ref, *, d_hy3, d_attn, d_kv, d_model, q_scale):
    x = x_ref[...]
    y = x * lax.rsqrt(jnp.mean(x * x, axis=-1, keepdims=True) + EPS) * g_ref[...]
    n = (y * (1.0 + mod_ref[1:2, :]) + mod_ref[0:1, :]).astype(BF16)

    s0 = d_hy3
    s1 = s0 + d_attn
    s2 = s1 + d_kv
    s3 = s2 + d_kv
    s4 = s3 + d_model
    uh_ref[...] = _dot(n, w_ref[:, 0:s0]).astype(BF16)
    v_ref[...] = _dot(n, w_ref[:, s2:s3]).astype(BF16)
    gh_ref[...] = _dot(n, w_ref[:, s3:s4]).astype(BF16)
    ga_ref[...] = _dot(n, w_ref[:, s4:s4 + d_model]).astype(BF16)

    cos = cos_ref[...]
    sin = sin_ref[...]

    def norm_rope(z, gain):
        zn = z * lax.rsqrt(jnp.mean(z * z, axis=-1, keepdims=True) + EPS) * gain
        return _rope(zn, cos, sin)

    qf = _dot(n, w_ref[:, s0:s1])
    for h in range(d_attn // HEAD_DIM):
        sl = slice(h * HEAD_DIM, (h + 1) * HEAD_DIM)
        q_ref[:, sl] = (norm_rope(qf[:, sl], qg_ref[...]) * q_scale).astype(BF16)
    kf = _dot(n, w_ref[:, s1:s2])
    for h in range(d_kv // HEAD_DIM):
        sl = slice(h * HEAD_DIM, (h + 1) * HEAD_DIM)
        k_ref[:, sl] = norm_rope(kf[:, sl], kg_ref[...]).astype(BF16)


def inproj(x, mod, norm_g, w_in_bf, cos, sin_signed, q_norm, k_norm, d_hy3, d_attn, d_kv):
    b, l, d = x.shape
    tm = _tile(l, 512)
    d_in = w_in_bf.shape[1]
    q_scale = math.log2(math.e) / math.sqrt(HEAD_DIM)
    kern = functools.partial(_inproj_kernel, d_hy3=d_hy3, d_attn=d_attn, d_kv=d_kv, d_model=d, q_scale=q_scale)

    def tok(w):
        return pl.BlockSpec((None, tm, w), lambda bi, i: (bi, i, 0))

    def const(shape):
        return pl.BlockSpec(shape, lambda bi, i: (0,) * len(shape))

    outs = [d_hy3, d_attn, d_kv, d_kv, d, d]
    return pl.pallas_call(
        kern,
        grid=(b, l // tm),
        in_specs=[tok(d),
                  pl.BlockSpec((None, N_MOD, d), lambda bi, i: (bi, 0, 0)),
                  const((1, d)),
                  const((d, d_in)),
                  pl.BlockSpec((tm, HEAD_DIM), lambda bi, i: (i, 0)),
                  pl.BlockSpec((tm, HEAD_DIM), lambda bi, i: (i, 0)),
                  const((1, HEAD_DIM)),
                  const((1, HEAD_DIM))],
        out_specs=[tok(w) for w in outs],
        out_shape=[jax.ShapeDtypeStruct((b, l, w), BF16) for w in outs],
        compiler_params=_cparams("parallel", "parallel"),
        name="inproj",
    )(x, mod, norm_g.reshape(1, d), w_in_bf, cos, sin_signed, q_norm.reshape(1, HEAD_DIM), k_norm.reshape(1, HEAD_DIM))


def _hyena_pre_kernel(u_ref, prev_ref, next_ref, w_ref, b_ref, x0_ref, uu_ref, *, c):
    i = pl.program_id(1)
    nblk = pl.num_programs(1)
    u = u_ref[...].astype(F32)
    tl = u.shape[0]
    row = lax.broadcasted_iota(jnp.int32, u.shape, 0)
    prev_row = prev_ref[7:8, :].astype(F32) * jnp.where(i > 0, 1.0, 0.0)
    next_row = next_ref[0:1, :].astype(F32) * jnp.where(i < nblk - 1, 1.0, 0.0)
    up = jnp.where(row == 0, prev_row, pltpu.roll(u, 1, 0))
    un = jnp.where(row == tl - 1, next_row, pltpu.roll(u, tl - 1, 0))
    y = up * w_ref[0:1, :] + u * w_ref[1:2, :] + un * w_ref[2:3, :] + b_ref[...]
    x0_ref[...] = y[:, 0:c].astype(BF16)
    uu_ref[...] = (y[:, c:2 * c] * y[:, 2 * c:3 * c]).astype(BF16)


def hyena_pre(uh, conv_w, conv_b):
    b, l, c3 = uh.shape
    c = c3 // 3
    tl = _tile(l, 512)
    r = tl // 8
    nrow8 = l // 8
    kern = functools.partial(_hyena_pre_kernel, c=c)
    return pl.pallas_call(
        kern,
        grid=(b, l // tl),
        in_specs=[pl.BlockSpec((None, tl, c3), lambda bi, i: (bi, i, 0)),
                  pl.BlockSpec((None, 8, c3), lambda bi, i: (bi, jnp.maximum(i * r - 1, 0), 0)),
                  pl.BlockSpec((None, 8, c3), lambda bi, i: (bi, jnp.minimum((i + 1) * r, nrow8 - 1), 0)),
                  pl.BlockSpec((3, c3), lambda bi, i: (0, 0)),
                  pl.BlockSpec((1, c3), lambda bi, i: (0, 0))],
        out_specs=[pl.BlockSpec((None, tl, c), lambda bi, i: (bi, i, 0)),
                   pl.BlockSpec((None, tl, c), lambda bi, i: (bi, i, 0))],
        out_shape=[jax.ShapeDtypeStruct((b, l, c), BF16)] * 2,
        compiler_params=_cparams("parallel", "parallel"),
        name="hyena_pre",
    )(uh, uh, uh, conv_w, conv_b.reshape(1, c3))


def _filter_kernel(z_ref, t_ref, w1_ref, b1_ref, f1_ref, w2_ref, b2_ref, f2_ref, w3_ref, b3_ref, dl_ref, o_ref, *, l):
    h = jnp.sin(f1_ref[...] * (_dot_hi(z_ref[...], w1_ref[...]) + b1_ref[...]))
    h = jnp.sin(f2_ref[...] * (_dot_hi(h, w2_ref[...]) + b2_ref[...]))
    k = (_dot_hi(h, w3_ref[...]) + b3_ref[...]) * jnp.exp(-t_ref[:, 0:1] * dl_ref[...])
    row = pl.program_id(0) * k.shape[0] + lax.broadcasted_iota(jnp.int32, k.shape, 0)
    o_ref[...] = jnp.where(row == l, 0.0, k).astype(BF16)


def hyena_filter(l, w1, b1, f1, w2, b2, f2, w3, b3):
    c2 = w3.shape[1]
    c = c2 // 2
    hid = w1.shape[1]
    t = jnp.linspace(0.0, 1.0, l, dtype=F32)[:, None]
    w = (2.0 * math.pi / l) * jnp.arange(l, dtype=F32)[:, None]
    bands = jnp.linspace(1e-4, FILTER_BANDS - 1, FILTER_BANDS, dtype=F32)[None, :]
    z = jnp.concatenate([t, jnp.cos(bands * w), -jnp.sin(bands * w)], axis=-1)
    z = jnp.pad(z, ((0, 0), (0, LANES - FILTER_EMB)))
    tlane = jnp.broadcast_to(t, (l, LANES))

    def mirrored(a):
        return jnp.concatenate([a, a[0:1], jnp.flip(a[1:], axis=0)], axis=0)

    def padv(v):
        return jnp.pad(v.astype(F32), (0, LANES - hid)).reshape(1, LANES)

    w1p = jnp.pad(w1.astype(F32), ((0, LANES - FILTER_EMB), (0, LANES - hid)))
    w2p = jnp.pad(w2.astype(F32), ((0, LANES - hid), (0, LANES - hid)))
    w3p = jnp.pad(w3.astype(F32), ((0, LANES - hid), (0, 0)))
    deltas = jnp.abs(jnp.linspace(MIN_DECAY, MAX_DECAY, c, dtype=F32)).reshape(1, c)
    tl = _tile(l, 1024)
    nfwd = l // tl

    def const(shape):
        return pl.BlockSpec(shape, lambda i: (0, 0))

    return pl.pallas_call(
        functools.partial(_filter_kernel, l=l),
        grid=(2 * nfwd,),
        in_specs=[pl.BlockSpec((tl, LANES), lambda i: (i, 0)),
                  pl.BlockSpec((tl, LANES), lambda i: (i, 0)),
                  const((LANES, LANES)), const((1, LANES)), const((1, LANES)),
                  const((LANES, LANES)), const((1, LANES)), const((1, LANES)),
                  pl.BlockSpec((LANES, c), lambda i: (0, i // nfwd)),
                  pl.BlockSpec((1, c), lambda i: (0, i // nfwd)),
                  const((1, c))],
        out_specs=pl.BlockSpec((tl, c), lambda i: (i, 0)),
        out_shape=jax.ShapeDtypeStruct((2 * l, c), BF16),
        compiler_params=_cparams("parallel"),
        name="hyena_filter",
    )(mirrored(z), mirrored(tlane), w1p, padv(b1), padv(f1), w2p, padv(b2), padv(f2), w3p,
      b3.astype(F32).reshape(1, c2), deltas)


def _fft_split(n):
    n2 = 128
    n1 = n // n2
    assert n1 * n2 == n and n1 % 16 == 0, n
    return n1, n2


def _angles(rows, cols, period):
    prod = (jnp.arange(rows, dtype=jnp.int32)[:, None] * jnp.arange(cols, dtype=jnp.int32)[None, :]) % period
    return prod.astype(F32) * (2.0 * math.pi / period)


def _dft_tables(n1, n2):
    n = n1 * n2
    ang1 = _angles(n1, n1, n1)
    c1, s1 = jnp.cos(ang1), jnp.sin(ang1)
    fa = jnp.concatenate([c1, -s1], axis=0).astype(BF16)
    fc = (jnp.concatenate([c1, -s1], axis=1) * (1.0 / n)).astype(BF16)
    ang2 = _angles(n2, n2, n2)
    c2, s2 = jnp.cos(ang2), jnp.sin(ang2)
    g_fwd = jnp.block([[c2, s2], [-s2, c2]]).astype(BF16)
    g_inv = jnp.block([[c2, -s2], [s2, c2]]).astype(BF16)
    angt = _angles(n1, n2, n)
    twr = jnp.broadcast_to(jnp.cos(angt)[:, :, None], (n1, n2, LANES))
    twi = jnp.broadcast_to(jnp.sin(angt)[:, :, None], (n1, n2, LANES))
    return fa, fc, g_fwd, g_inv, twr, twi


def _fft_a_kernel(f_ref, u_ref, o_ref):
    o_ref[...] = _dot(f_ref[...], u_ref[...]).astype(BF16)


def fft_a(fa, u):
    b, k, w = u.shape
    m = fa.shape[0]
    tn = _tile(w, 2048)
    return pl.pallas_call(
        _fft_a_kernel,
        grid=(b, w // tn),
        in_specs=[pl.BlockSpec((m, k), lambda bi, j: (0, 0)),
                  pl.BlockSpec((None, k, tn), lambda bi, j: (bi, 0, j))],
        out_specs=pl.BlockSpec((None, m, tn), lambda bi, j: (bi, 0, j)),
        out_shape=jax.ShapeDtypeStruct((b, m, w), BF16),
        compiler_params=_cparams("parallel", "parallel"),
        name="fft_a",
    )(fa, u)


def _lane_tile(t, c):
    return jnp.concatenate([t] * (c // LANES), axis=1) if c > LANES else t


def _fft_mf_kernel(a_ref, twr_ref, twi_ref, g_ref, o_ref, *, tk1):
    for j in range(tk1):
        ar = a_ref[0, j].astype(F32)
        ai = a_ref[1, j].astype(F32)
        c = ar.shape[1]
        twr = _lane_tile(twr_ref[j], c)
        twi = _lane_tile(twi_ref[j], c)
        br = ar * twr + ai * twi
        bi = ai * twr - ar * twi
        x = _dot(g_ref[...], jnp.concatenate([br, bi], axis=0).astype(BF16))
        n2 = ar.shape[0]
        o_ref[0, j] = x[:n2].astype(BF16)
        o_ref[1, j] = x[n2:].astype(BF16)


def fft_mf(a, twr, twi, g_fwd):
    _, n1, n2, c = a.shape
    tk1 = 8
    kern = functools.partial(_fft_mf_kernel, tk1=tk1)
    return pl.pallas_call(
        kern,
        grid=(n1 // tk1,),
        in_specs=[pl.BlockSpec((2, tk1, n2, c), lambda i: (0, i, 0, 0)),
                  pl.BlockSpec((tk1, n2, LANES), lambda i: (i, 0, 0)),
                  pl.BlockSpec((tk1, n2, LANES), lambda i: (i, 0, 0)),
                  pl.BlockSpec((2 * n2, 2 * n2), lambda i: (0, 0))],
        out_specs=pl.BlockSpec((2, tk1, n2, c), lambda i: (0, i, 0, 0)),
        out_shape=jax.ShapeDtypeStruct((2, n1, n2, c), BF16),
        compiler_params=_cparams("parallel"),
        name="fft_mf",
    )(a, twr, twi, g_fwd)


def _fft_m_kernel(a_ref, kf_ref, twr_ref, twi_ref, gf_ref, gi_ref, o_ref, *, tk1):
    for j in range(tk1):
        ar = a_ref[0, j].astype(F32)
        ai = a_ref[1, j].astype(F32)
        n2, c = ar.shape
        twr = _lane_tile(twr_ref[j], c)
        twi = _lane_tile(twi_ref[j], c)
        br = ar * twr + ai * twi
        bi = ai * twr - ar * twi
        x = _dot(gf_ref[...], jnp.concatenate([br, bi], axis=0).astype(BF16))
        xr, xi = x[:n2], x[n2:]
        kr = kf_ref[0, j].astype(F32)
        ki = kf_ref[1, j].astype(F32)
        zr = xr * kr - xi * ki
        zi = xr * ki + xi * kr
        y = _dot(gi_ref[...], jnp.concatenate([zr, zi], axis=0).astype(BF16))
        yr, yi = y[:n2], y[n2:]
        o_ref[0, j] = (yr * twr - yi * twi).astype(BF16)
        o_ref[1, j] = (yi * twr + yr * twi).astype(BF16)


def fft_m(a, kf, twr, twi, g_fwd, g_inv):
    b, _, n1, n2, c = a.shape
    tk1 = 8
    kern = functools.partial(_fft_m_kernel, tk1=tk1)
    return pl.pallas_call(
        kern,
        grid=(n1 // tk1, b),
        in_specs=[pl.BlockSpec((None, 2, tk1, n2, c), lambda i, bi: (bi, 0, i, 0, 0)),
                  pl.BlockSpec((2, tk1, n2, c), lambda i, bi: (0, i, 0, 0)),
                  pl.BlockSpec((tk1, n2, LANES), lambda i, bi: (i, 0, 0)),
                  pl.BlockSpec((tk1, n2, LANES), lambda i, bi: (i, 0, 0)),
                  pl.BlockSpec((2 * n2, 2 * n2), lambda i, bi: (0, 0)),
                  pl.BlockSpec((2 * n2, 2 * n2), lambda i, bi: (0, 0))],
        out_specs=pl.BlockSpec((None, 2, tk1, n2, c), lambda i, bi: (bi, 0, i, 0, 0)),
        out_shape=jax.ShapeDtypeStruct(a.shape, BF16),
        compiler_params=_cparams("parallel", "parallel"),
        name="fft_m",
    )(a, kf, twr, twi, g_fwd, g_inv)


def _fft_c_kernel(f_ref, a_ref, x0_ref, uu_ref, bias_ref, o_ref):
    y = _dot(f_ref[...], a_ref[...])
    y = y + uu_ref[...].astype(F32) * bias_ref[...]
    o_ref[...] = (x0_ref[...].astype(F32) * y).astype(BF16)


def fft_c(fc_half, a2, x0, uu, bias_tiled):
    b, k, w = a2.shape
    m = fc_half.shape[0]
    tn = bias_tiled.shape[1]
    return pl.pallas_call(
        _fft_c_kernel,
        grid=(b, w // tn),
        in_specs=[pl.BlockSpec((m, k), lambda bi, j: (0, 0)),
                  pl.BlockSpec((None, k, tn), lambda bi, j: (bi, 0, j)),
                  pl.BlockSpec((None, m, tn), lambda bi, j: (bi, 0, j)),
                  pl.BlockSpec((None, m, tn), lambda bi, j: (bi, 0, j)),
                  pl.BlockSpec((1, tn), lambda bi, j: (0, 0))],
        out_specs=pl.BlockSpec((None, m, tn), lambda bi, j: (bi, 0, j)),
        out_shape=jax.ShapeDtypeStruct((b, m, w), BF16),
        compiler_params=_cparams("parallel", "parallel"),
        name="fft_c",
    )(fc_half, a2, x0, uu, bias_tiled)


def hyena_long_conv(x0, uu, k_circ, filt_bias):
    b, l, c = uu.shape
    n = 2 * l
    n1, n2 = _fft_split(n)
    fa, fc, g_fwd, g_inv, twr, twi = _dft_tables(n1, n2)
    w = n2 * c
    ka = fft_a(fa, k_circ.reshape(1, n1, w))
    kf = fft_mf(ka.reshape(2, n1, n2, c), twr, twi, g_fwd)
    a = fft_a(fa[:, :n1 // 2], uu.reshape(b, n1 // 2, w))
    a2 = fft_m(a.reshape(b, 2, n1, n2, c), kf, twr, twi, g_fwd, g_inv)
    tn = _tile(w, 2048)
    bias_tiled = jnp.tile(filt_bias.astype(F32), tn // c).reshape(1, tn)
    yh = fft_c(fc[:n1 // 2], a2.reshape(b, 2 * n1, w), x0.reshape(b, n1 // 2, w), uu.reshape(b, n1 // 2, w),
               bias_tiled)
    return yh.reshape(b, l, c)


def _attn_kernel(q_ref, k_ref, v_ref, o_ref, qs_ref, m_ref, acc_ref, *, tq, tk, nsub):
    ki = pl.program_id(3)

    @pl.when(ki == 0)
    def _():
        qs_ref[0:tq, :] = q_ref[:, 0:HEAD_DIM]
        qs_ref[tq:2 * tq, :] = q_ref[:, HEAD_DIM:2 * HEAD_DIM]
        m_ref[...] = jnp.full(m_ref.shape, -jnp.inf, F32)
        acc_ref[...] = jnp.zeros(acc_ref.shape, F32)

    qs = qs_ref[...]
    for j in range(nsub):
        k = k_ref[j * tk:(j + 1) * tk, :]
        v = v_ref[j * tk:(j + 1) * tk, :]
        v_aug = jnp.concatenate([v, jnp.ones_like(v)], axis=1)
        s = lax.dot_general(qs, k, (((1,), (1,)), ((), ())), preferred_element_type=F32)
        m_prev = m_ref[...]
        m_new = jnp.maximum(m_prev, jnp.max(s, axis=-1, keepdims=True))
        alpha = jnp.exp2(m_prev - m_new)
        p = jnp.exp2((s - pltpu.repeat(m_new, tk // LANES, 1)).astype(BF16))
        acc_ref[...] = jnp.concatenate([alpha, alpha], axis=1) * acc_ref[...] + _dot(p, v_aug)
        m_ref[...] = m_new

    @pl.when(ki == pl.num_programs(3) - 1)
    def _():
        o = acc_ref[:, 0:HEAD_DIM] / acc_ref[:, HEAD_DIM:2 * HEAD_DIM]
        o_ref[:, 0:HEAD_DIM] = o[0:tq].astype(BF16)
        o_ref[:, HEAD_DIM:2 * HEAD_DIM] = o[tq:2 * tq].astype(BF16)


def attention(q, k, v):
    b, l, _ = q.shape
    tq = _tile(l, 512)
    tk = _tile(l, 512)
    nsub = _tile(l // tk, 4)
    g = (N_Q_HEADS // N_KV_HEADS) * HEAD_DIM
    kern = functools.partial(_attn_kernel, tq=tq, tk=tk, nsub=nsub)
    tkb = tk * nsub
    return pl.pallas_call(
        kern,
        grid=(b, N_KV_HEADS, l // tq, l // tkb),
        in_specs=[pl.BlockSpec((None, tq, g), lambda bi, h, i, j: (bi, i, h)),
                  pl.BlockSpec((None, tkb, HEAD_DIM), lambda bi, h, i, j: (bi, j, h)),
                  pl.BlockSpec((None, tkb, HEAD_DIM), lambda bi, h, i, j: (bi, j, h))],
        out_specs=pl.BlockSpec((None, tq, g), lambda bi, h, i, j: (bi, i, h)),
        out_shape=jax.ShapeDtypeStruct(q.shape, BF16),
        scratch_shapes=[pltpu.VMEM((2 * tq, HEAD_DIM), BF16),
                        pltpu.VMEM((2 * tq, LANES), F32),
                        pltpu.VMEM((2 * tq, 2 * HEAD_DIM), F32)],
        compiler_params=_cparams("parallel", "parallel", "parallel", "arbitrary"),
        name="attention",
    )(q, k, v)


def _mix_kernel(x_ref, yh_ref, ya_ref, gh_ref, ga_ref, mod_ref, woh_ref, woa_ref, wo_ref, g_ref, rw_ref, rb_ref,
                xn_ref, n2_ref, gates_ref):
    th = _dot(yh_ref[...], woh_ref[...])
    ta = _dot(ya_ref[...], woa_ref[...])
    mixed = (jax.nn.sigmoid(gh_ref[...].astype(F32)) * th + jax.nn.sigmoid(ga_ref[...].astype(F32)) * ta)
    mix = _dot(mixed.astype(BF16), wo_ref[...])
    x = x_ref[...] + mod_ref[2:3, :] * mix
    xn_ref[...] = x

    y = x * lax.rsqrt(jnp.mean(x * x, axis=-1, keepdims=True) + EPS) * g_ref[...]
    n2 = y * (1.0 + mod_ref[4:5, :]) + mod_ref[3:4, :]
    n2_hi = n2.astype(BF16)
    n2_ref[...] = n2_hi

    n2_lo = (n2 - n2_hi.astype(F32)).astype(BF16)
    rw = rw_ref[...]
    rw_hi = rw.astype(BF16)
    rw_lo = (rw - rw_hi.astype(F32)).astype(BF16)
    logits = _dot(n2_hi, rw_hi) + (_dot(n2_hi, rw_lo) + _dot(n2_lo, rw_hi)) + rb_ref[...]

    lane = lax.broadcasted_iota(jnp.int32, logits.shape, 1)
    work = logits
    vals, hots = [], []
    for _ in range(TOP_K):
        m = jnp.max(work, axis=-1, keepdims=True)
        idx = jnp.min(jnp.where(work == m, lane, LANES), axis=-1, keepdims=True)
        hot = lane == idx
        vals.append(m)
        hots.append(hot)
        work = jnp.where(hot, -jnp.inf, work)
    exps = [jnp.exp(v - vals[0]) for v in vals]
    den = exps[0] + exps[1] + exps[2] + exps[3]
    gates = jnp.zeros(logits.shape, F32)
    for hot, e in zip(hots, exps):
        gates = jnp.where(hot, e / den, gates)
    gates_ref[...] = gates


def mix_router(x, yh, ya, gh, ga, mod, w_out_h, w_out_a, w_o, norm_g, router_w_p, router_b_p):
    b, l, d = x.shape
    tm = _tile(l, 512)
    c = yh.shape[2]
    da = ya.shape[2]

    def tok(w):
        return pl.BlockSpec((None, tm, w), lambda bi, i: (bi, i, 0))

    def const(shape):
        return pl.BlockSpec(shape, lambda bi, i: (0,) * len(shape))

    return pl.pallas_call(
        _mix_kernel,
        grid=(b, l // tm),
        in_specs=[tok(d), tok(c), tok(da), tok(d), tok(d),
                  pl.BlockSpec((None, N_MOD, d), lambda bi, i: (bi, 0, 0)),
                  const((c, d)), const((da, d)), const((d, d)), const((1, d)),
                  const((d, LANES)), const((1, LANES))],
        out_specs=[tok(d), tok(d), tok(LANES)],
        out_shape=[jax.ShapeDtypeStruct((b, l, d), F32),
                   jax.ShapeDtypeStruct((b, l, d), BF16),
                   jax.ShapeDtypeStruct((b, l, LANES), F32)],
        compiler_params=_cparams("parallel", "parallel"),
        name="mix_router",
    )(x, yh, ya, gh, ga, mod, w_out_h, w_out_a, w_o, norm_g.reshape(1, d), router_w_p, router_b_p)


def _deinterleave_kernel(w_ref, p_ref, o_ref):
    o_ref[...] = _dot(w_ref[...].astype(BF16), p_ref[...]).astype(BF16)


def deinterleave_up(w_up):
    ne, d, f2 = w_up.shape
    src = lax.broadcasted_iota(jnp.int32, (f2, f2), 0)
    dst = lax.broadcasted_iota(jnp.int32, (f2, f2), 1)
    perm = (src == jnp.where(dst < f2 // 2, 2 * dst, 2 * (dst - f2 // 2) + 1)).astype(BF16)
    return pl.pallas_call(
        _deinterleave_kernel,
        grid=(ne,),
        in_specs=[pl.BlockSpec((None, d, f2), lambda e: (e, 0, 0)),
                  pl.BlockSpec((f2, f2), lambda e: (0, 0))],
        out_specs=pl.BlockSpec((None, d, f2), lambda e: (e, 0, 0)),
        out_shape=jax.ShapeDtypeStruct((ne, d, f2), BF16),
        compiler_params=_cparams("parallel"),
        name="deinterleave_up",
    )(w_up, perm)


def _split_bf16(a):
    hi = a.astype(BF16)
    return hi, (a - hi.astype(F32)).astype(BF16)


def _moe_kernel(n2_ref, gates_ref, xn_ref, mod_ref, wu_ref, bu_ref, wd_ref, bd_ref, o_ref, acc_ref, *, e_step):
    eb = pl.program_id(2)
    gates = gates_ref[...]

    @pl.when(eb == 0)
    def _():
        g_hi, g_lo = _split_bf16(gates)
        b_hi, b_lo = _split_bf16(bd_ref[...])
        acc_ref[...] = _dot(g_hi, b_hi) + (_dot(g_hi, b_lo) + _dot(g_lo, b_hi))

    n2 = n2_ref[...]
    gates_bf = gates.astype(BF16)
    f = wd_ref.shape[1]
    total = None
    for j in range(e_step):
        h = _dot(n2, wu_ref[j]) + bu_ref[j]
        xg = jnp.minimum(h[:, 0:f], SWIGLU_LIMIT)
        xl = jnp.clip(h[:, f:2 * f], -SWIGLU_LIMIT, SWIGLU_LIMIT)
        act = xg * jax.nn.sigmoid(SWIGLU_ALPHA * xg) * (xl + 1.0)
        sel = (lax.broadcasted_iota(jnp.int32, (LANES, f), 0) == eb * e_step + j).astype(BF16)
        ge = _dot(gates_bf, sel)
        y = _dot((act * ge).astype(BF16), wd_ref[j])
        total = y if total is None else total + y
    acc_ref[...] += total

    @pl.when(eb == pl.num_programs(2) - 1)
    def _():
        o_ref[...] = xn_ref[...] + mod_ref[5:6, :] * acc_ref[...]


def moe(n2, gates, xn, mod, wu, bu, wd, bd_p):
    b, l, d = xn.shape
    ne, f, _ = wd.shape
    tm = _tile(l, 1024)
    e_step = _tile(ne, 4)

    def tok(w):
        return pl.BlockSpec((None, tm, w), lambda bi, i, e: (bi, i, 0))

    return pl.pallas_call(
        functools.partial(_moe_kernel, e_step=e_step),
        grid=(b, l // tm, ne // e_step),
        in_specs=[tok(d), tok(LANES), tok(d),
                  pl.BlockSpec((None, N_MOD, d), lambda bi, i, e: (bi, 0, 0)),
                  pl.BlockSpec((e_step, d, 2 * f), lambda bi, i, e: (e, 0, 0)),
                  pl.BlockSpec((e_step, 1, 2 * f), lambda bi, i, e: (e, 0, 0)),
                  pl.BlockSpec((e_step, f, d), lambda bi, i, e: (e, 0, 0)),
                  pl.BlockSpec((LANES, d), lambda bi, i, e: (0, 0))],
        out_specs=tok(d),
        out_shape=jax.ShapeDtypeStruct((b, l, d), F32),
        scratch_shapes=[pltpu.VMEM((tm, d), F32)],
        compiler_params=_cparams("parallel", "parallel", "arbitrary"),
        name="moe",
    )(n2, gates, xn, mod, wu, bu, wd, bd_p)


def _rope_tables(l):
    rows = l // GRID_W
    row = jnp.repeat(jnp.arange(rows, dtype=F32), GRID_W)
    col = jnp.tile(jnp.arange(GRID_W, dtype=F32), rows)
    n_freq = HEAD_DIM // 4
    freqs = ROPE_THETA ** (-jnp.arange(n_freq, dtype=F32) / n_freq)
    ang_r = row[:, None] * freqs
    ang_c = col[:, None] * freqs
    cos = jnp.concatenate([jnp.cos(ang_r), jnp.cos(ang_r), jnp.cos(ang_c), jnp.cos(ang_c)], axis=-1)
    sin = jnp.concatenate([-jnp.sin(ang_r), jnp.sin(ang_r), -jnp.sin(ang_c), jnp.sin(ang_c)], axis=-1)
    return cos, sin


def _encoder_layer(x, mod, p):
    b, l, d = x.shape
    c = p['filt_bias'].shape[0]
    cos, sin_signed = _rope_tables(l)
    uh, q, k, v, gh, ga = inproj(x, mod, p['norm_mix'], p['w_in'], cos, sin_signed, p['q_norm'], p['k_norm'],
                                 3 * c, N_Q_HEADS * HEAD_DIM, N_KV_HEADS * HEAD_DIM)
    x0, uu = hyena_pre(uh, p['conv_w'], p['conv_b'])
    k_circ = hyena_filter(l, p['filt_w1'], p['filt_b1'], p['filt_freq1'], p['filt_w2'], p['filt_b2'],
                          p['filt_freq2'], p['filt_w3'], p['filt_b3'])
    yh = hyena_long_conv(x0, uu, k_circ, p['filt_bias'])
    ya = attention(q, k, v)
    xn, n2, gates = mix_router(x, yh, ya, gh, ga, mod, p['w_out_h'], p['w_out_a'], p['w_o'], p['norm_ffn'],
                               p['router_w'], p['router_b'])
    return moe(n2, gates, xn, mod, p['wu'], p['bu'], p['wd'], p['bd'])


def kernel(x_prompt, x_sample, c_prompt, c_sample, w_ada, b_ada, norm_mix, w_in, conv_w, conv_b, filt_w1, filt_b1, filt_freq1, filt_w2, filt_b2, filt_freq2, filt_w3, filt_b3, filt_bias, q_norm, k_norm, w_out_h, w_out_a, w_o, norm_ffn, router_w, router_b, w_up, b_up, w_down, b_down):
    depth = w_ada.shape[0]
    d = x_prompt.shape[-1]
    bp = c_prompt.shape[0]
    bs = c_sample.shape[0]
    ne = router_w.shape[-1]
    y_prompt, y_sample = x_prompt, x_sample
    for i in range(depth):
        rows = -(-(bp + bs) // 8) * 8
        c_all = jnp.pad(jnp.concatenate([c_prompt, c_sample], axis=0), ((0, rows - bp - bs), (0, 0)))
        mod = adaln(c_all, w_ada[i], b_ada[i]).reshape(rows, N_MOD, d)
        p = {
            'norm_mix': norm_mix[i], 'w_in': w_in[i].astype(BF16),
            'conv_w': conv_w[i], 'conv_b': conv_b[i],
            'filt_w1': filt_w1[i], 'filt_b1': filt_b1[i], 'filt_freq1': filt_freq1[i],
            'filt_w2': filt_w2[i], 'filt_b2': filt_b2[i], 'filt_freq2': filt_freq2[i],
            'filt_w3': filt_w3[i], 'filt_b3': filt_b3[i], 'filt_bias': filt_bias[i],
            'q_norm': q_norm[i], 'k_norm': k_norm[i],
            'w_out_h': w_out_h[i].astype(BF16), 'w_out_a': w_out_a[i].astype(BF16), 'w_o': w_o[i].astype(BF16),
            'norm_ffn': norm_ffn[i],
            'router_w': jnp.pad(router_w[i], ((0, 0), (0, LANES - ne))),
            'router_b': jnp.pad(router_b[i], (0, LANES - ne), constant_values=NEG_BIG).reshape(1, LANES),
            'wu': deinterleave_up(w_up[i]),
            'bu': jnp.concatenate([b_up[i][:, None, 0::2], b_up[i][:, None, 1::2]], axis=-1),
            'wd': w_down[i].astype(BF16),
            'bd': jnp.pad(b_down[i], ((0, LANES - ne), (0, 0))),
        }
        y_prompt = _encoder_layer(y_prompt, mod[:bp], p)
        y_sample = _encoder_layer(y_sample, mod[bp:bp + bs], p)
    return (y_prompt, y_sample)
```

```python
import functools
import math

import jax
import jax.numpy as jnp
from jax import lax
from jax.experimental import pallas as pl
from jax.experimental.pallas import tpu as pltpu

F32 = jnp.float32
BF16 = jnp.bfloat16

EPS = 1e-6
N_MOD = 6
GRID_W = 64
HEAD_DIM = 128
N_Q_HEADS = 4
N_KV_HEADS = 2
ROPE_THETA = 10000.0
TOP_K = 4
SWIGLU_ALPHA = 1.702
SWIGLU_LIMIT = 7.0
FILTER_EMB = 33
FILTER_BANDS = (FILTER_EMB - 1) // 2
DECAY_TARGET = 1e-2
MIN_DECAY = math.log(DECAY_TARGET) / 1.5
MAX_DECAY = math.log(DECAY_TARGET) / 0.3

LANES = 128
NEG_BIG = -1e30
VMEM_LIMIT = 56 * 1024 * 1024


def _cparams(*sem):
    return pltpu.CompilerParams(dimension_semantics=sem, vmem_limit_bytes=VMEM_LIMIT)


def _dot(a, b):
    return jnp.dot(a, b, preferred_element_type=F32)


def _dot_hi(a, b):
    return jnp.dot(a, b, preferred_element_type=F32, precision=lax.Precision.HIGHEST)


def _tile(n, want):
    t = min(n, want)
    assert n % t == 0, (n, want)
    return t


def _adaln_kernel(c_ref, w_ref, b_ref, o_ref):
    c = c_ref[...]
    o_ref[...] = _dot_hi(c * jax.nn.sigmoid(c), w_ref[...]) + b_ref[...]


def adaln(c, w_ada, b_ada):
    r, d = c.shape
    n = w_ada.shape[1]
    tn = _tile(n, 1536)
    return pl.pallas_call(
        _adaln_kernel,
        grid=(n // tn,),
        in_specs=[pl.BlockSpec((r, d), lambda j: (0, 0)),
                  pl.BlockSpec((d, tn), lambda j: (0, j)),
                  pl.BlockSpec((1, tn), lambda j: (0, j))],
        out_specs=pl.BlockSpec((r, tn), lambda j: (0, j)),
        out_shape=jax.ShapeDtypeStruct((r, n), F32),
        compiler_params=_cparams("arbitrary"),
        name="adaln",
    )(c, w_ada, b_ada.reshape(1, n))


def _rope(xn, cos, sin_signed):
    lane = lax.broadcasted_iota(jnp.int32, xn.shape, 1)
    first_half = (lane % 64) < 32
    rot = jnp.where(first_half, pltpu.roll(xn, 96, 1), pltpu.roll(xn, 32, 1))
    return xn * cos + rot * sin_signed


def _inproj_kernel(x_ref, mod_ref, g_ref, w_ref, cos_ref, sin_ref, qg_ref, kg_ref,
                   uh_ref, q_ref, k_ref, v_ref, gh_ref, ga_ref, *, d_hy3, d_attn, d_kv, d_model, q_scale):
    x = x_ref[...]
    y = x * lax.rsqrt(jnp.mean(x * x, axis=-1, keepdims=True) + EPS) * g_ref[...]
    n = (y * (1.0 + mod_ref[1:2, :]) + mod_ref[0:1, :]).astype(BF16)

    s0 = d_hy3
    s1 = s0 + d_attn
    s2 = s1 + d_kv
    s3 = s2 + d_kv
    s4 = s3 + d_model
    cos = cos_ref[...]
    sin = sin_ref[...]

    def norm_rope(z, gain):
        zn = z * lax.rsqrt(jnp.mean(z * z, axis=-1, keepdims=True) + EPS) * gain
        return _rope(zn, cos, sin)

    qf = _dot(n, w_ref[:, s0:s1])
    kf = _dot(n, w_ref[:, s1:s2])
    for h in range(d_attn // HEAD_DIM):
        sl = slice(h * HEAD_DIM, (h + 1) * HEAD_DIM)
        q_ref[:, sl] = (norm_rope(qf[:, sl], qg_ref[...]) * q_scale).astype(BF16)
    for h in range(d_kv // HEAD_DIM):
        sl = slice(h * HEAD_DIM, (h + 1) * HEAD_DIM)
        k_ref[:, sl] = norm_rope(kf[:, sl], kg_ref[...]).astype(BF16)

    uh_ref[...] = _dot(n, w_ref[:, 0:s0]).astype(BF16)
    v_ref[...] = _dot(n, w_ref[:, s2:s3]).astype(BF16)
    gh_ref[...] = _dot(n, w_ref[:, s3:s4]).astype(BF16)
    ga_ref[...] = _dot(n, w_ref[:, s4:s4 + d_model]).astype(BF16)


def inproj(x, mod, norm_g, w_in_bf, cos, sin_signed, q_norm, k_norm, d_hy3, d_attn, d_kv):
    b, l, d = x.shape
    tm = _tile(l, 512)
    d_in = w_in_bf.shape[1]
    q_scale = math.log2(math.e) / math.sqrt(HEAD_DIM)
    kern = functools.partial(_inproj_kernel, d_hy3=d_hy3, d_attn=d_attn, d_kv=d_kv, d_model=d, q_scale=q_scale)

    def tok(w):
        return pl.BlockSpec((None, tm, w), lambda bi, i: (bi, i, 0))

    def const(shape):
        return pl.BlockSpec(shape, lambda bi, i: (0,) * len(shape))

    outs = [d_hy3, d_attn, d_kv, d_kv, d, d]
    return pl.pallas_call(
        kern,
        grid=(b, l // tm),
        in_specs=[tok(d),
                  pl.BlockSpec((None, N_MOD, d), lambda bi, i: (bi, 0, 0)),
                  const((1, d)),
                  const((d, d_in)),
                  pl.BlockSpec((tm, HEAD_DIM), lambda bi, i: (i, 0)),
                  pl.BlockSpec((tm, HEAD_DIM), lambda bi, i: (i, 0)),
                  const((1, HEAD_DIM)),
                  const((1, HEAD_DIM))],
        out_specs=[tok(w) for w in outs],
        out_shape=[jax.ShapeDtypeStruct((b, l, w), BF16) for w in outs],
        compiler_params=_cparams("parallel", "parallel"),
        name="inproj",
    )(x, mod, norm_g.reshape(1, d), w_in_bf, cos, sin_signed, q_norm.reshape(1, HEAD_DIM), k_norm.reshape(1, HEAD_DIM))


def _hyena_pre_kernel(u_ref, prev_ref, next_ref, w_ref, b_ref, x0_ref, uu_ref, *, c):
    i = pl.program_id(1)
    nblk = pl.num_programs(1)
    u = u_ref[...].astype(F32)
    tl = u.shape[0]
    row = lax.broadcasted_iota(jnp.int32, u.shape, 0)
    prev_row = prev_ref[7:8, :].astype(F32) * jnp.where(i > 0, 1.0, 0.0)
    next_row = next_ref[0:1, :].astype(F32) * jnp.where(i < nblk - 1, 1.0, 0.0)
    up = jnp.where(row == 0, prev_row, pltpu.roll(u, 1, 0))
    un = jnp.where(row == tl - 1, next_row, pltpu.roll(u, tl - 1, 0))
    y = up * w_ref[0:1, :] + u * w_ref[1:2, :] + un * w_ref[2:3, :] + b_ref[...]
    x0_ref[...] = y[:, 0:c].astype(BF16)
    uu_ref[...] = (y[:, c:2 * c] * y[:, 2 * c:3 * c]).astype(BF16)


def hyena_pre(uh, conv_w, conv_b):
    b, l, c3 = uh.shape
    c = c3 // 3
    tl = _tile(l, 512)
    r = tl // 8
    nrow8 = l // 8
    kern = functools.partial(_hyena_pre_kernel, c=c)
    return pl.pallas_call(
        kern,
        grid=(b, l // tl),
        in_specs=[pl.BlockSpec((None, tl, c3), lambda bi, i: (bi, i, 0)),
                  pl.BlockSpec((None, 8, c3), lambda bi, i: (bi, jnp.maximum(i * r - 1, 0), 0)),
                  pl.BlockSpec((None, 8, c3), lambda bi, i: (bi, jnp.minimum((i + 1) * r, nrow8 - 1), 0)),
                  pl.BlockSpec((3, c3), lambda bi, i: (0, 0)),
                  pl.BlockSpec((1, c3), lambda bi, i: (0, 0))],
        out_specs=[pl.BlockSpec((None, tl, c), lambda bi, i: (bi, i, 0)),
                   pl.BlockSpec((None, tl, c), lambda bi, i: (bi, i, 0))],
        out_shape=[jax.ShapeDtypeStruct((b, l, c), BF16)] * 2,
        compiler_params=_cparams("parallel", "parallel"),
        name="hyena_pre",
    )(uh, uh, uh, conv_w, conv_b.reshape(1, c3))


def _filter_kernel(z_ref, t_ref, w1_ref, b1_ref, f1_ref, w2_ref, b2_ref, f2_ref, w3_ref, b3_ref, dl_ref, o_ref, *, l):
    h = jnp.sin(f1_ref[...] * (_dot_hi(z_ref[...], w1_ref[...]) + b1_ref[...]))
    h = jnp.sin(f2_ref[...] * (_dot_hi(h, w2_ref[...]) + b2_ref[...]))
    k = (_dot_hi(h, w3_ref[...]) + b3_ref[...]) * jnp.exp(-t_ref[:, 0:1] * dl_ref[...])
    row = pl.program_id(0) * k.shape[0] + lax.broadcasted_iota(jnp.int32, k.shape, 0)
    o_ref[...] = jnp.where(row == l, 0.0, k).astype(BF16)


def hyena_filter(l, w1, b1, f1, w2, b2, f2, w3, b3):
    c2 = w3.shape[1]
    c = c2 // 2
    hid = w1.shape[1]
    t = jnp.linspace(0.0, 1.0, l, dtype=F32)[:, None]
    w = (2.0 * math.pi / l) * jnp.arange(l, dtype=F32)[:, None]
    bands = jnp.linspace(1e-4, FILTER_BANDS - 1, FILTER_BANDS, dtype=F32)[None, :]
    z = jnp.concatenate([t, jnp.cos(bands * w), -jnp.sin(bands * w)], axis=-1)
    z = jnp.pad(z, ((0, 0), (0, LANES - FILTER_EMB)))
    tlane = jnp.broadcast_to(t, (l, LANES))

    def mirrored(a):
        return jnp.concatenate([a, a[0:1], jnp.flip(a[1:], axis=0)], axis=0)

    def padv(v):
        return jnp.pad(v.astype(F32), (0, LANES - hid)).reshape(1, LANES)

    w1p = jnp.pad(w1.astype(F32), ((0, LANES - FILTER_EMB), (0, LANES - hid)))
    w2p = jnp.pad(w2.astype(F32), ((0, LANES - hid), (0, LANES - hid)))
    w3p = jnp.pad(w3.astype(F32), ((0, LANES - hid), (0, 0)))
    deltas = jnp.abs(jnp.linspace(MIN_DECAY, MAX_DECAY, c, dtype=F32)).reshape(1, c)
    tl = _tile(l, 1024)
    nfwd = l // tl

    def const(shape):
        return pl.BlockSpec(shape, lambda i: (0, 0))

    return pl.pallas_call(
        functools.partial(_filter_kernel, l=l),
        grid=(2 * nfwd,),
        in_specs=[pl.BlockSpec((tl, LANES), lambda i: (i, 0)),
                  pl.BlockSpec((tl, LANES), lambda i: (i, 0)),
                  const((LANES, LANES)), const((1, LANES)), const((1, LANES)),
                  const((LANES, LANES)), const((1, LANES)), const((1, LANES)),
                  pl.BlockSpec((LANES, c), lambda i: (0, i // nfwd)),
                  pl.BlockSpec((1, c), lambda i: (0, i // nfwd)),
                  const((1, c))],
        out_specs=pl.BlockSpec((tl, c), lambda i: (i, 0)),
        out_shape=jax.ShapeDtypeStruct((2 * l, c), BF16),
        compiler_params=_cparams("parallel"),
        name="hyena_filter",
    )(mirrored(z), mirrored(tlane), w1p, padv(b1), padv(f1), w2p, padv(b2), padv(f2), w3p,
      b3.astype(F32).reshape(1, c2), deltas)


def _fft_split(n):
    n2 = 128
    n1 = n // n2
    assert n1 * n2 == n and n1 % 16 == 0, n
    return n1, n2


def _angles(rows, cols, period):
    prod = (jnp.arange(rows, dtype=jnp.int32)[:, None] * jnp.arange(cols, dtype=jnp.int32)[None, :]) % period
    return prod.astype(F32) * (2.0 * math.pi / period)


def _dft_tables(n1, n2):
    n = n1 * n2
    ang1 = _angles(n1, n1, n1)
    c1, s1 = jnp.cos(ang1), jnp.sin(ang1)
    fa = jnp.concatenate([c1, -s1], axis=0).astype(BF16)
    fc = (jnp.concatenate([c1, -s1], axis=1) * (1.0 / n)).astype(BF16)
    ang2 = _angles(n2, n2, n2)
    c2, s2 = jnp.cos(ang2), jnp.sin(ang2)
    g_fwd = jnp.block([[c2, s2], [-s2, c2]]).astype(BF16)
    g_inv = jnp.block([[c2, -s2], [s2, c2]]).astype(BF16)
    angt = _angles(n1, n2, n)
    twr = jnp.broadcast_to(jnp.cos(angt)[:, :, None], (n1, n2, LANES))
    twi = jnp.broadcast_to(jnp.sin(angt)[:, :, None], (n1, n2, LANES))
    return fa, fc, g_fwd, g_inv, twr, twi


def _fft_a_kernel(f_ref, u_ref, o_ref):
    o_ref[...] = _dot(f_ref[...], u_ref[...]).astype(BF16)


def fft_a(fa, u):
    b, k, w = u.shape
    m = fa.shape[0]
    tn = _tile(w, 2048)
    return pl.pallas_call(
        _fft_a_kernel,
        grid=(b, w // tn),
        in_specs=[pl.BlockSpec((m, k), lambda bi, j: (0, 0)),
                  pl.BlockSpec((None, k, tn), lambda bi, j: (bi, 0, j))],
        out_specs=pl.BlockSpec((None, m, tn), lambda bi, j: (bi, 0, j)),
        out_shape=jax.ShapeDtypeStruct((b, m, w), BF16),
        compiler_params=_cparams("parallel", "parallel"),
        name="fft_a",
    )(fa, u)


def _lane_tile(t, c):
    return jnp.concatenate([t] * (c // LANES), axis=1) if c > LANES else t


def _fft_mf_kernel(a_ref, twr_ref, twi_ref, g_ref, o_ref, *, tk1):
    for j in range(tk1):
        ar = a_ref[0, j].astype(F32)
        ai = a_ref[1, j].astype(F32)
        c = ar.shape[1]
        twr = _lane_tile(twr_ref[j], c)
        twi = _lane_tile(twi_ref[j], c)
        br = ar * twr + ai * twi
        bi = ai * twr - ar * twi
        x = _dot(g_ref[...], jnp.concatenate([br, bi], axis=0).astype(BF16))
        n2 = ar.shape[0]
        o_ref[0, j] = x[:n2].astype(BF16)
        o_ref[1, j] = x[n2:].astype(BF16)


def fft_mf(a, twr, twi, g_fwd):
    _, n1, n2, c = a.shape
    tk1 = 8
    kern = functools.partial(_fft_mf_kernel, tk1=tk1)
    return pl.pallas_call(
        kern,
        grid=(n1 // tk1,),
        in_specs=[pl.BlockSpec((2, tk1, n2, c), lambda i: (0, i, 0, 0)),
                  pl.BlockSpec((tk1, n2, LANES), lambda i: (i, 0, 0)),
                  pl.BlockSpec((tk1, n2, LANES), lambda i: (i, 0, 0)),
                  pl.BlockSpec((2 * n2, 2 * n2), lambda i: (0, 0))],
        out_specs=pl.BlockSpec((2, tk1, n2, c), lambda i: (0, i, 0, 0)),
        out_shape=jax.ShapeDtypeStruct((2, n1, n2, c), BF16),
        compiler_params=_cparams("parallel"),
        name="fft_mf",
    )(a, twr, twi, g_fwd)


def _fft_m_kernel(a_ref, kf_ref, twr_ref, twi_ref, gf_ref, gi_ref, o_ref, *, tk1):
    for j in range(tk1):
        ar = a_ref[0, j].astype(F32)
        ai = a_ref[1, j].astype(F32)
        n2, c = ar.shape
        twr = _lane_tile(twr_ref[j], c)
        twi = _lane_tile(twi_ref[j], c)
        br = ar * twr + ai * twi
        bi = ai * twr - ar * twi
        x = _dot(gf_ref[...], jnp.concatenate([br, bi], axis=0).astype(BF16))
        xr, xi = x[:n2], x[n2:]
        kr = kf_ref[0, j].astype(F32)
        ki = kf_ref[1, j].astype(F32)
        zr = xr * kr - xi * ki
        zi = xr * ki + xi * kr
        y = _dot(gi_ref[...], jnp.concatenate([zr, zi], axis=0).astype(BF16))
        yr, yi = y[:n2], y[n2:]
        o_ref[0, j] = (yr * twr - yi * twi).astype(BF16)
        o_ref[1, j] = (yi * twr + yr * twi).astype(BF16)


def fft_m(a, kf, twr, twi, g_fwd, g_inv):
    b, _, n1, n2, c = a.shape
    tk1 = 8
    kern = functools.partial(_fft_m_kernel, tk1=tk1)
    return pl.pallas_call(
        kern,
        grid=(n1 // tk1, b),
        in_specs=[pl.BlockSpec((None, 2, tk1, n2, c), lambda i, bi: (bi, 0, i, 0, 0)),
                  pl.BlockSpec((2, tk1, n2, c), lambda i, bi: (0, i, 0, 0)),
                  pl.BlockSpec((tk1, n2, LANES), lambda i, bi: (i, 0, 0)),
                  pl.BlockSpec((tk1, n2, LANES), lambda i, bi: (i, 0, 0)),
                  pl.BlockSpec((2 * n2, 2 * n2), lambda i, bi: (0, 0)),
                  pl.BlockSpec((2 * n2, 2 * n2), lambda i, bi: (0, 0))],
        out_specs=pl.BlockSpec((None, 2, tk1, n2, c), lambda i, bi: (bi, 0, i, 0, 0)),
        out_shape=jax.ShapeDtypeStruct(a.shape, BF16),
        compiler_params=_cparams("parallel", "parallel"),
        name="fft_m",
    )(a, kf, twr, twi, g_fwd, g_inv)


def _fft_c_kernel(f_ref, a_ref, x0_ref, uu_ref, bias_ref, o_ref):
    y = _dot(f_ref[...], a_ref[...])
    y = y + uu_ref[...].astype(F32) * bias_ref[...]
    o_ref[...] = (x0_ref[...].astype(F32) * y).astype(BF16)


def fft_c(fc_half, a2, x0, uu, bias_tiled):
    b, k, w = a2.shape
    m = fc_half.shape[0]
    tn = bias_tiled.shape[1]
    return pl.pallas_call(
        _fft_c_kernel,
        grid=(b, w // tn),
        in_specs=[pl.BlockSpec((m, k), lambda bi, j: (0, 0)),
                  pl.BlockSpec((None, k, tn), lambda bi, j: (bi, 0, j)),
                  pl.BlockSpec((None, m, tn), lambda bi, j: (bi, 0, j)),
                  pl.BlockSpec((None, m, tn), lambda bi, j: (bi, 0, j)),
                  pl.BlockSpec((1, tn), lambda bi, j: (0, 0))],
        out_specs=pl.BlockSpec((None, m, tn), lambda bi, j: (bi, 0, j)),
        out_shape=jax.ShapeDtypeStruct((b, m, w), BF16),
        compiler_params=_cparams("parallel", "parallel"),
        name="fft_c",
    )(fc_half, a2, x0, uu, bias_tiled)


def hyena_long_conv(x0, uu, k_circ, filt_bias):
    b, l, c = uu.shape
    n = 2 * l
    n1, n2 = _fft_split(n)
    fa, fc, g_fwd, g_inv, twr, twi = _dft_tables(n1, n2)
    w = n2 * c
    ka = fft_a(fa, k_circ.reshape(1, n1, w))
    kf = fft_mf(ka.reshape(2, n1, n2, c), twr, twi, g_fwd)
    a = fft_a(fa[:, :n1 // 2], uu.reshape(b, n1 // 2, w))
    a2 = fft_m(a.reshape(b, 2, n1, n2, c), kf, twr, twi, g_fwd, g_inv)
    tn = _tile(w, 2048)
    bias_tiled = jnp.tile(filt_bias.astype(F32), tn // c).reshape(1, tn)
    yh = fft_c(fc[:n1 // 2], a2.reshape(b, 2 * n1, w), x0.reshape(b, n1 // 2, w), uu.reshape(b, n1 // 2, w),
               bias_tiled)
    return yh.reshape(b, l, c)


def _attn_kernel(q_ref, k_ref, v_ref, o_ref, qs_ref, m_ref, acc_ref, *, tq, tk, nsub):
    ki = pl.program_id(3)

    @pl.when(ki == 0)
    def _():
        qs_ref[0:tq, :] = q_ref[:, 0:HEAD_DIM]
        qs_ref[tq:2 * tq, :] = q_ref[:, HEAD_DIM:2 * HEAD_DIM]
        m_ref[...] = jnp.full(m_ref.shape, -jnp.inf, F32)
        acc_ref[...] = jnp.zeros(acc_ref.shape, F32)

    qs = qs_ref[...]
    for j in range(nsub):
        k = k_ref[j * tk:(j + 1) * tk, :]
        v = v_ref[j * tk:(j + 1) * tk, :]
        v_aug = jnp.concatenate([v, jnp.ones_like(v)], axis=1)
        s = lax.dot_general(qs, k, (((1,), (1,)), ((), ())), preferred_element_type=F32)
        m_prev = m_ref[...]
        m_new = jnp.maximum(m_prev, jnp.max(s, axis=-1, keepdims=True))
        alpha = jnp.exp2(m_prev - m_new)
        p = jnp.exp2((s - pltpu.repeat(m_new, tk // LANES, 1)).astype(BF16))
        acc_ref[...] = jnp.concatenate([alpha, alpha], axis=1) * acc_ref[...] + _dot(p, v_aug)
        m_ref[...] = m_new

    @pl.when(ki == pl.num_programs(3) - 1)
    def _():
        o = acc_ref[:, 0:HEAD_DIM] / acc_ref[:, HEAD_DIM:2 * HEAD_DIM]
        o_ref[:, 0:HEAD_DIM] = o[0:tq].astype(BF16)
        o_ref[:, HEAD_DIM:2 * HEAD_DIM] = o[tq:2 * tq].astype(BF16)


def attention(q, k, v):
    b, l, _ = q.shape
    tq = _tile(l, 512)
    tk = _tile(l, 512)
    nsub = _tile(l // tk, 8)
    g = (N_Q_HEADS // N_KV_HEADS) * HEAD_DIM
    kern = functools.partial(_attn_kernel, tq=tq, tk=tk, nsub=nsub)
    tkb = tk * nsub
    return pl.pallas_call(
        kern,
        grid=(b, N_KV_HEADS, l // tq, l // tkb),
        in_specs=[pl.BlockSpec((None, tq, g), lambda bi, h, i, j: (bi, i, h)),
                  pl.BlockSpec((None, tkb, HEAD_DIM), lambda bi, h, i, j: (bi, j, h)),
                  pl.BlockSpec((None, tkb, HEAD_DIM), lambda bi, h, i, j: (bi, j, h))],
        out_specs=pl.BlockSpec((None, tq, g), lambda bi, h, i, j: (bi, i, h)),
        out_shape=jax.ShapeDtypeStruct(q.shape, BF16),
        scratch_shapes=[pltpu.VMEM((2 * tq, HEAD_DIM), BF16),
                        pltpu.VMEM((2 * tq, LANES), F32),
                        pltpu.VMEM((2 * tq, 2 * HEAD_DIM), F32)],
        compiler_params=_cparams("parallel", "parallel", "parallel", "arbitrary"),
        name="attention",
    )(q, k, v)


def _mix_kernel(x_ref, yh_ref, ya_ref, gh_ref, ga_ref, mod_ref, woh_ref, woa_ref, wo_ref, g_ref, rw_ref, rb_ref,
                xn_ref, n2_ref, gates_ref, *, n_exp, n_chunks):
    rows = x_ref.shape[0] // n_chunks
    for ci in range(n_chunks):
        _mix_rows(slice(ci * rows, (ci + 1) * rows), x_ref, yh_ref, ya_ref, gh_ref, ga_ref, mod_ref, woh_ref,
                  woa_ref, wo_ref, g_ref, rw_ref, rb_ref, xn_ref, n2_ref, gates_ref, n_exp)


def _mix_rows(r, x_ref, yh_ref, ya_ref, gh_ref, ga_ref, mod_ref, woh_ref, woa_ref, wo_ref, g_ref, rw_ref, rb_ref,
              xn_ref, n2_ref, gates_ref, n_exp):
    th = _dot(yh_ref[r, :], woh_ref[...])
    ta = _dot(ya_ref[r, :], woa_ref[...])
    mixed = (jax.nn.sigmoid(gh_ref[r, :].astype(F32)) * th + jax.nn.sigmoid(ga_ref[r, :].astype(F32)) * ta)
    mix = _dot(mixed.astype(BF16), wo_ref[...])
    x = x_ref[r, :] + mod_ref[2:3, :] * mix
    xn_ref[r, :] = x

    y = x * lax.rsqrt(jnp.mean(x * x, axis=-1, keepdims=True) + EPS) * g_ref[...]
    n2 = y * (1.0 + mod_ref[4:5, :]) + mod_ref[3:4, :]
    n2_hi = n2.astype(BF16)
    n2_ref[r, :] = n2_hi

    n2_lo = (n2 - n2_hi.astype(F32)).astype(BF16)
    r_hi = _dot(n2_hi, rw_ref[...])
    r_lo = _dot(n2_lo, rw_ref[...])
    lane = lax.broadcasted_iota(jnp.int32, r_hi.shape, 1)
    logits = r_hi + (pltpu.roll(r_hi, LANES - n_exp, 1) + r_lo) + rb_ref[...]
    logits = jnp.where(lane < n_exp, logits, NEG_BIG)

    work = logits
    vals, hots = [], []
    for _ in range(TOP_K):
        m = jnp.max(work, axis=-1, keepdims=True)
        idx = jnp.min(jnp.where(work == m, lane, LANES), axis=-1, keepdims=True)
        hot = lane == idx
        vals.append(m)
        hots.append(hot)
        work = jnp.where(hot, -jnp.inf, work)
    exps = [jnp.exp(v - vals[0]) for v in vals]
    den = exps[0] + exps[1] + exps[2] + exps[3]
    gates = jnp.zeros(logits.shape, F32)
    for hot, e in zip(hots, exps):
        gates = jnp.where(hot, e / den, gates)
    gates_ref[r, :] = gates


def pack_router(router_w, router_b):
    d, ne = router_w.shape
    assert 2 * ne <= LANES
    w_hi, w_lo = _split_bf16(router_w)
    w_p = jnp.concatenate([w_hi, w_lo, jnp.zeros((d, LANES - 2 * ne), BF16)], axis=1)
    return w_p, jnp.pad(router_b, (0, LANES - ne)).reshape(1, LANES), ne


def mix_router(x, yh, ya, gh, ga, mod, w_out_h, w_out_a, w_o, norm_g, router):
    router_w_p, router_b_p, n_exp = router
    b, l, d = x.shape
    tm = _tile(l, 1024)
    n_chunks = 2 if tm % 512 == 0 else 1
    c = yh.shape[2]
    da = ya.shape[2]

    def tok(w):
        return pl.BlockSpec((None, tm, w), lambda bi, i: (bi, i, 0))

    def const(shape):
        return pl.BlockSpec(shape, lambda bi, i: (0,) * len(shape))

    return pl.pallas_call(
        functools.partial(_mix_kernel, n_exp=n_exp, n_chunks=n_chunks),
        grid=(b, l // tm),
        in_specs=[tok(d), tok(c), tok(da), tok(d), tok(d),
                  pl.BlockSpec((None, N_MOD, d), lambda bi, i: (bi, 0, 0)),
                  const((c, d)), const((da, d)), const((d, d)), const((1, d)),
                  const((d, LANES)), const((1, LANES))],
        out_specs=[tok(d), tok(d), tok(LANES)],
        out_shape=[jax.ShapeDtypeStruct((b, l, d), F32),
                   jax.ShapeDtypeStruct((b, l, d), BF16),
                   jax.ShapeDtypeStruct((b, l, LANES), F32)],
        compiler_params=_cparams("parallel", "parallel"),
        name="mix_router",
    )(x, yh, ya, gh, ga, mod, w_out_h, w_out_a, w_o, norm_g.reshape(1, d), router_w_p, router_b_p)


def _deinterleave_kernel(w_ref, p_ref, o_ref):
    o_ref[...] = _dot(w_ref[...].astype(BF16), p_ref[...]).astype(BF16)


def deinterleave_up(w_up):
    ne, d, f2 = w_up.shape
    src = lax.broadcasted_iota(jnp.int32, (f2, f2), 0)
    dst = lax.broadcasted_iota(jnp.int32, (f2, f2), 1)
    perm = (src == jnp.where(dst < f2 // 2, 2 * dst, 2 * (dst - f2 // 2) + 1)).astype(BF16)
    return pl.pallas_call(
        _deinterleave_kernel,
        grid=(ne,),
        in_specs=[pl.BlockSpec((None, d, f2), lambda e: (e, 0, 0)),
                  pl.BlockSpec((f2, f2), lambda e: (0, 0))],
        out_specs=pl.BlockSpec((None, d, f2), lambda e: (e, 0, 0)),
        out_shape=jax.ShapeDtypeStruct((ne, d, f2), BF16),
        compiler_params=_cparams("parallel"),
        name="deinterleave_up",
    )(w_up, perm)


def _split_bf16(a):
    hi = a.astype(BF16)
    return hi, (a - hi.astype(F32)).astype(BF16)


def _moe_kernel(n2_ref, gates_ref, xn_ref, mod_ref, wu_ref, bu_ref, wd_ref, bd_ref, o_ref, acc_ref, *, e_step):
    eb = pl.program_id(2)
    gates = gates_ref[...]

    @pl.when(eb == 0)
    def _():
        g_hi, g_lo = _split_bf16(gates)
        b_hi, b_lo = _split_bf16(bd_ref[...])
        acc_ref[...] = _dot(g_hi, b_hi) + (_dot(g_hi, b_lo) + _dot(g_lo, b_hi))

    n2 = n2_ref[...]
    gates_bf = gates.astype(BF16)
    f = wd_ref.shape[1]
    total = None
    for j in range(e_step):
        h = _dot(n2, wu_ref[j]) + bu_ref[j]
        xg = jnp.minimum(h[:, 0:f], SWIGLU_LIMIT)
        xl = jnp.clip(h[:, f:2 * f], -SWIGLU_LIMIT, SWIGLU_LIMIT)
        act = xg * jax.nn.sigmoid(SWIGLU_ALPHA * xg) * (xl + 1.0)
        sel = (lax.broadcasted_iota(jnp.int32, (LANES, f), 0) == eb * e_step + j).astype(BF16)
        ge = _dot(gates_bf, sel)
        y = _dot((act * ge).astype(BF16), wd_ref[j])
        total = y if total is None else total + y
    acc_ref[...] += total

    @pl.when(eb == pl.num_programs(2) - 1)
    def _():
        o_ref[...] = xn_ref[...] + mod_ref[5:6, :] * acc_ref[...]


def moe(n2, gates, xn, mod, wu, bu, wd, bd_p):
    b, l, d = xn.shape
    ne, f, _ = wd.shape
    tm = _tile(l, 1024)
    e_step = _tile(ne, 4)

    def tok(w):
        return pl.BlockSpec((None, tm, w), lambda bi, i, e: (bi, i, 0))

    return pl.pallas_call(
        functools.partial(_moe_kernel, e_step=e_step),
        grid=(b, l // tm, ne // e_step),
        in_specs=[tok(d), tok(LANES), tok(d),
                  pl.BlockSpec((None, N_MOD, d), lambda bi, i, e: (bi, 0, 0)),
                  pl.BlockSpec((e_step, d, 2 * f), lambda bi, i, e: (e, 0, 0)),
                  pl.BlockSpec((e_step, 1, 2 * f), lambda bi, i, e: (e, 0, 0)),
                  pl.BlockSpec((e_step, f, d), lambda bi, i, e: (e, 0, 0)),
                  pl.BlockSpec((LANES, d), lambda bi, i, e: (0, 0))],
        out_specs=tok(d),
        out_shape=jax.ShapeDtypeStruct((b, l, d), F32),
        scratch_shapes=[pltpu.VMEM((tm, d), F32)],
        compiler_params=_cparams("parallel", "parallel", "arbitrary"),
        name="moe",
    )(n2, gates, xn, mod, wu, bu, wd, bd_p)


def _rope_tables(l):
    rows = l // GRID_W
    row = jnp.repeat(jnp.arange(rows, dtype=F32), GRID_W)
    col = jnp.tile(jnp.arange(GRID_W, dtype=F32), rows)
    n_freq = HEAD_DIM // 4
    freqs = ROPE_THETA ** (-jnp.arange(n_freq, dtype=F32) / n_freq)
    ang_r = row[:, None] * freqs
    ang_c = col[:, None] * freqs
    cos = jnp.concatenate([jnp.cos(ang_r), jnp.cos(ang_r), jnp.cos(ang_c), jnp.cos(ang_c)], axis=-1)
    sin = jnp.concatenate([-jnp.sin(ang_r), jnp.sin(ang_r), -jnp.sin(ang_c), jnp.sin(ang_c)], axis=-1)
    return cos, sin


def _encoder_layer(x, mod, p):
    b, l, d = x.shape
    c = p['filt_bias'].shape[0]
    cos, sin_signed = _rope_tables(l)
    uh, q, k, v, gh, ga = inproj(x, mod, p['norm_mix'], p['w_in'], cos, sin_signed, p['q_norm'], p['k_norm'],
                                 3 * c, N_Q_HEADS * HEAD_DIM, N_KV_HEADS * HEAD_DIM)
    x0, uu = hyena_pre(uh, p['conv_w'], p['conv_b'])
    k_circ = hyena_filter(l, p['filt_w1'], p['filt_b1'], p['filt_freq1'], p['filt_w2'], p['filt_b2'],
                          p['filt_freq2'], p['filt_w3'], p['filt_b3'])
    yh = hyena_long_conv(x0, uu, k_circ, p['filt_bias'])
    ya = attention(q, k, v)
    xn, n2, gates = mix_router(x, yh, ya, gh, ga, mod, p['w_out_h'], p['w_out_a'], p['w_o'], p['norm_ffn'],
                               p['router'])
    return moe(n2, gates, xn, mod, p['wu'], p['bu'], p['wd'], p['bd'])


def kernel(x_prompt, x_sample, c_prompt, c_sample, w_ada, b_ada, norm_mix, w_in, conv_w, conv_b, filt_w1, filt_b1, filt_freq1, filt_w2, filt_b2, filt_freq2, filt_w3, filt_b3, filt_bias, q_norm, k_norm, w_out_h, w_out_a, w_o, norm_ffn, router_w, router_b, w_up, b_up, w_down, b_down):
    depth = w_ada.shape[0]
    d = x_prompt.shape[-1]
    bp = c_prompt.shape[0]
    bs = c_sample.shape[0]
    ne = router_w.shape[-1]
    y_prompt, y_sample = x_prompt, x_sample
    for i in range(depth):
        rows = -(-(bp + bs) // 8) * 8
        c_all = jnp.pad(jnp.concatenate([c_prompt, c_sample], axis=0), ((0, rows - bp - bs), (0, 0)))
        mod = adaln(c_all, w_ada[i], b_ada[i]).reshape(rows, N_MOD, d)
        p = {
            'norm_mix': norm_mix[i], 'w_in': w_in[i].astype(BF16),
            'conv_w': conv_w[i], 'conv_b': conv_b[i],
            'filt_w1': filt_w1[i], 'filt_b1': filt_b1[i], 'filt_freq1': filt_freq1[i],
            'filt_w2': filt_w2[i], 'filt_b2': filt_b2[i], 'filt_freq2': filt_freq2[i],
            'filt_w3': filt_w3[i], 'filt_b3': filt_b3[i], 'filt_bias': filt_bias[i],
            'q_norm': q_norm[i], 'k_norm': k_norm[i],
            'w_out_h': w_out_h[i].astype(BF16), 'w_out_a': w_out_a[i].astype(BF16), 'w_o': w_o[i].astype(BF16),
            'norm_ffn': norm_ffn[i],
            'router': pack_router(router_w[i], router_b[i]),
            'wu': deinterleave_up(w_up[i]),
            'bu': jnp.concatenate([b_up[i][:, None, 0::2], b_up[i][:, None, 1::2]], axis=-1),
            'wd': w_down[i].astype(BF16),
            'bd': jnp.pad(b_down[i], ((0, LANES - ne), (0, 0))),
        }
        y_prompt = _encoder_layer(y_prompt, mod[:bp], p)
        y_sample = _encoder_layer(y_sample, mod[bp:bp + bs], p)
    return (y_prompt, y_sample)
```

```python
import functools
import math

import jax
import jax.numpy as jnp
from jax import lax
from jax.experimental import pallas as pl
from jax.experimental.pallas import tpu as pltpu

F32 = jnp.float32
BF16 = jnp.bfloat16

EPS = 1e-6
N_MOD = 6
GRID_W = 64
HEAD_DIM = 128
N_Q_HEADS = 4
N_KV_HEADS = 2
ROPE_THETA = 10000.0
TOP_K = 4
SWIGLU_ALPHA = 1.702
SWIGLU_LIMIT = 7.0
FILTER_EMB = 33
FILTER_BANDS = (FILTER_EMB - 1) // 2
DECAY_TARGET = 1e-2
MIN_DECAY = math.log(DECAY_TARGET) / 1.5
MAX_DECAY = math.log(DECAY_TARGET) / 0.3

LANES = 128
NEG_BIG = -1e30
VMEM_LIMIT = 56 * 1024 * 1024


def _cparams(*sem):
    return pltpu.CompilerParams(dimension_semantics=sem, vmem_limit_bytes=VMEM_LIMIT)


def _dot(a, b):
    return jnp.dot(a, b, preferred_element_type=F32)


def _dot_hi(a, b):
    return jnp.dot(a, b, preferred_element_type=F32, precision=lax.Precision.HIGHEST)


def _tile(n, want):
    t = min(n, want)
    assert n % t == 0, (n, want)
    return t


def _adaln_kernel(c_ref, w_ref, b_ref, o_ref):
    c = c_ref[...]
    o_ref[...] = _dot_hi(c * jax.nn.sigmoid(c), w_ref[...]) + b_ref[...]


def adaln(c, w_ada, b_ada):
    r, d = c.shape
    n = w_ada.shape[1]
    tn = _tile(n, 1536)
    return pl.pallas_call(
        _adaln_kernel,
        grid=(n // tn,),
        in_specs=[pl.BlockSpec((r, d), lambda j: (0, 0)),
                  pl.BlockSpec((d, tn), lambda j: (0, j)),
                  pl.BlockSpec((1, tn), lambda j: (0, j))],
        out_specs=pl.BlockSpec((r, tn), lambda j: (0, j)),
        out_shape=jax.ShapeDtypeStruct((r, n), F32),
        compiler_params=_cparams("arbitrary"),
        name="adaln",
    )(c, w_ada, b_ada.reshape(1, n))


def _rope(xn, cos, sin_signed):
    lane = lax.broadcasted_iota(jnp.int32, xn.shape, 1)
    first_half = (lane % 64) < 32
    rot = jnp.where(first_half, pltpu.roll(xn, 96, 1), pltpu.roll(xn, 32, 1))
    return xn * cos + rot * sin_signed


def _inproj_kernel(x_ref, mod_ref, g_ref, w_ref, cos_ref, sin_ref, qg_ref, kg_ref,
                   uh_ref, q_ref, k_ref, v_ref, gh_ref, ga_ref, *, d_hy3, d_attn, d_kv, d_model, q_scale):
    x = x_ref[...]
    y = x * lax.rsqrt(jnp.mean(x * x, axis=-1, keepdims=True) + EPS) * g_ref[...]
    n = (y * (1.0 + mod_ref[1:2, :]) + mod_ref[0:1, :]).astype(BF16)

    s0 = d_hy3
    s1 = s0 + d_attn
    s2 = s1 + d_kv
    s3 = s2 + d_kv
    s4 = s3 + d_model
    cos = cos_ref[...]
    sin = sin_ref[...]

    def norm_rope(z, gain):
        zn = z * lax.rsqrt(jnp.mean(z * z, axis=-1, keepdims=True) + EPS) * gain
        return _rope(zn, cos, sin)

    qf = _dot(n, w_ref[:, s0:s1])
    kf = _dot(n, w_ref[:, s1:s2])
    for h in range(d_attn // HEAD_DIM):
        sl = slice(h * HEAD_DIM, (h + 1) * HEAD_DIM)
        q_ref[:, sl] = (norm_rope(qf[:, sl], qg_ref[...]) * q_scale).astype(BF16)
    for h in range(d_kv // HEAD_DIM):
        sl = slice(h * HEAD_DIM, (h + 1) * HEAD_DIM)
        k_ref[:, sl] = norm_rope(kf[:, sl], kg_ref[...]).astype(BF16)

    uh_ref[...] = _dot(n, w_ref[:, 0:s0]).astype(BF16)
    v_ref[...] = _dot(n, w_ref[:, s2:s3]).astype(BF16)
    gh_ref[...] = _dot(n, w_ref[:, s3:s4]).astype(BF16)
    ga_ref[...] = _dot(n, w_ref[:, s4:s4 + d_model]).astype(BF16)


def inproj(x, mod, norm_g, w_in_bf, cos, sin_signed, q_norm, k_norm, d_hy3, d_attn, d_kv):
    b, l, d = x.shape
    tm = _tile(l, 512)
    d_in = w_in_bf.shape[1]
    q_scale = math.log2(math.e) / math.sqrt(HEAD_DIM)
    kern = functools.partial(_inproj_kernel, d_hy3=d_hy3, d_attn=d_attn, d_kv=d_kv, d_model=d, q_scale=q_scale)

    def tok(w):
        return pl.BlockSpec((None, tm, w), lambda bi, i: (bi, i, 0))

    def const(shape):
        return pl.BlockSpec(shape, lambda bi, i: (0,) * len(shape))

    outs = [d_hy3, d_attn, d_kv, d_kv, d, d]
    return pl.pallas_call(
        kern,
        grid=(b, l // tm),
        in_specs=[tok(d),
                  pl.BlockSpec((None, N_MOD, d), lambda bi, i: (bi, 0, 0)),
                  const((1, d)),
                  const((d, d_in)),
                  pl.BlockSpec((tm, HEAD_DIM), lambda bi, i: (i, 0)),
                  pl.BlockSpec((tm, HEAD_DIM), lambda bi, i: (i, 0)),
                  const((1, HEAD_DIM)),
                  const((1, HEAD_DIM))],
        out_specs=[tok(w) for w in outs],
        out_shape=[jax.ShapeDtypeStruct((b, l, w), BF16) for w in outs],
        compiler_params=_cparams("parallel", "parallel"),
        name="inproj",
    )(x, mod, norm_g.reshape(1, d), w_in_bf, cos, sin_signed, q_norm.reshape(1, HEAD_DIM), k_norm.reshape(1, HEAD_DIM))


def _hyena_pre_kernel(u_ref, prev_ref, next_ref, w_ref, b_ref, x0_ref, uu_ref, *, c):
    i = pl.program_id(1)
    nblk = pl.num_programs(1)
    u = u_ref[...].astype(F32)
    tl = u.shape[0]
    row = lax.broadcasted_iota(jnp.int32, u.shape, 0)
    prev_row = prev_ref[7:8, :].astype(F32) * jnp.where(i > 0, 1.0, 0.0)
    next_row = next_ref[0:1, :].astype(F32) * jnp.where(i < nblk - 1, 1.0, 0.0)
    up = jnp.where(row == 0, prev_row, pltpu.roll(u, 1, 0))
    un = jnp.where(row == tl - 1, next_row, pltpu.roll(u, tl - 1, 0))
    y = up * w_ref[0:1, :] + u * w_ref[1:2, :] + un * w_ref[2:3, :] + b_ref[...]
    x0_ref[...] = y[:, 0:c].astype(BF16)
    uu_ref[...] = (y[:, c:2 * c] * y[:, 2 * c:3 * c]).astype(BF16)


def hyena_pre(uh, conv_w, conv_b):
    b, l, c3 = uh.shape
    c = c3 // 3
    tl = _tile(l, 512)
    r = tl // 8
    nrow8 = l // 8
    kern = functools.partial(_hyena_pre_kernel, c=c)
    return pl.pallas_call(
        kern,
        grid=(b, l // tl),
        in_specs=[pl.BlockSpec((None, tl, c3), lambda bi, i: (bi, i, 0)),
                  pl.BlockSpec((None, 8, c3), lambda bi, i: (bi, jnp.maximum(i * r - 1, 0), 0)),
                  pl.BlockSpec((None, 8, c3), lambda bi, i: (bi, jnp.minimum((i + 1) * r, nrow8 - 1), 0)),
                  pl.BlockSpec((3, c3), lambda bi, i: (0, 0)),
                  pl.BlockSpec((1, c3), lambda bi, i: (0, 0))],
        out_specs=[pl.BlockSpec((None, tl, c), lambda bi, i: (bi, i, 0)),
                   pl.BlockSpec((None, tl, c), lambda bi, i: (bi, i, 0))],
        out_shape=[jax.ShapeDtypeStruct((b, l, c), BF16)] * 2,
        compiler_params=_cparams("parallel", "parallel"),
        name="hyena_pre",
    )(uh, uh, uh, conv_w, conv_b.reshape(1, c3))


def _filter_kernel(z_ref, t_ref, w1_ref, b1_ref, f1_ref, w2_ref, b2_ref, f2_ref, w3_ref, b3_ref, dl_ref, o_ref, *, l):
    h = jnp.sin(f1_ref[...] * (_dot_hi(z_ref[...], w1_ref[...]) + b1_ref[...]))
    h = jnp.sin(f2_ref[...] * (_dot_hi(h, w2_ref[...]) + b2_ref[...]))
    k = (_dot_hi(h, w3_ref[...]) + b3_ref[...]) * jnp.exp(-t_ref[:, 0:1] * dl_ref[...])
    row = pl.program_id(0) * k.shape[0] + lax.broadcasted_iota(jnp.int32, k.shape, 0)
    o_ref[...] = jnp.where(row == l, 0.0, k).astype(BF16)


def hyena_filter(l, w1, b1, f1, w2, b2, f2, w3, b3):
    c2 = w3.shape[1]
    c = c2 // 2
    hid = w1.shape[1]
    t = jnp.linspace(0.0, 1.0, l, dtype=F32)[:, None]
    w = (2.0 * math.pi / l) * jnp.arange(l, dtype=F32)[:, None]
    bands = jnp.linspace(1e-4, FILTER_BANDS - 1, FILTER_BANDS, dtype=F32)[None, :]
    z = jnp.concatenate([t, jnp.cos(bands * w), -jnp.sin(bands * w)], axis=-1)
    z = jnp.pad(z, ((0, 0), (0, LANES - FILTER_EMB)))
    tlane = jnp.broadcast_to(t, (l, LANES))

    def mirrored(a):
        return jnp.concatenate([a, a[0:1], jnp.flip(a[1:], axis=0)], axis=0)

    def padv(v):
        return jnp.pad(v.astype(F32), (0, LANES - hid)).reshape(1, LANES)

    w1p = jnp.pad(w1.astype(F32), ((0, LANES - FILTER_EMB), (0, LANES - hid)))
    w2p = jnp.pad(w2.astype(F32), ((0, LANES - hid), (0, LANES - hid)))
    w3p = jnp.pad(w3.astype(F32), ((0, LANES - hid), (0, 0)))
    deltas = jnp.abs(jnp.linspace(MIN_DECAY, MAX_DECAY, c, dtype=F32)).reshape(1, c)
    tl = _tile(l, 1024)
    nfwd = l // tl

    def const(shape):
        return pl.BlockSpec(shape, lambda i: (0, 0))

    return pl.pallas_call(
        functools.partial(_filter_kernel, l=l),
        grid=(2 * nfwd,),
        in_specs=[pl.BlockSpec((tl, LANES), lambda i: (i, 0)),
                  pl.BlockSpec((tl, LANES), lambda i: (i, 0)),
                  const((LANES, LANES)), const((1, LANES)), const((1, LANES)),
                  const((LANES, LANES)), const((1, LANES)), const((1, LANES)),
                  pl.BlockSpec((LANES, c), lambda i: (0, i // nfwd)),
                  pl.BlockSpec((1, c), lambda i: (0, i // nfwd)),
                  const((1, c))],
        out_specs=pl.BlockSpec((tl, c), lambda i: (i, 0)),
        out_shape=jax.ShapeDtypeStruct((2 * l, c), BF16),
        compiler_params=_cparams("parallel"),
        name="hyena_filter",
    )(mirrored(z), mirrored(tlane), w1p, padv(b1), padv(f1), w2p, padv(b2), padv(f2), w3p,
      b3.astype(F32).reshape(1, c2), deltas)


def _fft_split(n):
    n2 = 128
    n1 = n // n2
    assert n1 * n2 == n and n1 % 16 == 0, n
    return n1, n2


def _angles(rows, cols, period):
    prod = (jnp.arange(rows, dtype=jnp.int32)[:, None] * jnp.arange(cols, dtype=jnp.int32)[None, :]) % period
    return prod.astype(F32) * (2.0 * math.pi / period)


def _dft_tables(n1, n2):
    n = n1 * n2
    ang1 = _angles(n1, n1, n1)
    c1, s1 = jnp.cos(ang1), jnp.sin(ang1)
    fa = jnp.concatenate([c1, -s1], axis=0).astype(BF16)
    fc = (jnp.concatenate([c1, -s1], axis=1) * (1.0 / n)).astype(BF16)
    ang2 = _angles(n2, n2, n2)
    c2, s2 = jnp.cos(ang2), jnp.sin(ang2)
    g_fwd = jnp.block([[c2, s2], [-s2, c2]]).astype(BF16)
    g_inv = jnp.block([[c2, -s2], [s2, c2]]).astype(BF16)
    angt = _angles(n1, n2, n)
    twr = jnp.broadcast_to(jnp.cos(angt)[:, :, None], (n1, n2, LANES))
    twi = jnp.broadcast_to(jnp.sin(angt)[:, :, None], (n1, n2, LANES))
    return fa, fc, g_fwd, g_inv, twr, twi


def _fft_a_kernel(f_ref, u_ref, o_ref):
    o_ref[...] = _dot(f_ref[...], u_ref[...]).astype(BF16)


def fft_a(fa, u):
    b, k, w = u.shape
    m = fa.shape[0]
    tn = _tile(w, 2048)
    return pl.pallas_call(
        _fft_a_kernel,
        grid=(b, w // tn),
        in_specs=[pl.BlockSpec((m, k), lambda bi, j: (0, 0)),
                  pl.BlockSpec((None, k, tn), lambda bi, j: (bi, 0, j))],
        out_specs=pl.BlockSpec((None, m, tn), lambda bi, j: (bi, 0, j)),
        out_shape=jax.ShapeDtypeStruct((b, m, w), BF16),
        compiler_params=_cparams("parallel", "parallel"),
        name="fft_a",
    )(fa, u)


def _lane_tile(t, c):
    return jnp.concatenate([t] * (c // LANES), axis=1) if c > LANES else t


def _fft_mf_kernel(a_ref, twr_ref, twi_ref, g_ref, o_ref, *, tk1):
    for j in range(tk1):
        ar = a_ref[0, j].astype(F32)
        ai = a_ref[1, j].astype(F32)
        c = ar.shape[1]
        twr = _lane_tile(twr_ref[j], c)
        twi = _lane_tile(twi_ref[j], c)
        br = ar * twr + ai * twi
        bi = ai * twr - ar * twi
        x = _dot(g_ref[...], jnp.concatenate([br, bi], axis=0).astype(BF16))
        n2 = ar.shape[0]
        o_ref[0, j] = x[:n2].astype(BF16)
        o_ref[1, j] = x[n2:].astype(BF16)


def fft_mf(a, twr, twi, g_fwd):
    _, n1, n2, c = a.shape
    tk1 = 8
    kern = functools.partial(_fft_mf_kernel, tk1=tk1)
    return pl.pallas_call(
        kern,
        grid=(n1 // tk1,),
        in_specs=[pl.BlockSpec((2, tk1, n2, c), lambda i: (0, i, 0, 0)),
                  pl.BlockSpec((tk1, n2, LANES), lambda i: (i, 0, 0)),
                  pl.BlockSpec((tk1, n2, LANES), lambda i: (i, 0, 0)),
                  pl.BlockSpec((2 * n2, 2 * n2), lambda i: (0, 0))],
        out_specs=pl.BlockSpec((2, tk1, n2, c), lambda i: (0, i, 0, 0)),
        out_shape=jax.ShapeDtypeStruct((2, n1, n2, c), BF16),
        compiler_params=_cparams("parallel"),
        name="fft_mf",
    )(a, twr, twi, g_fwd)


def _fft_m_kernel(a_ref, kf_ref, twr_ref, twi_ref, gf_ref, gi_ref, o_ref, *, tk1):
    for j in range(tk1):
        ar = a_ref[0, j].astype(F32)
        ai = a_ref[1, j].astype(F32)
        n2, c = ar.shape
        twr = _lane_tile(twr_ref[j], c)
        twi = _lane_tile(twi_ref[j], c)
        br = ar * twr + ai * twi
        bi = ai * twr - ar * twi
        x = _dot(gf_ref[...], jnp.concatenate([br, bi], axis=0).astype(BF16))
        xr, xi = x[:n2], x[n2:]
        kr = kf_ref[0, j].astype(F32)
        ki = kf_ref[1, j].astype(F32)
        zr = xr * kr - xi * ki
        zi = xr * ki + xi * kr
        y = _dot(gi_ref[...], jnp.concatenate([zr, zi], axis=0).astype(BF16))
        yr, yi = y[:n2], y[n2:]
        o_ref[0, j] = (yr * twr - yi * twi).astype(BF16)
        o_ref[1, j] = (yi * twr + yr * twi).astype(BF16)


def fft_m(a, kf, twr, twi, g_fwd, g_inv):
    b, _, n1, n2, c = a.shape
    tk1 = 8
    kern = functools.partial(_fft_m_kernel, tk1=tk1)
    return pl.pallas_call(
        kern,
        grid=(n1 // tk1, b),
        in_specs=[pl.BlockSpec((None, 2, tk1, n2, c), lambda i, bi: (bi, 0, i, 0, 0)),
                  pl.BlockSpec((2, tk1, n2, c), lambda i, bi: (0, i, 0, 0)),
                  pl.BlockSpec((tk1, n2, LANES), lambda i, bi: (i, 0, 0)),
                  pl.BlockSpec((tk1, n2, LANES), lambda i, bi: (i, 0, 0)),
                  pl.BlockSpec((2 * n2, 2 * n2), lambda i, bi: (0, 0)),
                  pl.BlockSpec((2 * n2, 2 * n2), lambda i, bi: (0, 0))],
        out_specs=pl.BlockSpec((None, 2, tk1, n2, c), lambda i, bi: (bi, 0, i, 0, 0)),
        out_shape=jax.ShapeDtypeStruct(a.shape, BF16),
        compiler_params=_cparams("parallel", "parallel"),
        name="fft_m",
    )(a, kf, twr, twi, g_fwd, g_inv)


def _fft_c_kernel(f_ref, a_ref, x0_ref, uu_ref, bias_ref, o_ref):
    y = _dot(f_ref[...], a_ref[...])
    y = y + uu_ref[...].astype(F32) * bias_ref[...]
    o_ref[...] = (x0_ref[...].astype(F32) * y).astype(BF16)


def fft_c(fc_half, a2, x0, uu, bias_tiled):
    b, k, w = a2.shape
    m = fc_half.shape[0]
    tn = bias_tiled.shape[1]
    return pl.pallas_call(
        _fft_c_kernel,
        grid=(b, w // tn),
        in_specs=[pl.BlockSpec((m, k), lambda bi, j: (0, 0)),
                  pl.BlockSpec((None, k, tn), lambda bi, j: (bi, 0, j)),
                  pl.BlockSpec((None, m, tn), lambda bi, j: (bi, 0, j)),
                  pl.BlockSpec((None, m, tn), lambda bi, j: (bi, 0, j)),
                  pl.BlockSpec((1, tn), lambda bi, j: (0, 0))],
        out_specs=pl.BlockSpec((None, m, tn), lambda bi, j: (bi, 0, j)),
        out_shape=jax.ShapeDtypeStruct((b, m, w), BF16),
        compiler_params=_cparams("parallel", "parallel"),
        name="fft_c",
    )(fc_half, a2, x0, uu, bias_tiled)


def hyena_long_conv(x0, uu, k_circ, filt_bias):
    b, l, c = uu.shape
    n = 2 * l
    n1, n2 = _fft_split(n)
    fa, fc, g_fwd, g_inv, twr, twi = _dft_tables(n1, n2)
    w = n2 * c
    ka = fft_a(fa, k_circ.reshape(1, n1, w))
    kf = fft_mf(ka.reshape(2, n1, n2, c), twr, twi, g_fwd)
    a = fft_a(fa[:, :n1 // 2], uu.reshape(b, n1 // 2, w))
    a2 = fft_m(a.reshape(b, 2, n1, n2, c), kf, twr, twi, g_fwd, g_inv)
    tn = _tile(w, 2048)
    bias_tiled = jnp.tile(filt_bias.astype(F32), tn // c).reshape(1, tn)
    yh = fft_c(fc[:n1 // 2], a2.reshape(b, 2 * n1, w), x0.reshape(b, n1 // 2, w), uu.reshape(b, n1 // 2, w),
               bias_tiled)
    return yh.reshape(b, l, c)


def _attn_kernel(q_ref, k_ref, v_ref, o_ref, qs_ref, m_ref, acc_ref, *, tq, tk, nsub):
    ki = pl.program_id(3)

    @pl.when(ki == 0)
    def _():
        qs_ref[0:tq, :] = q_ref[:, 0:HEAD_DIM]
        qs_ref[tq:2 * tq, :] = q_ref[:, HEAD_DIM:2 * HEAD_DIM]
        m_ref[...] = jnp.full(m_ref.shape, -jnp.inf, F32)
        acc_ref[...] = jnp.zeros(acc_ref.shape, F32)

    qs = qs_ref[...]
    for j in range(nsub):
        k = k_ref[j * tk:(j + 1) * tk, :]
        v = v_ref[j * tk:(j + 1) * tk, :]
        v_aug = jnp.concatenate([v, jnp.ones_like(v)], axis=1)
        s = lax.dot_general(qs, k, (((1,), (1,)), ((), ())), preferred_element_type=F32)
        m_prev = m_ref[...]
        m_new = jnp.maximum(m_prev, jnp.max(s, axis=-1, keepdims=True))
        alpha = jnp.exp2(m_prev - m_new)
        p = jnp.exp2((s - pltpu.repeat(m_new, tk // LANES, 1)).astype(BF16))
        acc_ref[...] = jnp.concatenate([alpha, alpha], axis=1) * acc_ref[...] + _dot(p, v_aug)
        m_ref[...] = m_new

    @pl.when(ki == pl.num_programs(3) - 1)
    def _():
        o = acc_ref[:, 0:HEAD_DIM] / acc_ref[:, HEAD_DIM:2 * HEAD_DIM]
        o_ref[:, 0:HEAD_DIM] = o[0:tq].astype(BF16)
        o_ref[:, HEAD_DIM:2 * HEAD_DIM] = o[tq:2 * tq].astype(BF16)


def attention(q, k, v):
    b, l, _ = q.shape
    tq = _tile(l, 512)
    tk = _tile(l, 512)
    nsub = _tile(l // tk, 16)
    g = (N_Q_HEADS // N_KV_HEADS) * HEAD_DIM
    kern = functools.partial(_attn_kernel, tq=tq, tk=tk, nsub=nsub)
    tkb = tk * nsub
    return pl.pallas_call(
        kern,
        grid=(b, N_KV_HEADS, l // tq, l // tkb),
        in_specs=[pl.BlockSpec((None, tq, g), lambda bi, h, i, j: (bi, i, h)),
                  pl.BlockSpec((None, tkb, HEAD_DIM), lambda bi, h, i, j: (bi, j, h)),
                  pl.BlockSpec((None, tkb, HEAD_DIM), lambda bi, h, i, j: (bi, j, h))],
        out_specs=pl.BlockSpec((None, tq, g), lambda bi, h, i, j: (bi, i, h)),
        out_shape=jax.ShapeDtypeStruct(q.shape, BF16),
        scratch_shapes=[pltpu.VMEM((2 * tq, HEAD_DIM), BF16),
                        pltpu.VMEM((2 * tq, LANES), F32),
                        pltpu.VMEM((2 * tq, 2 * HEAD_DIM), F32)],
        compiler_params=_cparams("parallel", "parallel", "parallel", "arbitrary"),
        name="attention",
    )(q, k, v)


def _mix_kernel(x_ref, yh_ref, ya_ref, gh_ref, ga_ref, mod_ref, woh_ref, woa_ref, wo_ref, g_ref, rw_ref, rb_ref,
                xn_ref, n2_ref, gates_ref, *, n_exp, n_chunks):
    rows = x_ref.shape[0] // n_chunks
    for ci in range(n_chunks):
        _mix_rows(slice(ci * rows, (ci + 1) * rows), x_ref, yh_ref, ya_ref, gh_ref, ga_ref, mod_ref, woh_ref,
                  woa_ref, wo_ref, g_ref, rw_ref, rb_ref, xn_ref, n2_ref, gates_ref, n_exp)


def _mix_rows(r, x_ref, yh_ref, ya_ref, gh_ref, ga_ref, mod_ref, woh_ref, woa_ref, wo_ref, g_ref, rw_ref, rb_ref,
              xn_ref, n2_ref, gates_ref, n_exp):
    th = _dot(yh_ref[r, :], woh_ref[...])
    ta = _dot(ya_ref[r, :], woa_ref[...])
    mixed = (jax.nn.sigmoid(gh_ref[r, :].astype(F32)) * th + jax.nn.sigmoid(ga_ref[r, :].astype(F32)) * ta)
    mix = _dot(mixed.astype(BF16), wo_ref[...])
    x = x_ref[r, :] + mod_ref[2:3, :] * mix
    xn_ref[r, :] = x

    y = x * lax.rsqrt(jnp.mean(x * x, axis=-1, keepdims=True) + EPS) * g_ref[...]
    n2 = y * (1.0 + mod_ref[4:5, :]) + mod_ref[3:4, :]
    n2_hi = n2.astype(BF16)
    n2_ref[r, :] = n2_hi

    n2_lo = (n2 - n2_hi.astype(F32)).astype(BF16)
    r_hi = _dot(n2_hi, rw_ref[...])
    r_lo = _dot(n2_lo, rw_ref[...])
    lane = lax.broadcasted_iota(jnp.int32, r_hi.shape, 1)
    logits = r_hi + (pltpu.roll(r_hi, LANES - n_exp, 1) + r_lo) + rb_ref[...]
    logits = jnp.where(lane < n_exp, logits, NEG_BIG)

    work = logits
    vals, hots = [], []
    for _ in range(TOP_K):
        m = jnp.max(work, axis=-1, keepdims=True)
        idx = jnp.min(jnp.where(work == m, lane, LANES), axis=-1, keepdims=True)
        hot = lane == idx
        vals.append(m)
        hots.append(hot)
        work = jnp.where(hot, -jnp.inf, work)
    exps = [jnp.exp(v - vals[0]) for v in vals]
    den = exps[0] + exps[1] + exps[2] + exps[3]
    gates = jnp.zeros(logits.shape, F32)
    for hot, e in zip(hots, exps):
        gates = jnp.where(hot, e / den, gates)
    gates_ref[r, :] = gates


def pack_router(router_w, router_b):
    d, ne = router_w.shape
    assert 2 * ne <= LANES
    w_hi, w_lo = _split_bf16(router_w)
    w_p = jnp.concatenate([w_hi, w_lo, jnp.zeros((d, LANES - 2 * ne), BF16)], axis=1)
    return w_p, jnp.pad(router_b, (0, LANES - ne)).reshape(1, LANES), ne


def mix_router(x, yh, ya, gh, ga, mod, w_out_h, w_out_a, w_o, norm_g, router):
    router_w_p, router_b_p, n_exp = router
    b, l, d = x.shape
    tm = _tile(l, 1024)
    n_chunks = 2 if tm % 512 == 0 else 1
    c = yh.shape[2]
    da = ya.shape[2]

    def tok(w):
        return pl.BlockSpec((None, tm, w), lambda bi, i: (bi, i, 0))

    def const(shape):
        return pl.BlockSpec(shape, lambda bi, i: (0,) * len(shape))

    return pl.pallas_call(
        functools.partial(_mix_kernel, n_exp=n_exp, n_chunks=n_chunks),
        grid=(b, l // tm),
        in_specs=[tok(d), tok(c), tok(da), tok(d), tok(d),
                  pl.BlockSpec((None, N_MOD, d), lambda bi, i: (bi, 0, 0)),
                  const((c, d)), const((da, d)), const((d, d)), const((1, d)),
                  const((d, LANES)), const((1, LANES))],
        out_specs=[tok(d), tok(d), tok(LANES)],
        out_shape=[jax.ShapeDtypeStruct((b, l, d), F32),
                   jax.ShapeDtypeStruct((b, l, d), BF16),
                   jax.ShapeDtypeStruct((b, l, LANES), F32)],
        compiler_params=_cparams("parallel", "parallel"),
        name="mix_router",
    )(x, yh, ya, gh, ga, mod, w_out_h, w_out_a, w_o, norm_g.reshape(1, d), router_w_p, router_b_p)


def _deinterleave_kernel(w_ref, p_ref, o_ref):
    o_ref[...] = _dot(w_ref[...].astype(BF16), p_ref[...]).astype(BF16)


def deinterleave_up(w_up):
    ne, d, f2 = w_up.shape
    src = lax.broadcasted_iota(jnp.int32, (f2, f2), 0)
    dst = lax.broadcasted_iota(jnp.int32, (f2, f2), 1)
    perm = (src == jnp.where(dst < f2 // 2, 2 * dst, 2 * (dst - f2 // 2) + 1)).astype(BF16)
    return pl.pallas_call(
        _deinterleave_kernel,
        grid=(ne,),
        in_specs=[pl.BlockSpec((None, d, f2), lambda e: (e, 0, 0)),
                  pl.BlockSpec((f2, f2), lambda e: (0, 0))],
        out_specs=pl.BlockSpec((None, d, f2), lambda e: (e, 0, 0)),
        out_shape=jax.ShapeDtypeStruct((ne, d, f2), BF16),
        compiler_params=_cparams("parallel"),
        name="deinterleave_up",
    )(w_up, perm)


def _split_bf16(a):
    hi = a.astype(BF16)
    return hi, (a - hi.astype(F32)).astype(BF16)


def _moe_kernel(n2_ref, gates_ref, xn_ref, mod_ref, wu_ref, bu_ref, wd_ref, bd_ref, o_ref, acc_ref, *, e_step):
    eb = pl.program_id(2)
    gates = gates_ref[...]

    @pl.when(eb == 0)
    def _():
        b_hi, b_lo = _split_bf16(bd_ref[...])
        g_bf = gates.astype(BF16)
        acc_ref[...] = _dot(g_bf, b_hi) + _dot(g_bf, b_lo)

    n2 = n2_ref[...]
    f = wd_ref.shape[1]
    acts = []
    for j in range(e_step):
        h = _dot(n2, wu_ref[j]) + bu_ref[j]
        xg = jnp.minimum(h[:, 0:f], SWIGLU_LIMIT)
        xl = jnp.clip(h[:, f:2 * f], -SWIGLU_LIMIT, SWIGLU_LIMIT)
        act = xg * jax.nn.sigmoid(SWIGLU_ALPHA * xg) * (xl + 1.0)
        ge = pltpu.roll(gates, (LANES - (eb * e_step + j)) & (LANES - 1), 1)[:, 0:1]
        acts.append((act * ge).astype(BF16))
    acc_ref[...] += _dot(jnp.concatenate(acts, axis=1), wd_ref[...].reshape(e_step * f, wd_ref.shape[2]))

    @pl.when(eb == pl.num_programs(2) - 1)
    def _():
        o_ref[...] = xn_ref[...] + mod_ref[5:6, :] * acc_ref[...]


def moe(n2, gates, xn, mod, wu, bu, wd, bd_p):
    b, l, d = xn.shape
    ne, f, _ = wd.shape
    tm = _tile(l, 1024)
    e_step = _tile(ne, 4)

    def tok(w):
        return pl.BlockSpec((None, tm, w), lambda bi, i, e: (bi, i, 0))

    return pl.pallas_call(
        functools.partial(_moe_kernel, e_step=e_step),
        grid=(b, l // tm, ne // e_step),
        in_specs=[tok(d), tok(LANES), tok(d),
                  pl.BlockSpec((None, N_MOD, d), lambda bi, i, e: (bi, 0, 0)),
                  pl.BlockSpec((e_step, d, 2 * f), lambda bi, i, e: (e, 0, 0)),
                  pl.BlockSpec((e_step, 1, 2 * f), lambda bi, i, e: (e, 0, 0)),
                  pl.BlockSpec((e_step, f, d), lambda bi, i, e: (e, 0, 0)),
                  pl.BlockSpec((LANES, d), lambda bi, i, e: (0, 0))],
        out_specs=tok(d),
        out_shape=jax.ShapeDtypeStruct((b, l, d), F32),
        scratch_shapes=[pltpu.VMEM((tm, d), F32)],
        compiler_params=_cparams("parallel", "parallel", "arbitrary"),
        name="moe",
    )(n2, gates, xn, mod, wu, bu, wd, bd_p)


def _rope_tables(l):
    rows = l // GRID_W
    row = jnp.repeat(jnp.arange(rows, dtype=F32), GRID_W)
    col = jnp.tile(jnp.arange(GRID_W, dtype=F32), rows)
    n_freq = HEAD_DIM // 4
    freqs = ROPE_THETA ** (-jnp.arange(n_freq, dtype=F32) / n_freq)
    ang_r = row[:, None] * freqs
    ang_c = col[:, None] * freqs
    cos = jnp.concatenate([jnp.cos(ang_r), jnp.cos(ang_r), jnp.cos(ang_c), jnp.cos(ang_c)], axis=-1)
    sin = jnp.concatenate([-jnp.sin(ang_r), jnp.sin(ang_r), -jnp.sin(ang_c), jnp.sin(ang_c)], axis=-1)
    return cos, sin


def _encoder_layer(x, mod, p):
    b, l, d = x.shape
    c = p['filt_bias'].shape[0]
    cos, sin_signed = _rope_tables(l)
    uh, q, k, v, gh, ga = inproj(x, mod, p['norm_mix'], p['w_in'], cos, sin_signed, p['q_norm'], p['k_norm'],
                                 3 * c, N_Q_HEADS * HEAD_DIM, N_KV_HEADS * HEAD_DIM)
    x0, uu = hyena_pre(uh, p['conv_w'], p['conv_b'])
    k_circ = hyena_filter(l, p['filt_w1'], p['filt_b1'], p['filt_freq1'], p['filt_w2'], p['filt_b2'],
                          p['filt_freq2'], p['filt_w3'], p['filt_b3'])
    yh = hyena_long_conv(x0, uu, k_circ, p['filt_bias'])
    ya = attention(q, k, v)
    xn, n2, gates = mix_router(x, yh, ya, gh, ga, mod, p['w_out_h'], p['w_out_a'], p['w_o'], p['norm_ffn'],
                               p['router'])
    return moe(n2, gates, xn, mod, p['wu'], p['bu'], p['wd'], p['bd'])


def kernel(x_prompt, x_sample, c_prompt, c_sample, w_ada, b_ada, norm_mix, w_in, conv_w, conv_b, filt_w1, filt_b1, filt_freq1, filt_w2, filt_b2, filt_freq2, filt_w3, filt_b3, filt_bias, q_norm, k_norm, w_out_h, w_out_a, w_o, norm_ffn, router_w, router_b, w_up, b_up, w_down, b_down):
    depth = w_ada.shape[0]
    d = x_prompt.shape[-1]
    bp = c_prompt.shape[0]
    bs = c_sample.shape[0]
    ne = router_w.shape[-1]
    y_prompt, y_sample = x_prompt, x_sample
    for i in range(depth):
        rows = -(-(bp + bs) // 8) * 8
        c_all = jnp.pad(jnp.concatenate([c_prompt, c_sample], axis=0), ((0, rows - bp - bs), (0, 0)))
        mod = adaln(c_all, w_ada[i], b_ada[i]).reshape(rows, N_MOD, d)
        p = {
            'norm_mix': norm_mix[i], 'w_in': w_in[i].astype(BF16),
            'conv_w': conv_w[i], 'conv_b': conv_b[i],
            'filt_w1': filt_w1[i], 'filt_b1': filt_b1[i], 'filt_freq1': filt_freq1[i],
            'filt_w2': filt_w2[i], 'filt_b2': filt_b2[i], 'filt_freq2': filt_freq2[i],
            'filt_w3': filt_w3[i], 'filt_b3': filt_b3[i], 'filt_bias': filt_bias[i],
            'q_norm': q_norm[i], 'k_norm': k_norm[i],
            'w_out_h': w_out_h[i].astype(BF16), 'w_out_a': w_out_a[i].astype(BF16), 'w_o': w_o[i].astype(BF16),
            'norm_ffn': norm_ffn[i],
            'router': pack_router(router_w[i], router_b[i]),
            'wu': deinterleave_up(w_up[i]),
            'bu': jnp.concatenate([b_up[i][:, None, 0::2], b_up[i][:, None, 1::2]], axis=-1),
            'wd': w_down[i].astype(BF16),
            'bd': jnp.pad(b_down[i], ((0, LANES - ne), (0, 0))),
        }
        y_prompt = _encoder_layer(y_prompt, mod[:bp], p)
        y_sample = _encoder_layer(y_sample, mod[bp:bp + bs], p)
    return (y_prompt, y_sample)
```

```python
import functools
import math

import jax
import jax.numpy as jnp
from jax import lax
from jax.experimental import pallas as pl
from jax.experimental.pallas import tpu as pltpu

F32 = jnp.float32
BF16 = jnp.bfloat16

EPS = 1e-6
N_MOD = 6
GRID_W = 64
HEAD_DIM = 128
N_Q_HEADS = 4
N_KV_HEADS = 2
ROPE_THETA = 10000.0
TOP_K = 4
SWIGLU_ALPHA = 1.702
SWIGLU_LIMIT = 7.0
FILTER_EMB = 33
FILTER_BANDS = (FILTER_EMB - 1) // 2
DECAY_TARGET = 1e-2
MIN_DECAY = math.log(DECAY_TARGET) / 1.5
MAX_DECAY = math.log(DECAY_TARGET) / 0.3

LANES = 128
NEG_BIG = -1e30
VMEM_LIMIT = 56 * 1024 * 1024


def _cparams(*sem):
    return pltpu.CompilerParams(dimension_semantics=sem, vmem_limit_bytes=VMEM_LIMIT)


def _dot(a, b):
    return jnp.dot(a, b, preferred_element_type=F32)


def _dot_hi(a, b):
    return jnp.dot(a, b, preferred_element_type=F32, precision=lax.Precision.HIGHEST)


def _tile(n, want):
    t = min(n, want)
    assert n % t == 0, (n, want)
    return t


def _adaln_kernel(c_ref, w_ref, b_ref, o_ref):
    c = c_ref[...]
    o_ref[...] = _dot_hi(c * jax.nn.sigmoid(c), w_ref[...]) + b_ref[...]


def adaln(c, w_ada, b_ada):
    r, d = c.shape
    n = w_ada.shape[1]
    tn = _tile(n, 1536)
    return pl.pallas_call(
        _adaln_kernel,
        grid=(n // tn,),
        in_specs=[pl.BlockSpec((r, d), lambda j: (0, 0)),
                  pl.BlockSpec((d, tn), lambda j: (0, j)),
                  pl.BlockSpec((1, tn), lambda j: (0, j))],
        out_specs=pl.BlockSpec((r, tn), lambda j: (0, j)),
        out_shape=jax.ShapeDtypeStruct((r, n), F32),
        compiler_params=_cparams("arbitrary"),
        name="adaln",
    )(c, w_ada, b_ada.reshape(1, n))


def _rope(xn, cos, sin_signed):
    lane = lax.broadcasted_iota(jnp.int32, xn.shape, 1)
    first_half = (lane % 64) < 32
    rot = jnp.where(first_half, pltpu.roll(xn, 96, 1), pltpu.roll(xn, 32, 1))
    return xn * cos + rot * sin_signed


def _inproj_kernel(x_ref, xp_ref, xn_ref, mod_ref, g_ref, w_ref, cos_ref, sin_ref, qg_ref, kg_ref, cw_ref, cb_ref,
                   x0_ref, uu_ref, q_ref, k_ref, v_ref, gh_ref, ga_ref, *, d_hy3, d_attn, d_kv, d_model, q_scale):
    def norm_mod(x):
        y = x * lax.rsqrt(jnp.mean(x * x, axis=-1, keepdims=True) + EPS) * g_ref[...]
        return (y * (1.0 + mod_ref[1:2, :]) + mod_ref[0:1, :]).astype(BF16)

    n = norm_mod(x_ref[...])

    s0 = d_hy3
    s1 = s0 + d_attn
    s2 = s1 + d_kv
    s3 = s2 + d_kv
    s4 = s3 + d_model
    cos = cos_ref[...]
    sin = sin_ref[...]

    def norm_rope(z, gain):
        zn = z * lax.rsqrt(jnp.mean(z * z, axis=-1, keepdims=True) + EPS) * gain
        return _rope(zn, cos, sin)

    i = pl.program_id(1)
    u = _dot(n, w_ref[:, 0:s0])
    edge = _dot(norm_mod(jnp.concatenate([xp_ref[...], xn_ref[...]], axis=0)), w_ref[:, 0:s0])
    prev_row = edge[7:8, :] * jnp.where(i > 0, 1.0, 0.0)
    next_row = edge[8:9, :] * jnp.where(i < pl.num_programs(1) - 1, 1.0, 0.0)
    tm = u.shape[0]
    row = lax.broadcasted_iota(jnp.int32, u.shape, 0)
    up = jnp.where(row == 0, prev_row, pltpu.roll(u, 1, 0))
    un = jnp.where(row == tm - 1, next_row, pltpu.roll(u, tm - 1, 0))
    yc = up * cw_ref[0:1, :] + u * cw_ref[1:2, :] + un * cw_ref[2:3, :] + cb_ref[...]
    c = d_hy3 // 3
    x0_ref[...] = yc[:, 0:c].astype(BF16)
    uu_ref[...] = (yc[:, c:2 * c] * yc[:, 2 * c:3 * c]).astype(BF16)

    qf = _dot(n, w_ref[:, s0:s1])
    kf = _dot(n, w_ref[:, s1:s2])
    for h in range(d_attn // HEAD_DIM):
        sl = slice(h * HEAD_DIM, (h + 1) * HEAD_DIM)
        q_ref[:, sl] = (norm_rope(qf[:, sl], qg_ref[...]) * q_scale).astype(BF16)
    for h in range(d_kv // HEAD_DIM):
        sl = slice(h * HEAD_DIM, (h + 1) * HEAD_DIM)
        k_ref[:, sl] = norm_rope(kf[:, sl], kg_ref[...]).astype(BF16)

    v_ref[...] = _dot(n, w_ref[:, s2:s3]).astype(BF16)
    gh_ref[...] = _dot(n, w_ref[:, s3:s4]).astype(BF16)
    ga_ref[...] = _dot(n, w_ref[:, s4:s4 + d_model]).astype(BF16)


def inproj(x, mod, norm_g, w_in_bf, cos, sin_signed, q_norm, k_norm, conv_w, conv_b, d_hy3, d_attn, d_kv):
    b, l, d = x.shape
    tm = _tile(l, 512)
    r = tm // 8
    nrow8 = l // 8
    d_in = w_in_bf.shape[1]
    q_scale = math.log2(math.e) / math.sqrt(HEAD_DIM)
    kern = functools.partial(_inproj_kernel, d_hy3=d_hy3, d_attn=d_attn, d_kv=d_kv, d_model=d, q_scale=q_scale)

    def tok(w):
        return pl.BlockSpec((None, tm, w), lambda bi, i: (bi, i, 0))

    def const(shape):
        return pl.BlockSpec(shape, lambda bi, i: (0,) * len(shape))

    outs = [d_hy3 // 3, d_hy3 // 3, d_attn, d_kv, d_kv, d, d]
    return pl.pallas_call(
        kern,
        grid=(b, l // tm),
        in_specs=[tok(d),
                  pl.BlockSpec((None, 8, d), lambda bi, i: (bi, jnp.maximum(i * r - 1, 0), 0)),
                  pl.BlockSpec((None, 8, d), lambda bi, i: (bi, jnp.minimum((i + 1) * r, nrow8 - 1), 0)),
                  pl.BlockSpec((None, N_MOD, d), lambda bi, i: (bi, 0, 0)),
                  const((1, d)),
                  const((d, d_in)),
                  pl.BlockSpec((tm, HEAD_DIM), lambda bi, i: (i, 0)),
                  pl.BlockSpec((tm, HEAD_DIM), lambda bi, i: (i, 0)),
                  const((1, HEAD_DIM)),
                  const((1, HEAD_DIM)),
                  const((3, d_hy3)),
                  const((1, d_hy3))],
        out_specs=[tok(w) for w in outs],
        out_shape=[jax.ShapeDtypeStruct((b, l, w), BF16) for w in outs],
        compiler_params=_cparams("parallel", "parallel"),
        name="inproj",
    )(x, x, x, mod, norm_g.reshape(1, d), w_in_bf, cos, sin_signed, q_norm.reshape(1, HEAD_DIM),
      k_norm.reshape(1, HEAD_DIM), conv_w, conv_b.reshape(1, d_hy3))


def _filter_kernel(w1_ref, b1_ref, f1_ref, w2_ref, b2_ref, f2_ref, w3_ref, b3_ref, dl_ref, o_ref, *, l):
    tl = o_ref.shape[0]
    row = pl.program_id(0) * tl + lax.broadcasted_iota(jnp.int32, (tl, LANES), 0)
    pos = jnp.where(row < l, row, 2 * l - row).astype(F32)
    lane = lax.broadcasted_iota(jnp.int32, (tl, LANES), 1)
    t = pos * (1.0 / (l - 1))
    band_idx = jnp.where(lane <= FILTER_BANDS, lane - 1, lane - 1 - FILTER_BANDS).astype(F32)
    band = 1e-4 + band_idx * ((FILTER_BANDS - 1 - 1e-4) / (FILTER_BANDS - 1))
    phase = jnp.where(lane <= FILTER_BANDS, 0.0, 0.5 * math.pi)
    trig = jnp.cos(band * (pos * (2.0 * math.pi / l)) + phase)
    z = jnp.where(lane == 0, t, jnp.where(lane < FILTER_EMB, trig, 0.0))
    h = jnp.sin(f1_ref[...] * (_dot_hi(z, w1_ref[...]) + b1_ref[...]))
    h = jnp.sin(f2_ref[...] * (_dot_hi(h, w2_ref[...]) + b2_ref[...]))
    k = (_dot_hi(h, w3_ref[...]) + b3_ref[...]) * jnp.exp(-t[:, 0:1] * dl_ref[...])
    o_ref[...] = jnp.where(row[:, 0:1] == l, 0.0, k).astype(BF16)


def hyena_filter(l, w1, b1, f1, w2, b2, f2, w3, b3):
    c2 = w3.shape[1]
    c = c2 // 2
    hid = w1.shape[1]
    assert w1.shape[0] == FILTER_EMB

    def padv(v):
        return jnp.pad(v.astype(F32), (0, LANES - hid)).reshape(1, LANES)

    w1p = jnp.pad(w1.astype(F32), ((0, LANES - FILTER_EMB), (0, LANES - hid)))
    w2p = jnp.pad(w2.astype(F32), ((0, LANES - hid), (0, LANES - hid)))
    w3p = jnp.pad(w3.astype(F32), ((0, LANES - hid), (0, 0)))
    deltas = jnp.abs(jnp.linspace(MIN_DECAY, MAX_DECAY, c, dtype=F32)).reshape(1, c)
    tl = _tile(l, 1024)
    nfwd = l // tl

    def const(shape):
        return pl.BlockSpec(shape, lambda i: (0, 0))

    return pl.pallas_call(
        functools.partial(_filter_kernel, l=l),
        grid=(2 * nfwd,),
        in_specs=[const((LANES, LANES)), const((1, LANES)), const((1, LANES)),
                  const((LANES, LANES)), const((1, LANES)), const((1, LANES)),
                  pl.BlockSpec((LANES, c), lambda i: (0, i // nfwd)),
                  pl.BlockSpec((1, c), lambda i: (0, i // nfwd)),
                  const((1, c))],
        out_specs=pl.BlockSpec((tl, c), lambda i: (i, 0)),
        out_shape=jax.ShapeDtypeStruct((2 * l, c), BF16),
        compiler_params=_cparams("parallel"),
        name="hyena_filter",
    )(w1p, padv(b1), padv(f1), w2p, padv(b2), padv(f2), w3p, b3.astype(F32).reshape(1, c2), deltas)


def _fft_split(n):
    n2 = 128
    n1 = n // n2
    assert n1 * n2 == n and n1 % 16 == 0, n
    return n1, n2


def _angles(rows, cols, period):
    prod = (jnp.arange(rows, dtype=jnp.int32)[:, None] * jnp.arange(cols, dtype=jnp.int32)[None, :]) % period
    return prod.astype(F32) * (2.0 * math.pi / period)


def _dft_tables(n1, n2):
    n = n1 * n2
    nk = n1 // 2 + 8
    ang1 = _angles(nk, n1, n1)
    c1, s1 = jnp.cos(ang1), jnp.sin(ang1)
    fa = jnp.concatenate([c1, -s1], axis=0).astype(BF16)
    k1 = jnp.arange(nk)
    wgt = jnp.where((k1 == 0) | (k1 == n1 // 2), 1.0, jnp.where(k1 < n1 // 2, 2.0, 0.0)) * (1.0 / n)
    fc = jnp.concatenate([c1.T * wgt[None, :], -s1.T * wgt[None, :]], axis=1).astype(BF16)
    ang2 = _angles(n2, n2, n2)
    c2, s2 = jnp.cos(ang2), jnp.sin(ang2)
    g_fwd = jnp.block([[c2, s2], [-s2, c2]]).astype(BF16)
    g_inv = jnp.block([[c2, -s2], [s2, c2]]).astype(BF16)
    angt = _angles(nk, n2, n)
    twr = jnp.broadcast_to(jnp.cos(angt)[:, :, None], (nk, n2, LANES))
    twi = jnp.broadcast_to(jnp.sin(angt)[:, :, None], (nk, n2, LANES))
    return nk, fa, fc, g_fwd, g_inv, twr, twi


def _fft_a_kernel(f_ref, u_ref, o_ref):
    o_ref[...] = _dot(f_ref[...], u_ref[...]).astype(BF16)


def fft_a(fa, u):
    b, k, w = u.shape
    m = fa.shape[0]
    tn = _tile(w, 2048)
    return pl.pallas_call(
        _fft_a_kernel,
        grid=(b, w // tn),
        in_specs=[pl.BlockSpec((m, k), lambda bi, j: (0, 0)),
                  pl.BlockSpec((None, k, tn), lambda bi, j: (bi, 0, j))],
        out_specs=pl.BlockSpec((None, m, tn), lambda bi, j: (bi, 0, j)),
        out_shape=jax.ShapeDtypeStruct((b, m, w), BF16),
        compiler_params=_cparams("parallel", "parallel"),
        name="fft_a",
    )(fa, u)


def _lane_tile(t, c):
    return jnp.concatenate([t] * (c // LANES), axis=1) if c > LANES else t


def _fft_mf_kernel(a_ref, twr_ref, twi_ref, g_ref, o_ref, *, tk1):
    for j in range(tk1):
        ar = a_ref[0, j].astype(F32)
        ai = a_ref[1, j].astype(F32)
        c = ar.shape[1]
        twr = _lane_tile(twr_ref[j], c)
        twi = _lane_tile(twi_ref[j], c)
        br = ar * twr + ai * twi
        bi = ai * twr - ar * twi
        x = _dot(g_ref[...], jnp.concatenate([br, bi], axis=0).astype(BF16))
        n2 = ar.shape[0]
        o_ref[0, j] = x[:n2].astype(BF16)
        o_ref[1, j] = x[n2:].astype(BF16)


def fft_mf(a, twr, twi, g_fwd):
    _, n1, n2, c = a.shape
    tk1 = 8
    kern = functools.partial(_fft_mf_kernel, tk1=tk1)
    return pl.pallas_call(
        kern,
        grid=(n1 // tk1,),
        in_specs=[pl.BlockSpec((2, tk1, n2, c), lambda i: (0, i, 0, 0)),
                  pl.BlockSpec((tk1, n2, LANES), lambda i: (i, 0, 0)),
                  pl.BlockSpec((tk1, n2, LANES), lambda i: (i, 0, 0)),
                  pl.BlockSpec((2 * n2, 2 * n2), lambda i: (0, 0))],
        out_specs=pl.BlockSpec((2, tk1, n2, c), lambda i: (0, i, 0, 0)),
        out_shape=jax.ShapeDtypeStruct((2, n1, n2, c), BF16),
        compiler_params=_cparams("parallel"),
        name="fft_mf",
    )(a, twr, twi, g_fwd)


def _fft_m_kernel(a_ref, kf_ref, twr_ref, twi_ref, gf_ref, gi_ref, o_ref, *, tk1):
    for j in range(tk1):
        ar = a_ref[0, j].astype(F32)
        ai = a_ref[1, j].astype(F32)
        n2, c = ar.shape
        twr = _lane_tile(twr_ref[j], c)
        twi = _lane_tile(twi_ref[j], c)
        br = ar * twr + ai * twi
        bi = ai * twr - ar * twi
        x = _dot(gf_ref[...], jnp.concatenate([br, bi], axis=0).astype(BF16))
        xr, xi = x[:n2], x[n2:]
        kr = kf_ref[0, j].astype(F32)
        ki = kf_ref[1, j].astype(F32)
        zr = xr * kr - xi * ki
        zi = xr * ki + xi * kr
        y = _dot(gi_ref[...], jnp.concatenate([zr, zi], axis=0).astype(BF16))
        yr, yi = y[:n2], y[n2:]
        o_ref[0, j] = (yr * twr - yi * twi).astype(BF16)
        o_ref[1, j] = (yi * twr + yr * twi).astype(BF16)


def fft_m(a, kf, twr, twi, g_fwd, g_inv):
    b, _, n1, n2, c = a.shape
    tk1 = 8
    kern = functools.partial(_fft_m_kernel, tk1=tk1)
    return pl.pallas_call(
        kern,
        grid=(n1 // tk1, b),
        in_specs=[pl.BlockSpec((None, 2, tk1, n2, c), lambda i, bi: (bi, 0, i, 0, 0)),
                  pl.BlockSpec((2, tk1, n2, c), lambda i, bi: (0, i, 0, 0)),
                  pl.BlockSpec((tk1, n2, LANES), lambda i, bi: (i, 0, 0)),
                  pl.BlockSpec((tk1, n2, LANES), lambda i, bi: (i, 0, 0)),
                  pl.BlockSpec((2 * n2, 2 * n2), lambda i, bi: (0, 0)),
                  pl.BlockSpec((2 * n2, 2 * n2), lambda i, bi: (0, 0))],
        out_specs=pl.BlockSpec((None, 2, tk1, n2, c), lambda i, bi: (bi, 0, i, 0, 0)),
        out_shape=jax.ShapeDtypeStruct(a.shape, BF16),
        compiler_params=_cparams("parallel", "parallel"),
        name="fft_m",
    )(a, kf, twr, twi, g_fwd, g_inv)


def _fft_c_kernel(f_ref, a_ref, x0_ref, uu_ref, bias_ref, o_ref):
    y = _dot(f_ref[...], a_ref[...])
    y = y + uu_ref[...].astype(F32) * bias_ref[...]
    o_ref[...] = (x0_ref[...].astype(F32) * y).astype(BF16)


def fft_c(fc_half, a2, x0, uu, bias_tiled):
    b, k, w = a2.shape
    m = fc_half.shape[0]
    tn = bias_tiled.shape[1]
    return pl.pallas_call(
        _fft_c_kernel,
        grid=(b, w // tn),
        in_specs=[pl.BlockSpec((m, k), lambda bi, j: (0, 0)),
                  pl.BlockSpec((None, k, tn), lambda bi, j: (bi, 0, j)),
                  pl.BlockSpec((None, m, tn), lambda bi, j: (bi, 0, j)),
                  pl.BlockSpec((None, m, tn), lambda bi, j: (bi, 0, j)),
                  pl.BlockSpec((1, tn), lambda bi, j: (0, 0))],
        out_specs=pl.BlockSpec((None, m, tn), lambda bi, j: (bi, 0, j)),
        out_shape=jax.ShapeDtypeStruct((b, m, w), BF16),
        compiler_params=_cparams("parallel", "parallel"),
        name="fft_c",
    )(fc_half, a2, x0, uu, bias_tiled)


def hyena_long_conv(x0, uu, k_circ, filt_bias):
    b, l, c = uu.shape
    n = 2 * l
    n1, n2 = _fft_split(n)
    nk, fa, fc, g_fwd, g_inv, twr, twi = _dft_tables(n1, n2)
    w = n2 * c
    ka = fft_a(fa, k_circ.reshape(1, n1, w))
    kf = fft_mf(ka.reshape(2, nk, n2, c), twr, twi, g_fwd)
    a = fft_a(fa[:, :n1 // 2], uu.reshape(b, n1 // 2, w))
    a2 = fft_m(a.reshape(b, 2, nk, n2, c), kf, twr, twi, g_fwd, g_inv)
    tn = _tile(w, 2048)
    bias_tiled = jnp.tile(filt_bias.astype(F32), tn // c).reshape(1, tn)
    yh = fft_c(fc[:n1 // 2], a2.reshape(b, 2 * nk, w), x0.reshape(b, n1 // 2, w), uu.reshape(b, n1 // 2, w),
               bias_tiled)
    return yh.reshape(b, l, c)


def _attn_kernel(q_ref, k_ref, v_ref, o_ref, qs_ref, m_ref, acc_ref, *, tq, tk, nsub):
    ki = pl.program_id(3)

    @pl.when(ki == 0)
    def _():
        qs_ref[0:tq, :] = q_ref[:, 0:HEAD_DIM]
        qs_ref[tq:2 * tq, :] = q_ref[:, HEAD_DIM:2 * HEAD_DIM]
        m_ref[...] = jnp.full(m_ref.shape, -jnp.inf, F32)
        acc_ref[...] = jnp.zeros(acc_ref.shape, F32)

    qs = qs_ref[...]
    for j in range(nsub):
        k = k_ref[j * tk:(j + 1) * tk, :]
        v = v_ref[j * tk:(j + 1) * tk, :]
        v_aug = jnp.concatenate([v, jnp.ones_like(v)], axis=1)
        s = lax.dot_general(qs, k, (((1,), (1,)), ((), ())), preferred_element_type=F32)
        m_prev = m_ref[...]
        m_new = jnp.maximum(m_prev, jnp.max(s, axis=-1, keepdims=True))
        alpha = jnp.exp2(m_prev - m_new)
        p = jnp.exp2((s - pltpu.repeat(m_new, tk // LANES, 1)).astype(BF16))
        acc_ref[...] = jnp.concatenate([alpha, alpha], axis=1) * acc_ref[...] + _dot(p, v_aug)
        m_ref[...] = m_new

    @pl.when(ki == pl.num_programs(3) - 1)
    def _():
        o = acc_ref[:, 0:HEAD_DIM] / acc_ref[:, HEAD_DIM:2 * HEAD_DIM]
        o_ref[:, 0:HEAD_DIM] = o[0:tq].astype(BF16)
        o_ref[:, HEAD_DIM:2 * HEAD_DIM] = o[tq:2 * tq].astype(BF16)


def attention(q, k, v):
    b, l, _ = q.shape
    tq = _tile(l, 512)
    tk = _tile(l, 512)
    nsub = _tile(l // tk, 16)
    g = (N_Q_HEADS // N_KV_HEADS) * HEAD_DIM
    kern = functools.partial(_attn_kernel, tq=tq, tk=tk, nsub=nsub)
    tkb = tk * nsub
    return pl.pallas_call(
        kern,
        grid=(b, N_KV_HEADS, l // tq, l // tkb),
        in_specs=[pl.BlockSpec((None, tq, g), lambda bi, h, i, j: (bi, i, h)),
                  pl.BlockSpec((None, tkb, HEAD_DIM), lambda bi, h, i, j: (bi, j, h)),
                  pl.BlockSpec((None, tkb, HEAD_DIM), lambda bi, h, i, j: (bi, j, h))],
        out_specs=pl.BlockSpec((None, tq, g), lambda bi, h, i, j: (bi, i, h)),
        out_shape=jax.ShapeDtypeStruct(q.shape, BF16),
        scratch_shapes=[pltpu.VMEM((2 * tq, HEAD_DIM), BF16),
                        pltpu.VMEM((2 * tq, LANES), F32),
                        pltpu.VMEM((2 * tq, 2 * HEAD_DIM), F32)],
        compiler_params=_cparams("parallel", "parallel", "parallel", "arbitrary"),
        name="attention",
    )(q, k, v)


def _mix_kernel(x_ref, yh_ref, ya_ref, gh_ref, ga_ref, mod_ref, woh_ref, woa_ref, wo_ref, g_ref, rw_ref, rb_ref,
                xn_ref, n2_ref, gates_ref, *, n_exp, n_chunks):
    rows = x_ref.shape[0] // n_chunks
    for ci in range(n_chunks):
        _mix_rows(slice(ci * rows, (ci + 1) * rows), x_ref, yh_ref, ya_ref, gh_ref, ga_ref, mod_ref, woh_ref,
                  woa_ref, wo_ref, g_ref, rw_ref, rb_ref, xn_ref, n2_ref, gates_ref, n_exp)


def _mix_rows(r, x_ref, yh_ref, ya_ref, gh_ref, ga_ref, mod_ref, woh_ref, woa_ref, wo_ref, g_ref, rw_ref, rb_ref,
              xn_ref, n2_ref, gates_ref, n_exp):
    th = _dot(yh_ref[r, :], woh_ref[...])
    ta = _dot(ya_ref[r, :], woa_ref[...])
    mixed = (jax.nn.sigmoid(gh_ref[r, :].astype(F32)) * th + jax.nn.sigmoid(ga_ref[r, :].astype(F32)) * ta)
    mix = _dot(mixed.astype(BF16), wo_ref[...])
    x = x_ref[r, :] + mod_ref[2:3, :] * mix
    xn_ref[r, :] = x

    y = x * lax.rsqrt(jnp.mean(x * x, axis=-1, keepdims=True) + EPS) * g_ref[...]
    n2 = y * (1.0 + mod_ref[4:5, :]) + mod_ref[3:4, :]
    n2_hi = n2.astype(BF16)
    n2_ref[r, :] = n2_hi

    n2_lo = (n2 - n2_hi.astype(F32)).astype(BF16)
    r_hi = _dot(n2_hi, rw_ref[...])
    r_lo = _dot(n2_lo, rw_ref[...])
    lane = lax.broadcasted_iota(jnp.int32, r_hi.shape, 1)
    logits = r_hi + (pltpu.roll(r_hi, LANES - n_exp, 1) + r_lo) + rb_ref[...]
    logits = jnp.where(lane < n_exp, logits, NEG_BIG)

    work = logits
    vals, hots = [], []
    for _ in range(TOP_K):
        m = jnp.max(work, axis=-1, keepdims=True)
        idx = jnp.min(jnp.where(work == m, lane, LANES), axis=-1, keepdims=True)
        hot = lane == idx
        vals.append(m)
        hots.append(hot)
        work = jnp.where(hot, -jnp.inf, work)
    exps = [jnp.exp(v - vals[0]) for v in vals]
    den = exps[0] + exps[1] + exps[2] + exps[3]
    gates = jnp.zeros(logits.shape, F32)
    for hot, e in zip(hots, exps):
        gates = jnp.where(hot, e / den, gates)
    gates_ref[r, :] = gates


def pack_router(router_w, router_b):
    d, ne = router_w.shape
    assert 2 * ne <= LANES
    w_hi, w_lo = _split_bf16(router_w)
    w_p = jnp.concatenate([w_hi, w_lo, jnp.zeros((d, LANES - 2 * ne), BF16)], axis=1)
    return w_p, jnp.pad(router_b, (0, LANES - ne)).reshape(1, LANES), ne


def mix_router(x, yh, ya, gh, ga, mod, w_out_h, w_out_a, w_o, norm_g, router):
    router_w_p, router_b_p, n_exp = router
    b, l, d = x.shape
    tm = _tile(l, 1024)
    n_chunks = 2 if tm % 512 == 0 else 1
    c = yh.shape[2]
    da = ya.shape[2]

    def tok(w):
        return pl.BlockSpec((None, tm, w), lambda bi, i: (bi, i, 0))

    def const(shape):
        return pl.BlockSpec(shape, lambda bi, i: (0,) * len(shape))

    return pl.pallas_call(
        functools.partial(_mix_kernel, n_exp=n_exp, n_chunks=n_chunks),
        grid=(b, l // tm),
        in_specs=[tok(d), tok(c), tok(da), tok(d), tok(d),
                  pl.BlockSpec((None, N_MOD, d), lambda bi, i: (bi, 0, 0)),
                  const((c, d)), const((da, d)), const((d, d)), const((1, d)),
                  const((d, LANES)), const((1, LANES))],
        out_specs=[tok(d), tok(d), tok(LANES)],
        out_shape=[jax.ShapeDtypeStruct((b, l, d), F32),
                   jax.ShapeDtypeStruct((b, l, d), BF16),
                   jax.ShapeDtypeStruct((b, l, LANES), F32)],
        compiler_params=_cparams("parallel", "parallel"),
        name="mix_router",
    )(x, yh, ya, gh, ga, mod, w_out_h, w_out_a, w_o, norm_g.reshape(1, d), router_w_p, router_b_p)


def _deinterleave_kernel(w_ref, p_ref, o_ref):
    o_ref[...] = _dot(w_ref[...].astype(BF16), p_ref[...]).astype(BF16)


def deinterleave_up(w_up):
    ne, d, f2 = w_up.shape
    src = lax.broadcasted_iota(jnp.int32, (f2, f2), 0)
    dst = lax.broadcasted_iota(jnp.int32, (f2, f2), 1)
    perm = (src == jnp.where(dst < f2 // 2, 2 * dst, 2 * (dst - f2 // 2) + 1)).astype(BF16)
    return pl.pallas_call(
        _deinterleave_kernel,
        grid=(ne,),
        in_specs=[pl.BlockSpec((None, d, f2), lambda e: (e, 0, 0)),
                  pl.BlockSpec((f2, f2), lambda e: (0, 0))],
        out_specs=pl.BlockSpec((None, d, f2), lambda e: (e, 0, 0)),
        out_shape=jax.ShapeDtypeStruct((ne, d, f2), BF16),
        compiler_params=_cparams("parallel"),
        name="deinterleave_up",
    )(w_up, perm)


def _split_bf16(a):
    hi = a.astype(BF16)
    return hi, (a - hi.astype(F32)).astype(BF16)


def _moe_kernel(n2_ref, gates_ref, xn_ref, mod_ref, wu_ref, bu_ref, wd_ref, bd_ref, o_ref, acc_ref, *, e_step):
    eb = pl.program_id(2)
    gates = gates_ref[...]

    @pl.when(eb == 0)
    def _():
        b_hi, b_lo = _split_bf16(bd_ref[...])
        g_bf = gates.astype(BF16)
        acc_ref[...] = _dot(g_bf, b_hi) + _dot(g_bf, b_lo)

    n2 = n2_ref[...]
    f = wd_ref.shape[1]
    acts = []
    for j in range(e_step):
        h = _dot(n2, wu_ref[j]) + bu_ref[j]
        xg = jnp.minimum(h[:, 0:f], SWIGLU_LIMIT)
        xl = jnp.clip(h[:, f:2 * f], -SWIGLU_LIMIT, SWIGLU_LIMIT)
        act = xg * jax.nn.sigmoid(SWIGLU_ALPHA * xg) * (xl + 1.0)
        ge = pltpu.roll(gates, (LANES - (eb * e_step + j)) & (LANES - 1), 1)[:, 0:1]
        acts.append((act * ge).astype(BF16))
    acc_ref[...] += _dot(jnp.concatenate(acts, axis=1), wd_ref[...].reshape(e_step * f, wd_ref.shape[2]))

    @pl.when(eb == pl.num_programs(2) - 1)
    def _():
        o_ref[...] = xn_ref[...] + mod_ref[5:6, :] * acc_ref[...]


def moe(n2, gates, xn, mod, wu, bu, wd, bd_p):
    b, l, d = xn.shape
    ne, f, _ = wd.shape
    tm = _tile(l, 1024)
    e_step = _tile(ne, 4)

    def tok(w):
        return pl.BlockSpec((None, tm, w), lambda bi, i, e: (bi, i, 0))

    return pl.pallas_call(
        functools.partial(_moe_kernel, e_step=e_step),
        grid=(b, l // tm, ne // e_step),
        in_specs=[tok(d), tok(LANES), tok(d),
                  pl.BlockSpec((None, N_MOD, d), lambda bi, i, e: (bi, 0, 0)),
                  pl.BlockSpec((e_step, d, 2 * f), lambda bi, i, e: (e, 0, 0)),
                  pl.BlockSpec((e_step, 1, 2 * f), lambda bi, i, e: (e, 0, 0)),
                  pl.BlockSpec((e_step, f, d), lambda bi, i, e: (e, 0, 0)),
                  pl.BlockSpec((LANES, d), lambda bi, i, e: (0, 0))],
        out_specs=tok(d),
        out_shape=jax.ShapeDtypeStruct((b, l, d), F32),
        scratch_shapes=[pltpu.VMEM((tm, d), F32)],
        compiler_params=_cparams("parallel", "parallel", "arbitrary"),
        name="moe",
    )(n2, gates, xn, mod, wu, bu, wd, bd_p)


def _rope_tables(l):
    rows = l // GRID_W
    row = jnp.repeat(jnp.arange(rows, dtype=F32), GRID_W)
    col = jnp.tile(jnp.arange(GRID_W, dtype=F32), rows)
    n_freq = HEAD_DIM // 4
    freqs = ROPE_THETA ** (-jnp.arange(n_freq, dtype=F32) / n_freq)
    ang_r = row[:, None] * freqs
    ang_c = col[:, None] * freqs
    cos = jnp.concatenate([jnp.cos(ang_r), jnp.cos(ang_r), jnp.cos(ang_c), jnp.cos(ang_c)], axis=-1)
    sin = jnp.concatenate([-jnp.sin(ang_r), jnp.sin(ang_r), -jnp.sin(ang_c), jnp.sin(ang_c)], axis=-1)
    return cos, sin


def _encoder_layer(x, mod, p):
    b, l, d = x.shape
    c = p['filt_bias'].shape[0]
    cos, sin_signed = _rope_tables(l)
    x0, uu, q, k, v, gh, ga = inproj(x, mod, p['norm_mix'], p['w_in'], cos, sin_signed, p['q_norm'], p['k_norm'],
                                     p['conv_w'], p['conv_b'], 3 * c, N_Q_HEADS * HEAD_DIM, N_KV_HEADS * HEAD_DIM)
    k_circ = hyena_filter(l, p['filt_w1'], p['filt_b1'], p['filt_freq1'], p['filt_w2'], p['filt_b2'],
                          p['filt_freq2'], p['filt_w3'], p['filt_b3'])
    yh = hyena_long_conv(x0, uu, k_circ, p['filt_bias'])
    ya = attention(q, k, v)
    xn, n2, gates = mix_router(x, yh, ya, gh, ga, mod, p['w_out_h'], p['w_out_a'], p['w_o'], p['norm_ffn'],
                               p['router'])
    return moe(n2, gates, xn, mod, p['wu'], p['bu'], p['wd'], p['bd'])


def kernel(x_prompt, x_sample, c_prompt, c_sample, w_ada, b_ada, norm_mix, w_in, conv_w, conv_b, filt_w1, filt_b1, filt_freq1, filt_w2, filt_b2, filt_freq2, filt_w3, filt_b3, filt_bias, q_norm, k_norm, w_out_h, w_out_a, w_o, norm_ffn, router_w, router_b, w_up, b_up, w_down, b_down):
    depth = w_ada.shape[0]
    d = x_prompt.shape[-1]
    bp = c_prompt.shape[0]
    bs = c_sample.shape[0]
    ne = router_w.shape[-1]
    y_prompt, y_sample = x_prompt, x_sample
    for i in range(depth):
        rows = -(-(bp + bs) // 8) * 8
        c_all = jnp.pad(jnp.concatenate([c_prompt, c_sample], axis=0), ((0, rows - bp - bs), (0, 0)))
        mod = adaln(c_all, w_ada[i], b_ada[i]).reshape(rows, N_MOD, d)
        p = {
            'norm_mix': norm_mix[i], 'w_in': w_in[i].astype(BF16),
            'conv_w': conv_w[i], 'conv_b': conv_b[i],
            'filt_w1': filt_w1[i], 'filt_b1': filt_b1[i], 'filt_freq1': filt_freq1[i],
            'filt_w2': filt_w2[i], 'filt_b2': filt_b2[i], 'filt_freq2': filt_freq2[i],
            'filt_w3': filt_w3[i], 'filt_b3': filt_b3[i], 'filt_bias': filt_bias[i],
            'q_norm': q_norm[i], 'k_norm': k_norm[i],
            'w_out_h': w_out_h[i].astype(BF16), 'w_out_a': w_out_a[i].astype(BF16), 'w_o': w_o[i].astype(BF16),
            'norm_ffn': norm_ffn[i],
            'router': pack_router(router_w[i], router_b[i]),
            'wu': deinterleave_up(w_up[i]),
            'bu': jnp.concatenate([b_up[i][:, None, 0::2], b_up[i][:, None, 1::2]], axis=-1),
            'wd': w_down[i].astype(BF16),
            'bd': jnp.pad(b_down[i], ((0, LANES - ne), (0, 0))),
        }
        y_prompt = _encoder_layer(y_prompt, mod[:bp], p)
        y_sample = _encoder_layer(y_sample, mod[bp:bp + bs], p)
    return (y_prompt, y_sample)
```

```python
import functools
import math

import jax
import jax.numpy as jnp
from jax import lax
from jax.experimental import pallas as pl
from jax.experimental.pallas import tpu as pltpu

F32 = jnp.float32
BF16 = jnp.bfloat16

EPS = 1e-6
N_MOD = 6
GRID_W = 64
HEAD_DIM = 128
N_Q_HEADS = 4
N_KV_HEADS = 2
ROPE_THETA = 10000.0
TOP_K = 4
SWIGLU_ALPHA = 1.702
SWIGLU_LIMIT = 7.0
FILTER_EMB = 33
FILTER_BANDS = (FILTER_EMB - 1) // 2
DECAY_TARGET = 1e-2
MIN_DECAY = math.log(DECAY_TARGET) / 1.5
MAX_DECAY = math.log(DECAY_TARGET) / 0.3

LANES = 128
NEG_BIG = -1e30
VMEM_LIMIT = 56 * 1024 * 1024


def _cparams(*sem):
    return pltpu.CompilerParams(dimension_semantics=sem, vmem_limit_bytes=VMEM_LIMIT)


def _dot(a, b):
    return jnp.dot(a, b, preferred_element_type=F32)


def _dot_hi(a, b):
    return jnp.dot(a, b, preferred_element_type=F32, precision=lax.Precision.HIGHEST)


def _tile(n, want):
    t = min(n, want)
    assert n % t == 0, (n, want)
    return t


def _adaln_kernel(c_ref, w_ref, b_ref, o_ref):
    c = c_ref[...]
    o_ref[...] = _dot_hi(c * jax.nn.sigmoid(c), w_ref[...]) + b_ref[...]


def adaln(c, w_ada, b_ada):
    r, d = c.shape
    n = w_ada.shape[1]
    tn = _tile(n, 1536)
    return pl.pallas_call(
        _adaln_kernel,
        grid=(n // tn,),
        in_specs=[pl.BlockSpec((r, d), lambda j: (0, 0)),
                  pl.BlockSpec((d, tn), lambda j: (0, j)),
                  pl.BlockSpec((1, tn), lambda j: (0, j))],
        out_specs=pl.BlockSpec((r, tn), lambda j: (0, j)),
        out_shape=jax.ShapeDtypeStruct((r, n), F32),
        compiler_params=_cparams("arbitrary"),
        name="adaln",
    )(c, w_ada, b_ada.reshape(1, n))


def _rope(xn, cos, sin_signed):
    lane = lax.broadcasted_iota(jnp.int32, xn.shape, 1)
    first_half = (lane % 64) < 32
    rot = jnp.where(first_half, pltpu.roll(xn, 96, 1), pltpu.roll(xn, 32, 1))
    return xn * cos + rot * sin_signed


def _inproj_kernel(x_ref, xp_ref, xn_ref, mod_ref, g_ref, w_ref, cos_ref, sin_ref, qg_ref, kg_ref, cw_ref, cb_ref,
                   x0_ref, uu_ref, q_ref, k_ref, v_ref, gh_ref, ga_ref, *, d_hy3, d_attn, d_kv, d_model, q_scale):
    def norm_mod(x):
        y = x * lax.rsqrt(jnp.mean(x * x, axis=-1, keepdims=True) + EPS) * g_ref[...]
        return (y * (1.0 + mod_ref[1:2, :]) + mod_ref[0:1, :]).astype(BF16)

    n = norm_mod(x_ref[...])

    s0 = d_hy3
    s1 = s0 + d_attn
    s2 = s1 + d_kv
    s3 = s2 + d_kv
    s4 = s3 + d_model
    cos = cos_ref[...]
    sin = sin_ref[...]

    def norm_rope(z, gain):
        zn = z * lax.rsqrt(jnp.mean(z * z, axis=-1, keepdims=True) + EPS) * gain
        return _rope(zn, cos, sin)

    i = pl.program_id(1)
    tm = n.shape[0]
    n_edge = norm_mod(jnp.concatenate([xp_ref[...], xn_ref[...]], axis=0))
    u_ext = _dot(jnp.concatenate([n, n_edge], axis=0), w_ref[:, 0:s0])
    u = u_ext[0:tm]
    prev_row = u_ext[tm + 7:tm + 8, :] * jnp.where(i > 0, 1.0, 0.0)
    next_row = u_ext[tm + 8:tm + 9, :] * jnp.where(i < pl.num_programs(1) - 1, 1.0, 0.0)
    row = lax.broadcasted_iota(jnp.int32, u.shape, 0)
    up = jnp.where(row == 0, prev_row, pltpu.roll(u, 1, 0))
    un = jnp.where(row == tm - 1, next_row, pltpu.roll(u, tm - 1, 0))
    yc = up * cw_ref[0:1, :] + u * cw_ref[1:2, :] + un * cw_ref[2:3, :] + cb_ref[...]
    c = d_hy3 // 3
    x0_ref[...] = yc[:, 0:c].astype(BF16)
    uu_ref[...] = (yc[:, c:2 * c] * yc[:, 2 * c:3 * c]).astype(BF16)

    qf = _dot(n, w_ref[:, s0:s1])
    kf = _dot(n, w_ref[:, s1:s2])
    for h in range(d_attn // HEAD_DIM):
        sl = slice(h * HEAD_DIM, (h + 1) * HEAD_DIM)
        q_ref[:, sl] = (norm_rope(qf[:, sl], qg_ref[...]) * q_scale).astype(BF16)
    for h in range(d_kv // HEAD_DIM):
        sl = slice(h * HEAD_DIM, (h + 1) * HEAD_DIM)
        k_ref[:, sl] = norm_rope(kf[:, sl], kg_ref[...]).astype(BF16)

    v_ref[...] = _dot(n, w_ref[:, s2:s3]).astype(BF16)
    gh_ref[...] = _dot(n, w_ref[:, s3:s4]).astype(BF16)
    ga_ref[...] = _dot(n, w_ref[:, s4:s4 + d_model]).astype(BF16)


def inproj(x, mod, norm_g, w_in_bf, cos, sin_signed, q_norm, k_norm, conv_w, conv_b, d_hy3, d_attn, d_kv):
    b, l, d = x.shape
    tm = _tile(l, 512)
    r = tm // 8
    nrow8 = l // 8
    d_in = w_in_bf.shape[1]
    q_scale = math.log2(math.e) / math.sqrt(HEAD_DIM)
    kern = functools.partial(_inproj_kernel, d_hy3=d_hy3, d_attn=d_attn, d_kv=d_kv, d_model=d, q_scale=q_scale)

    def tok(w):
        return pl.BlockSpec((None, tm, w), lambda bi, i: (bi, i, 0))

    def const(shape):
        return pl.BlockSpec(shape, lambda bi, i: (0,) * len(shape))

    outs = [d_hy3 // 3, d_hy3 // 3, d_attn, d_kv, d_kv, d, d]
    return pl.pallas_call(
        kern,
        grid=(b, l // tm),
        in_specs=[tok(d),
                  pl.BlockSpec((None, 8, d), lambda bi, i: (bi, jnp.maximum(i * r - 1, 0), 0)),
                  pl.BlockSpec((None, 8, d), lambda bi, i: (bi, jnp.minimum((i + 1) * r, nrow8 - 1), 0)),
                  pl.BlockSpec((None, N_MOD, d), lambda bi, i: (bi, 0, 0)),
                  const((1, d)),
                  const((d, d_in)),
                  pl.BlockSpec((tm, HEAD_DIM), lambda bi, i: (i, 0)),
                  pl.BlockSpec((tm, HEAD_DIM), lambda bi, i: (i, 0)),
                  const((1, HEAD_DIM)),
                  const((1, HEAD_DIM)),
                  const((3, d_hy3)),
                  const((1, d_hy3))],
        out_specs=[tok(w) for w in outs],
        out_shape=[jax.ShapeDtypeStruct((b, l, w), BF16) for w in outs],
        compiler_params=_cparams("parallel", "parallel"),
        name="inproj",
    )(x, x, x, mod, norm_g.reshape(1, d), w_in_bf, cos, sin_signed, q_norm.reshape(1, HEAD_DIM),
      k_norm.reshape(1, HEAD_DIM), conv_w, conv_b.reshape(1, d_hy3))


HALF_LANES = LANES // 2


def _filter_kernel(w1_ref, b1_ref, f1_ref, w2_ref, b2_ref, f2_ref, w3a_ref, w3b_ref, b3_ref, dl_ref, o_ref, *, l):
    tl = o_ref.shape[0]
    half = tl // 2
    lane = lax.broadcasted_iota(jnp.int32, (half, LANES), 1)
    feat = lane % HALF_LANES
    row = (pl.program_id(0) * tl + lax.broadcasted_iota(jnp.int32, (half, LANES), 0)
           + jnp.where(lane < HALF_LANES, 0, half))
    pos = jnp.where(row < l, row, 2 * l - row).astype(F32)
    t = pos * (1.0 / (l - 1))
    band_idx = jnp.where(feat <= FILTER_BANDS, feat - 1, feat - 1 - FILTER_BANDS).astype(F32)
    band = 1e-4 + band_idx * ((FILTER_BANDS - 1 - 1e-4) / (FILTER_BANDS - 1))
    phase = jnp.where(feat <= FILTER_BANDS, 0.0, 0.5 * math.pi)
    trig = jnp.cos(band * (pos * (2.0 * math.pi / l)) + phase)
    z = jnp.where(feat == 0, t, jnp.where(feat < FILTER_EMB, trig, 0.0))
    h = jnp.sin(f1_ref[...] * (_dot_hi(z, w1_ref[...]) + b1_ref[...]))
    h = jnp.sin(f2_ref[...] * (_dot_hi(h, w2_ref[...]) + b2_ref[...]))
    for w3_ref, lane0, r0 in ((w3a_ref, 0, 0), (w3b_ref, HALF_LANES, half)):
        k = (_dot_hi(h, w3_ref[...]) + b3_ref[...]) * jnp.exp(-t[:, lane0:lane0 + 1] * dl_ref[...])
        o_ref[r0:r0 + half, :] = jnp.where(row[:, lane0:lane0 + 1] == l, 0.0, k).astype(BF16)


def hyena_filter(l, w1, b1, f1, w2, b2, f2, w3, b3):
    c2 = w3.shape[1]
    c = c2 // 2
    hid = w1.shape[1]
    assert w1.shape[0] == FILTER_EMB and FILTER_EMB <= HALF_LANES and hid <= HALF_LANES

    def pad_half(a, rows):
        return jnp.pad(a.astype(F32), ((0, rows - a.shape[0]), (0, HALF_LANES - a.shape[1])))

    def block_diag2(a):
        z = jnp.zeros_like(a)
        return jnp.block([[a, z], [z, a]])

    def padv(v):
        return jnp.tile(jnp.pad(v.astype(F32), (0, HALF_LANES - hid)), 2).reshape(1, LANES)

    w1p = block_diag2(pad_half(w1, HALF_LANES))
    w2p = block_diag2(pad_half(w2, HALF_LANES))
    w3h = jnp.pad(w3.astype(F32), ((0, HALF_LANES - hid), (0, 0)))
    w3a = jnp.concatenate([w3h, jnp.zeros_like(w3h)], axis=0)
    w3b = jnp.concatenate([jnp.zeros_like(w3h), w3h], axis=0)
    deltas = jnp.abs(jnp.linspace(MIN_DECAY, MAX_DECAY, c, dtype=F32)).reshape(1, c)
    tl = _tile(l, 1024)
    nfwd = l // tl

    def const(shape):
        return pl.BlockSpec(shape, lambda i: (0, 0))

    return pl.pallas_call(
        functools.partial(_filter_kernel, l=l),
        grid=(2 * nfwd,),
        in_specs=[const((LANES, LANES)), const((1, LANES)), const((1, LANES)),
                  const((LANES, LANES)), const((1, LANES)), const((1, LANES)),
                  pl.BlockSpec((LANES, c), lambda i: (0, i // nfwd)),
                  pl.BlockSpec((LANES, c), lambda i: (0, i // nfwd)),
                  pl.BlockSpec((1, c), lambda i: (0, i // nfwd)),
                  const((1, c))],
        out_specs=pl.BlockSpec((tl, c), lambda i: (i, 0)),
        out_shape=jax.ShapeDtypeStruct((2 * l, c), BF16),
        compiler_params=_cparams("parallel"),
        name="hyena_filter",
    )(w1p, padv(b1), padv(f1), w2p, padv(b2), padv(f2), w3a, w3b, b3.astype(F32).reshape(1, c2), deltas)


def _fft_split(n):
    n2 = 128
    n1 = n // n2
    assert n1 * n2 == n and n1 % 16 == 0, n
    return n1, n2


def _angles(rows, cols, period):
    prod = (jnp.arange(rows, dtype=jnp.int32)[:, None] * jnp.arange(cols, dtype=jnp.int32)[None, :]) % period
    return prod.astype(F32) * (2.0 * math.pi / period)


def _dft_tables(n1, n2):
    n = n1 * n2
    nk = n1 // 2 + 8
    ang1 = _angles(nk, n1, n1)
    c1, s1 = jnp.cos(ang1), jnp.sin(ang1)
    fa = jnp.concatenate([c1, -s1], axis=0).astype(BF16)
    k1 = jnp.arange(nk)
    wgt = jnp.where((k1 == 0) | (k1 == n1 // 2), 1.0, jnp.where(k1 < n1 // 2, 2.0, 0.0)) * (1.0 / n)
    fc = jnp.concatenate([c1.T * wgt[None, :], -s1.T * wgt[None, :]], axis=1).astype(BF16)
    ang2 = _angles(n2, n2, n2)
    c2, s2 = jnp.cos(ang2), jnp.sin(ang2)
    g_fwd = jnp.block([[c2, s2], [-s2, c2]]).astype(BF16)
    g_inv = jnp.block([[c2, -s2], [s2, c2]]).astype(BF16)
    angt = _angles(nk, n2, n)
    twr = jnp.broadcast_to(jnp.cos(angt)[:, :, None], (nk, n2, LANES))
    twi = jnp.broadcast_to(jnp.sin(angt)[:, :, None], (nk, n2, LANES))
    return nk, fa, fc, g_fwd, g_inv, twr, twi


def _fft_a_kernel(f_ref, u_ref, o_ref):
    o_ref[...] = _dot(f_ref[...], u_ref[...]).astype(BF16)


def fft_a(fa, u):
    b, k, w = u.shape
    m = fa.shape[0]
    tn = _tile(w, 2048)
    return pl.pallas_call(
        _fft_a_kernel,
        grid=(b, w // tn),
        in_specs=[pl.BlockSpec((m, k), lambda bi, j: (0, 0)),
                  pl.BlockSpec((None, k, tn), lambda bi, j: (bi, 0, j))],
        out_specs=pl.BlockSpec((None, m, tn), lambda bi, j: (bi, 0, j)),
        out_shape=jax.ShapeDtypeStruct((b, m, w), BF16),
        compiler_params=_cparams("parallel", "parallel"),
        name="fft_a",
    )(fa, u)


def _lane_tile(t, c):
    return jnp.concatenate([t] * (c // LANES), axis=1) if c > LANES else t


def _fft_mf_kernel(a_ref, twr_ref, twi_ref, g_ref, o_ref, *, tk1):
    for j in range(tk1):
        ar = a_ref[0, j].astype(F32)
        ai = a_ref[1, j].astype(F32)
        c = ar.shape[1]
        twr = _lane_tile(twr_ref[j], c)
        twi = _lane_tile(twi_ref[j], c)
        br = ar * twr + ai * twi
        bi = ai * twr - ar * twi
        x = _dot(g_ref[...], jnp.concatenate([br, bi], axis=0).astype(BF16))
        n2 = ar.shape[0]
        o_ref[0, j] = x[:n2].astype(BF16)
        o_ref[1, j] = x[n2:].astype(BF16)


def fft_mf(a, twr, twi, g_fwd):
    _, n1, n2, c = a.shape
    tk1 = 8
    kern = functools.partial(_fft_mf_kernel, tk1=tk1)
    return pl.pallas_call(
        kern,
        grid=(n1 // tk1,),
        in_specs=[pl.BlockSpec((2, tk1, n2, c), lambda i: (0, i, 0, 0)),
                  pl.BlockSpec((tk1, n2, LANES), lambda i: (i, 0, 0)),
                  pl.BlockSpec((tk1, n2, LANES), lambda i: (i, 0, 0)),
                  pl.BlockSpec((2 * n2, 2 * n2), lambda i: (0, 0))],
        out_specs=pl.BlockSpec((2, tk1, n2, c), lambda i: (0, i, 0, 0)),
        out_shape=jax.ShapeDtypeStruct((2, n1, n2, c), BF16),
        compiler_params=_cparams("parallel"),
        name="fft_mf",
    )(a, twr, twi, g_fwd)


def _fft_m_kernel(a_ref, kf_ref, twr_ref, twi_ref, gf_ref, gi_ref, o_ref, *, tk1):
    for j in range(tk1):
        ar = a_ref[0, j].astype(F32)
        ai = a_ref[1, j].astype(F32)
        n2, c = ar.shape
        twr = _lane_tile(twr_ref[j], c)
        twi = _lane_tile(twi_ref[j], c)
        br = ar * twr + ai * twi
        bi = ai * twr - ar * twi
        x = _dot(gf_ref[...], jnp.concatenate([br, bi], axis=0).astype(BF16))
        xr, xi = x[:n2], x[n2:]
        kr = kf_ref[0, j].astype(F32)
        ki = kf_ref[1, j].astype(F32)
        zr = xr * kr - xi * ki
        zi = xr * ki + xi * kr
        y = _dot(gi_ref[...], jnp.concatenate([zr, zi], axis=0).astype(BF16))
        yr, yi = y[:n2], y[n2:]
        o_ref[0, j] = (yr * twr - yi * twi).astype(BF16)
        o_ref[1, j] = (yi * twr + yr * twi).astype(BF16)


def fft_m(a, kf, twr, twi, g_fwd, g_inv):
    b, _, n1, n2, c = a.shape
    tk1 = 8
    kern = functools.partial(_fft_m_kernel, tk1=tk1)
    return pl.pallas_call(
        kern,
        grid=(n1 // tk1, b),
        in_specs=[pl.BlockSpec((None, 2, tk1, n2, c), lambda i, bi: (bi, 0, i, 0, 0)),
                  pl.BlockSpec((2, tk1, n2, c), lambda i, bi: (0, i, 0, 0)),
                  pl.BlockSpec((tk1, n2, LANES), lambda i, bi: (i, 0, 0)),
                  pl.BlockSpec((tk1, n2, LANES), lambda i, bi: (i, 0, 0)),
                  pl.BlockSpec((2 * n2, 2 * n2), lambda i, bi: (0, 0)),
                  pl.BlockSpec((2 * n2, 2 * n2), lambda i, bi: (0, 0))],
        out_specs=pl.BlockSpec((None, 2, tk1, n2, c), lambda i, bi: (bi, 0, i, 0, 0)),
        out_shape=jax.ShapeDtypeStruct(a.shape, BF16),
        compiler_params=_cparams("parallel", "parallel"),
        name="fft_m",
    )(a, kf, twr, twi, g_fwd, g_inv)


def _fft_c_kernel(f_ref, a_ref, uu_ref, bias_ref, o_ref):
    y = _dot(f_ref[...], a_ref[...])
    o_ref[...] = (y + uu_ref[...].astype(F32) * bias_ref[...]).astype(BF16)


def fft_c(fc_half, a2, uu, bias_tiled):
    b, k, w = a2.shape
    m = fc_half.shape[0]
    tn = bias_tiled.shape[1]
    return pl.pallas_call(
        _fft_c_kernel,
        grid=(b, w // tn),
        in_specs=[pl.BlockSpec((m, k), lambda bi, j: (0, 0)),
                  pl.BlockSpec((None, k, tn), lambda bi, j: (bi, 0, j)),
                  pl.BlockSpec((None, m, tn), lambda bi, j: (bi, 0, j)),
                  pl.BlockSpec((1, tn), lambda bi, j: (0, 0))],
        out_specs=pl.BlockSpec((None, m, tn), lambda bi, j: (bi, 0, j)),
        out_shape=jax.ShapeDtypeStruct((b, m, w), BF16),
        compiler_params=_cparams("parallel", "parallel"),
        name="fft_c",
    )(fc_half, a2, uu, bias_tiled)


def hyena_long_conv(uu, k_circ, filt_bias):
    b, l, c = uu.shape
    n = 2 * l
    n1, n2 = _fft_split(n)
    nk, fa, fc, g_fwd, g_inv, twr, twi = _dft_tables(n1, n2)
    w = n2 * c
    ka = fft_a(fa, k_circ.reshape(1, n1, w))
    kf = fft_mf(ka.reshape(2, nk, n2, c), twr, twi, g_fwd)
    uu_v = uu.reshape(b, n1 // 2, w)
    a = fft_a(fa[:, :n1 // 2], uu_v)
    a2 = fft_m(a.reshape(b, 2, nk, n2, c), kf, twr, twi, g_fwd, g_inv)
    tn = _tile(w, 2048)
    bias_tiled = jnp.tile(filt_bias.astype(F32), tn // c).reshape(1, tn)
    yc = fft_c(fc[:n1 // 2], a2.reshape(b, 2 * nk, w), uu_v, bias_tiled)
    return yc.reshape(b, l, c)


def _attn_kernel(q_ref, k_ref, v_ref, o_ref, qs_ref, m_ref, acc_ref, *, tq, tk, nsub):
    ki = pl.program_id(3)

    @pl.when(ki == 0)
    def _():
        qs_ref[0:tq, :] = q_ref[:, 0:HEAD_DIM]
        qs_ref[tq:2 * tq, :] = q_ref[:, HEAD_DIM:2 * HEAD_DIM]
        m_ref[...] = jnp.full(m_ref.shape, -jnp.inf, F32)
        acc_ref[...] = jnp.zeros(acc_ref.shape, F32)

    qs = qs_ref[...]
    for j in range(nsub):
        k = k_ref[j * tk:(j + 1) * tk, :]
        v = v_ref[j * tk:(j + 1) * tk, :]
        v_aug = jnp.concatenate([v, jnp.ones_like(v)], axis=1)
        s = lax.dot_general(qs, k, (((1,), (1,)), ((), ())), preferred_element_type=F32)
        m_prev = m_ref[...]
        m_new = jnp.maximum(m_prev, jnp.max(s, axis=-1, keepdims=True))
        alpha = jnp.exp2(m_prev - m_new)
        p = jnp.exp2((s - pltpu.repeat(m_new, tk // LANES, 1)).astype(BF16))
        acc_ref[...] = jnp.concatenate([alpha, alpha], axis=1) * acc_ref[...] + _dot(p, v_aug)
        m_ref[...] = m_new

    @pl.when(ki == pl.num_programs(3) - 1)
    def _():
        o = acc_ref[:, 0:HEAD_DIM] / acc_ref[:, HEAD_DIM:2 * HEAD_DIM]
        o_ref[:, 0:HEAD_DIM] = o[0:tq].astype(BF16)
        o_ref[:, HEAD_DIM:2 * HEAD_DIM] = o[tq:2 * tq].astype(BF16)


def attention(q, k, v):
    b, l, _ = q.shape
    tq = _tile(l, 512)
    tk = _tile(l, 512)
    nsub = _tile(l // tk, 16)
    g = (N_Q_HEADS // N_KV_HEADS) * HEAD_DIM
    kern = functools.partial(_attn_kernel, tq=tq, tk=tk, nsub=nsub)
    tkb = tk * nsub
    return pl.pallas_call(
        kern,
        grid=(b, N_KV_HEADS, l // tq, l // tkb),
        in_specs=[pl.BlockSpec((None, tq, g), lambda bi, h, i, j: (bi, i, h)),
                  pl.BlockSpec((None, tkb, HEAD_DIM), lambda bi, h, i, j: (bi, j, h)),
                  pl.BlockSpec((None, tkb, HEAD_DIM), lambda bi, h, i, j: (bi, j, h))],
        out_specs=pl.BlockSpec((None, tq, g), lambda bi, h, i, j: (bi, i, h)),
        out_shape=jax.ShapeDtypeStruct(q.shape, BF16),
        scratch_shapes=[pltpu.VMEM((2 * tq, HEAD_DIM), BF16),
                        pltpu.VMEM((2 * tq, LANES), F32),
                        pltpu.VMEM((2 * tq, 2 * HEAD_DIM), F32)],
        compiler_params=_cparams("parallel", "parallel", "parallel", "arbitrary"),
        name="attention",
    )(q, k, v)


def _mix_kernel(x_ref, x0_ref, yc_ref, ya_ref, gh_ref, ga_ref, mod_ref, woh_ref, woa_ref, wo_ref, g_ref, rw_ref, rb_ref,
                xn_ref, n2_ref, gates_ref, *, n_exp, n_chunks):
    rows = x_ref.shape[0] // n_chunks
    for ci in range(n_chunks):
        _mix_rows(slice(ci * rows, (ci + 1) * rows), x_ref, x0_ref, yc_ref, ya_ref, gh_ref, ga_ref, mod_ref, woh_ref,
                  woa_ref, wo_ref, g_ref, rw_ref, rb_ref, xn_ref, n2_ref, gates_ref, n_exp)


def _mix_rows(r, x_ref, x0_ref, yc_ref, ya_ref, gh_ref, ga_ref, mod_ref, woh_ref, woa_ref, wo_ref, g_ref, rw_ref, rb_ref,
              xn_ref, n2_ref, gates_ref, n_exp):
    yh = (x0_ref[r, :].astype(F32) * yc_ref[r, :].astype(F32)).astype(BF16)
    th = _dot(yh, woh_ref[...])
    ta = _dot(ya_ref[r, :], woa_ref[...])
    mixed = (jax.nn.sigmoid(gh_ref[r, :].astype(F32)) * th + jax.nn.sigmoid(ga_ref[r, :].astype(F32)) * ta)
    mix = _dot(mixed.astype(BF16), wo_ref[...])
    x = x_ref[r, :] + mod_ref[2:3, :] * mix
    xn_ref[r, :] = x

    y = x * lax.rsqrt(jnp.mean(x * x, axis=-1, keepdims=True) + EPS) * g_ref[...]
    n2 = y * (1.0 + mod_ref[4:5, :]) + mod_ref[3:4, :]
    n2_hi = n2.astype(BF16)
    n2_ref[r, :] = n2_hi

    n2_lo = (n2 - n2_hi.astype(F32)).astype(BF16)
    r_hi = _dot(n2_hi, rw_ref[...])
    r_lo = _dot(n2_lo, rw_ref[...])
    lane = lax.broadcasted_iota(jnp.int32, r_hi.shape, 1)
    logits = r_hi + (pltpu.roll(r_hi, LANES - n_exp, 1) + r_lo) + rb_ref[...]
    logits = jnp.where(lane < n_exp, logits, NEG_BIG)

    work = logits
    vals, hots = [], []
    for _ in range(TOP_K):
        m = jnp.max(work, axis=-1, keepdims=True)
        idx = jnp.min(jnp.where(work == m, lane, LANES), axis=-1, keepdims=True)
        hot = lane == idx
        vals.append(m)
        hots.append(hot)
        work = jnp.where(hot, -jnp.inf, work)
    exps = [jnp.exp(v - vals[0]) for v in vals]
    den = exps[0] + exps[1] + exps[2] + exps[3]
    gates = jnp.zeros(logits.shape, F32)
    for hot, e in zip(hots, exps):
        gates = jnp.where(hot, e / den, gates)
    gates_ref[r, :] = gates


def pack_router(router_w, router_b):
    d, ne = router_w.shape
    assert 2 * ne <= LANES
    w_hi, w_lo = _split_bf16(router_w)
    w_p = jnp.concatenate([w_hi, w_lo, jnp.zeros((d, LANES - 2 * ne), BF16)], axis=1)
    return w_p, jnp.pad(router_b, (0, LANES - ne)).reshape(1, LANES), ne


def mix_router(x, x0, yc, ya, gh, ga, mod, w_out_h, w_out_a, w_o, norm_g, router):
    router_w_p, router_b_p, n_exp = router
    b, l, d = x.shape
    tm = _tile(l, 1024)
    n_chunks = 2 if tm % 512 == 0 else 1
    c = yc.shape[2]
    da = ya.shape[2]

    def tok(w):
        return pl.BlockSpec((None, tm, w), lambda bi, i: (bi, i, 0))

    def const(shape):
        return pl.BlockSpec(shape, lambda bi, i: (0,) * len(shape))

    return pl.pallas_call(
        functools.partial(_mix_kernel, n_exp=n_exp, n_chunks=n_chunks),
        grid=(b, l // tm),
        in_specs=[tok(d), tok(c), tok(c), tok(da), tok(d), tok(d),
                  pl.BlockSpec((None, N_MOD, d), lambda bi, i: (bi, 0, 0)),
                  const((c, d)), const((da, d)), const((d, d)), const((1, d)),
                  const((d, LANES)), const((1, LANES))],
        out_specs=[tok(d), tok(d), tok(LANES)],
        out_shape=[jax.ShapeDtypeStruct((b, l, d), F32),
                   jax.ShapeDtypeStruct((b, l, d), BF16),
                   jax.ShapeDtypeStruct((b, l, LANES), F32)],
        compiler_params=_cparams("parallel", "parallel"),
        name="mix_router",
    )(x, x0, yc, ya, gh, ga, mod, w_out_h, w_out_a, w_o, norm_g.reshape(1, d), router_w_p, router_b_p)


def _deinterleave_kernel(w_ref, p_ref, o_ref):
    o_ref[...] = _dot(w_ref[...].astype(BF16), p_ref[...]).astype(BF16)


def deinterleave_up(w_up):
    ne, d, f2 = w_up.shape
    src = lax.broadcasted_iota(jnp.int32, (f2, f2), 0)
    dst = lax.broadcasted_iota(jnp.int32, (f2, f2), 1)
    perm = (src == jnp.where(dst < f2 // 2, 2 * dst, 2 * (dst - f2 // 2) + 1)).astype(BF16)
    return pl.pallas_call(
        _deinterleave_kernel,
        grid=(ne,),
        in_specs=[pl.BlockSpec((None, d, f2), lambda e: (e, 0, 0)),
                  pl.BlockSpec((f2, f2), lambda e: (0, 0))],
        out_specs=pl.BlockSpec((None, d, f2), lambda e: (e, 0, 0)),
        out_shape=jax.ShapeDtypeStruct((ne, d, f2), BF16),
        compiler_params=_cparams("parallel"),
        name="deinterleave_up",
    )(w_up, perm)


def _split_bf16(a):
    hi = a.astype(BF16)
    return hi, (a - hi.astype(F32)).astype(BF16)


def _moe_kernel(n2_ref, gates_ref, xn_ref, mod_ref, wu_ref, bu_ref, wd_ref, bd_ref, o_ref, acc_ref, *, e_step):
    eb = pl.program_id(2)
    gates = gates_ref[...]

    @pl.when(eb == 0)
    def _():
        b_hi, b_lo = _split_bf16(bd_ref[...])
        g_bf = gates.astype(BF16)
        acc_ref[...] = _dot(g_bf, b_hi) + _dot(g_bf, b_lo)

    n2 = n2_ref[...]
    f = wd_ref.shape[1]
    acts = []
    for j in range(e_step):
        h = _dot(n2, wu_ref[j]) + bu_ref[j]
        xg = jnp.minimum(h[:, 0:f], SWIGLU_LIMIT)
        xl = jnp.clip(h[:, f:2 * f], -SWIGLU_LIMIT, SWIGLU_LIMIT)
        act = xg * jax.nn.sigmoid(SWIGLU_ALPHA * xg) * (xl + 1.0)
        ge = pltpu.roll(gates, (LANES - (eb * e_step + j)) & (LANES - 1), 1)[:, 0:1]
        acts.append((act * ge).astype(BF16))
    acc_ref[...] += _dot(jnp.concatenate(acts, axis=1), wd_ref[...].reshape(e_step * f, wd_ref.shape[2]))

    @pl.when(eb == pl.num_programs(2) - 1)
    def _():
        o_ref[...] = xn_ref[...] + mod_ref[5:6, :] * acc_ref[...]


def moe(n2, gates, xn, mod, wu, bu, wd, bd_p):
    b, l, d = xn.shape
    ne, f, _ = wd.shape
    tm = _tile(l, 1024)
    e_step = _tile(ne, 4)

    def tok(w):
        return pl.BlockSpec((None, tm, w), lambda bi, i, e: (bi, i, 0))

    return pl.pallas_call(
        functools.partial(_moe_kernel, e_step=e_step),
        grid=(b, l // tm, ne // e_step),
        in_specs=[tok(d), tok(LANES), tok(d),
                  pl.BlockSpec((None, N_MOD, d), lambda bi, i, e: (bi, 0, 0)),
                  pl.BlockSpec((e_step, d, 2 * f), lambda bi, i, e: (e, 0, 0)),
                  pl.BlockSpec((e_step, 1, 2 * f), lambda bi, i, e: (e, 0, 0)),
                  pl.BlockSpec((e_step, f, d), lambda bi, i, e: (e, 0, 0)),
                  pl.BlockSpec((LANES, d), lambda bi, i, e: (0, 0))],
        out_specs=tok(d),
        out_shape=jax.ShapeDtypeStruct((b, l, d), F32),
        scratch_shapes=[pltpu.VMEM((tm, d), F32)],
        compiler_params=_cparams("parallel", "parallel", "arbitrary"),
        name="moe",
    )(n2, gates, xn, mod, wu, bu, wd, bd_p)


def _rope_tables(l):
    rows = l // GRID_W
    row = jnp.repeat(jnp.arange(rows, dtype=F32), GRID_W)
    col = jnp.tile(jnp.arange(GRID_W, dtype=F32), rows)
    n_freq = HEAD_DIM // 4
    freqs = ROPE_THETA ** (-jnp.arange(n_freq, dtype=F32) / n_freq)
    ang_r = row[:, None] * freqs
    ang_c = col[:, None] * freqs
    cos = jnp.concatenate([jnp.cos(ang_r), jnp.cos(ang_r), jnp.cos(ang_c), jnp.cos(ang_c)], axis=-1)
    sin = jnp.concatenate([-jnp.sin(ang_r), jnp.sin(ang_r), -jnp.sin(ang_c), jnp.sin(ang_c)], axis=-1)
    return cos, sin


def _encoder_layer(x, mod, p):
    b, l, d = x.shape
    c = p['filt_bias'].shape[0]
    cos, sin_signed = _rope_tables(l)
    x0, uu, q, k, v, gh, ga = inproj(x, mod, p['norm_mix'], p['w_in'], cos, sin_signed, p['q_norm'], p['k_norm'],
                                     p['conv_w'], p['conv_b'], 3 * c, N_Q_HEADS * HEAD_DIM, N_KV_HEADS * HEAD_DIM)
    k_circ = hyena_filter(l, p['filt_w1'], p['filt_b1'], p['filt_freq1'], p['filt_w2'], p['filt_b2'],
                          p['filt_freq2'], p['filt_w3'], p['filt_b3'])
    yc = hyena_long_conv(uu, k_circ, p['filt_bias'])
    ya = attention(q, k, v)
    xn, n2, gates = mix_router(x, x0, yc, ya, gh, ga, mod, p['w_out_h'], p['w_out_a'], p['w_o'], p['norm_ffn'],
                               p['router'])
    return moe(n2, gates, xn, mod, p['wu'], p['bu'], p['wd'], p['bd'])


def kernel(x_prompt, x_sample, c_prompt, c_sample, w_ada, b_ada, norm_mix, w_in, conv_w, conv_b, filt_w1, filt_b1, filt_freq1, filt_w2, filt_b2, filt_freq2, filt_w3, filt_b3, filt_bias, q_norm, k_norm, w_out_h, w_out_a, w_o, norm_ffn, router_w, router_b, w_up, b_up, w_down, b_down):
    depth = w_ada.shape[0]
    d = x_prompt.shape[-1]
    bp = c_prompt.shape[0]
    bs = c_sample.shape[0]
    ne = router_w.shape[-1]
    y_prompt, y_sample = x_prompt, x_sample
    for i in range(depth):
        rows = -(-(bp + bs) // 8) * 8
        c_all = jnp.pad(jnp.concatenate([c_prompt, c_sample], axis=0), ((0, rows - bp - bs), (0, 0)))
        mod = adaln(c_all, w_ada[i], b_ada[i]).reshape(rows, N_MOD, d)
        p = {
            'norm_mix': norm_mix[i], 'w_in': w_in[i].astype(BF16),
            'conv_w': conv_w[i], 'conv_b': conv_b[i],
            'filt_w1': filt_w1[i], 'filt_b1': filt_b1[i], 'filt_freq1': filt_freq1[i],
            'filt_w2': filt_w2[i], 'filt_b2': filt_b2[i], 'filt_freq2': filt_freq2[i],
            'filt_w3': filt_w3[i], 'filt_b3': filt_b3[i], 'filt_bias': filt_bias[i],
            'q_norm': q_norm[i], 'k_norm': k_norm[i],
            'w_out_h': w_out_h[i].astype(BF16), 'w_out_a': w_out_a[i].astype(BF16), 'w_o': w_o[i].astype(BF16),
            'norm_ffn': norm_ffn[i],
            'router': pack_router(router_w[i], router_b[i]),
            'wu': deinterleave_up(w_up[i]),
            'bu': jnp.concatenate([b_up[i][:, None, 0::2], b_up[i][:, None, 1::2]], axis=-1),
            'wd': w_down[i].astype(BF16),
            'bd': jnp.pad(b_down[i], ((0, LANES - ne), (0, 0))),
        }
        y_prompt = _encoder_layer(y_prompt, mod[:bp], p)
        y_sample = _encoder_layer(y_sample, mod[bp:bp + bs], p)
    return (y_prompt, y_sample)
```

```python
import functools
import math

import jax
import jax.numpy as jnp
from jax import lax
from jax.experimental import pallas as pl
from jax.experimental.pallas import tpu as pltpu

F32 = jnp.float32
BF16 = jnp.bfloat16

EPS = 1e-6
N_MOD = 6
GRID_W = 64
HEAD_DIM = 128
N_Q_HEADS = 4
N_KV_HEADS = 2
ROPE_THETA = 10000.0
TOP_K = 4
SWIGLU_ALPHA = 1.702
SWIGLU_LIMIT = 7.0
FILTER_EMB = 33
FILTER_BANDS = (FILTER_EMB - 1) // 2
DECAY_TARGET = 1e-2
MIN_DECAY = math.log(DECAY_TARGET) / 1.5
MAX_DECAY = math.log(DECAY_TARGET) / 0.3

LANES = 128
SUBLANES = 8
NEG_BIG = -1e30
VMEM_LIMIT = 56 * 1024 * 1024

TN_ADALN = 1536
TM_INPROJ = 512
TL_FILTER = 1024
TN_DFT = 4096
TK1_DFT = 8
TQ_ATTN = 512
TK_ATTN = 512
NSUB_ATTN = 16
TM_MIX = 1024
CHUNKS_MIX = 2
TM_MOE = 1024
E_STEP_MOE = 4


def _cparams(*sem):
    return pltpu.CompilerParams(dimension_semantics=sem, vmem_limit_bytes=VMEM_LIMIT)


def _dot(a, b):
    return jnp.dot(a, b, preferred_element_type=F32)


def _dot_hi(a, b):
    return jnp.dot(a, b, preferred_element_type=F32, precision=lax.Precision.HIGHEST)


def _tile(n, want):
    t = min(n, want)
    assert n % t == 0, (n, want)
    return t


def _adaln_kernel(c_ref, w_ref, b_ref, o_ref):
    c = c_ref[...]
    o_ref[...] = _dot_hi(c * jax.nn.sigmoid(c), w_ref[...]) + b_ref[...]


def adaln(c, w_ada, b_ada):
    r, d = c.shape
    n = w_ada.shape[1]
    tn = _tile(n, TN_ADALN)
    return pl.pallas_call(
        _adaln_kernel,
        grid=(n // tn,),
        in_specs=[pl.BlockSpec((r, d), lambda j: (0, 0)),
                  pl.BlockSpec((d, tn), lambda j: (0, j)),
                  pl.BlockSpec((1, tn), lambda j: (0, j))],
        out_specs=pl.BlockSpec((r, tn), lambda j: (0, j)),
        out_shape=jax.ShapeDtypeStruct((r, n), F32),
        compiler_params=_cparams("arbitrary"),
        name="adaln",
    )(c, w_ada, b_ada.reshape(1, n))


def _rope(xn, cos, sin_signed):
    q = HEAD_DIM // 4
    lane = lax.broadcasted_iota(jnp.int32, xn.shape, 1)
    first_half = (lane % (2 * q)) < q
    rot = jnp.where(first_half, pltpu.roll(xn, HEAD_DIM - q, 1), pltpu.roll(xn, q, 1))
    return xn * cos + rot * sin_signed


def _inproj_kernel(x_ref, xp_ref, xn_ref, mod_ref, g_ref, w_ref, cos_ref, sin_ref, qg_ref, kg_ref, cw_ref, cb_ref,
                   x0_ref, uu_ref, q_ref, k_ref, v_ref, gh_ref, ga_ref, *, d_hy3, d_attn, d_kv, d_model, q_scale):
    def norm_mod(x):
        y = x * lax.rsqrt(jnp.mean(x * x, axis=-1, keepdims=True) + EPS) * g_ref[...]
        return (y * (1.0 + mod_ref[1:2, :]) + mod_ref[0:1, :]).astype(BF16)

    n = norm_mod(x_ref[...])

    s0 = d_hy3
    s1 = s0 + d_attn
    s2 = s1 + d_kv
    s3 = s2 + d_kv
    s4 = s3 + d_model
    cos = cos_ref[...]
    sin = sin_ref[...]

    def norm_rope(z, gain):
        zn = z * lax.rsqrt(jnp.mean(z * z, axis=-1, keepdims=True) + EPS) * gain
        return _rope(zn, cos, sin)

    i = pl.program_id(1)
    tm = n.shape[0]
    n_edge = norm_mod(jnp.concatenate([xp_ref[...], xn_ref[...]], axis=0))
    u_ext = _dot(jnp.concatenate([n, n_edge], axis=0), w_ref[:, 0:s0])
    u = u_ext[0:tm]
    prev_row = u_ext[tm + 7:tm + 8, :] * jnp.where(i > 0, 1.0, 0.0)
    next_row = u_ext[tm + 8:tm + 9, :] * jnp.where(i < pl.num_programs(1) - 1, 1.0, 0.0)
    row = lax.broadcasted_iota(jnp.int32, u.shape, 0)
    up = jnp.where(row == 0, prev_row, pltpu.roll(u, 1, 0))
    un = jnp.where(row == tm - 1, next_row, pltpu.roll(u, tm - 1, 0))
    yc = up * cw_ref[0:1, :] + u * cw_ref[1:2, :] + un * cw_ref[2:3, :] + cb_ref[...]
    c = d_hy3 // 3
    x0_ref[...] = yc[:, 0:c].astype(BF16)
    uu_ref[...] = (yc[:, c:2 * c] * yc[:, 2 * c:3 * c]).astype(BF16)

    qf = _dot(n, w_ref[:, s0:s1])
    kf = _dot(n, w_ref[:, s1:s2])
    for h in range(d_attn // HEAD_DIM):
        sl = slice(h * HEAD_DIM, (h + 1) * HEAD_DIM)
        q_ref[:, sl] = (norm_rope(qf[:, sl], qg_ref[...]) * q_scale).astype(BF16)
    for h in range(d_kv // HEAD_DIM):
        sl = slice(h * HEAD_DIM, (h + 1) * HEAD_DIM)
        k_ref[:, sl] = norm_rope(kf[:, sl], kg_ref[...]).astype(BF16)

    v_ref[...] = _dot(n, w_ref[:, s2:s3]).astype(BF16)
    gh_ref[...] = _dot(n, w_ref[:, s3:s4]).astype(BF16)
    ga_ref[...] = _dot(n, w_ref[:, s4:s4 + d_model]).astype(BF16)


def inproj(x, mod, norm_g, w_in_bf, cos, sin_signed, q_norm, k_norm, conv_w, conv_b, d_hy3, d_attn, d_kv):
    b, l, d = x.shape
    tm = _tile(l, TM_INPROJ)
    r = tm // SUBLANES
    nrow8 = l // SUBLANES
    d_in = w_in_bf.shape[1]
    q_scale = math.log2(math.e) / math.sqrt(HEAD_DIM)
    kern = functools.partial(_inproj_kernel, d_hy3=d_hy3, d_attn=d_attn, d_kv=d_kv, d_model=d, q_scale=q_scale)

    def tok(w):
        return pl.BlockSpec((None, tm, w), lambda bi, i: (bi, i, 0))

    def const(shape):
        return pl.BlockSpec(shape, lambda bi, i: (0,) * len(shape))

    outs = [d_hy3 // 3, d_hy3 // 3, d_attn, d_kv, d_kv, d, d]
    return pl.pallas_call(
        kern,
        grid=(b, l // tm),
        in_specs=[tok(d),
                  pl.BlockSpec((None, SUBLANES, d), lambda bi, i: (bi, jnp.maximum(i * r - 1, 0), 0)),
                  pl.BlockSpec((None, SUBLANES, d), lambda bi, i: (bi, jnp.minimum((i + 1) * r, nrow8 - 1), 0)),
                  pl.BlockSpec((None, N_MOD, d), lambda bi, i: (bi, 0, 0)),
                  const((1, d)),
                  const((d, d_in)),
                  pl.BlockSpec((tm, HEAD_DIM), lambda bi, i: (i, 0)),
                  pl.BlockSpec((tm, HEAD_DIM), lambda bi, i: (i, 0)),
                  const((1, HEAD_DIM)),
                  const((1, HEAD_DIM)),
                  const((3, d_hy3)),
                  const((1, d_hy3))],
        out_specs=[tok(w) for w in outs],
        out_shape=[jax.ShapeDtypeStruct((b, l, w), BF16) for w in outs],
        compiler_params=_cparams("parallel", "parallel"),
        name="inproj",
    )(x, x, x, mod, norm_g.reshape(1, d), w_in_bf, cos, sin_signed, q_norm.reshape(1, HEAD_DIM),
      k_norm.reshape(1, HEAD_DIM), conv_w, conv_b.reshape(1, d_hy3))


HALF_LANES = LANES // 2


def _filter_kernel(w1_ref, b1_ref, f1_ref, w2_ref, b2_ref, f2_ref, w3a_ref, w3b_ref, b3_ref, dl_ref, o_ref, *, l):
    tl = o_ref.shape[0]
    half = tl // 2
    lane = lax.broadcasted_iota(jnp.int32, (half, LANES), 1)
    feat = lane % HALF_LANES
    row = (pl.program_id(0) * tl + lax.broadcasted_iota(jnp.int32, (half, LANES), 0)
           + jnp.where(lane < HALF_LANES, 0, half))
    pos = jnp.where(row < l, row, 2 * l - row).astype(F32)
    t = pos * (1.0 / (l - 1))
    band_idx = jnp.where(feat <= FILTER_BANDS, feat - 1, feat - 1 - FILTER_BANDS).astype(F32)
    band = 1e-4 + band_idx * ((FILTER_BANDS - 1 - 1e-4) / (FILTER_BANDS - 1))
    phase = jnp.where(feat <= FILTER_BANDS, 0.0, 0.5 * math.pi)
    trig = jnp.cos(band * (pos * (2.0 * math.pi / l)) + phase)
    z = jnp.where(feat == 0, t, jnp.where(feat < FILTER_EMB, trig, 0.0))
    h = jnp.sin(f1_ref[...] * (_dot_hi(z, w1_ref[...]) + b1_ref[...]))
    h = jnp.sin(f2_ref[...] * (_dot_hi(h, w2_ref[...]) + b2_ref[...]))
    for w3_ref, lane0, r0 in ((w3a_ref, 0, 0), (w3b_ref, HALF_LANES, half)):
        k = (_dot_hi(h, w3_ref[...]) + b3_ref[...]) * jnp.exp(-t[:, lane0:lane0 + 1] * dl_ref[...])
        o_ref[r0:r0 + half, :] = jnp.where(row[:, lane0:lane0 + 1] == l, 0.0, k).astype(BF16)


def hyena_filter(l, w1, b1, f1, w2, b2, f2, w3, b3):
    c2 = w3.shape[1]
    c = c2 // 2
    hid = w1.shape[1]
    assert w1.shape[0] == FILTER_EMB and FILTER_EMB <= HALF_LANES and hid <= HALF_LANES

    def pad_half(a, rows):
        return jnp.pad(a.astype(F32), ((0, rows - a.shape[0]), (0, HALF_LANES - a.shape[1])))

    def block_diag2(a):
        z = jnp.zeros_like(a)
        return jnp.block([[a, z], [z, a]])

    def padv(v):
        return jnp.tile(jnp.pad(v.astype(F32), (0, HALF_LANES - hid)), 2).reshape(1, LANES)

    w1p = block_diag2(pad_half(w1, HALF_LANES))
    w2p = block_diag2(pad_half(w2, HALF_LANES))
    w3h = jnp.pad(w3.astype(F32), ((0, HALF_LANES - hid), (0, 0)))
    w3a = jnp.concatenate([w3h, jnp.zeros_like(w3h)], axis=0)
    w3b = jnp.concatenate([jnp.zeros_like(w3h), w3h], axis=0)
    deltas = jnp.abs(jnp.linspace(MIN_DECAY, MAX_DECAY, c, dtype=F32)).reshape(1, c)
    tl = _tile(l, TL_FILTER)
    nfwd = l // tl

    def const(shape):
        return pl.BlockSpec(shape, lambda i: (0, 0))

    return pl.pallas_call(
        functools.partial(_filter_kernel, l=l),
        grid=(2 * nfwd,),
        in_specs=[const((LANES, LANES)), const((1, LANES)), const((1, LANES)),
                  const((LANES, LANES)), const((1, LANES)), const((1, LANES)),
                  pl.BlockSpec((LANES, c), lambda i: (0, i // nfwd)),
                  pl.BlockSpec((LANES, c), lambda i: (0, i // nfwd)),
                  pl.BlockSpec((1, c), lambda i: (0, i // nfwd)),
                  const((1, c))],
        out_specs=pl.BlockSpec((tl, c), lambda i: (i, 0)),
        out_shape=jax.ShapeDtypeStruct((2 * l, c), BF16),
        compiler_params=_cparams("parallel"),
        name="hyena_filter",
    )(w1p, padv(b1), padv(f1), w2p, padv(b2), padv(f2), w3a, w3b, b3.astype(F32).reshape(1, c2), deltas)


def _fft_split(n):
    n2 = 128
    n1 = n // n2
    assert n1 * n2 == n and n1 % 16 == 0, n
    return n1, n2


def _angles(rows, cols, period):
    prod = (jnp.arange(rows, dtype=jnp.int32)[:, None] * jnp.arange(cols, dtype=jnp.int32)[None, :]) % period
    return prod.astype(F32) * (2.0 * math.pi / period)


def _dft_tables(n1, n2):
    n = n1 * n2
    nk = n1 // 2 + 8
    ang1 = _angles(nk, n1, n1)
    c1, s1 = jnp.cos(ang1), jnp.sin(ang1)
    fa = jnp.concatenate([c1, -s1], axis=0).astype(BF16)
    k1 = jnp.arange(nk)
    wgt = jnp.where((k1 == 0) | (k1 == n1 // 2), 1.0, jnp.where(k1 < n1 // 2, 2.0, 0.0)) * (1.0 / n)
    fc = jnp.concatenate([c1.T * wgt[None, :], -s1.T * wgt[None, :]], axis=1).astype(BF16)
    ang2 = _angles(n2, n2, n2)
    c2, s2 = jnp.cos(ang2), jnp.sin(ang2)
    g_fwd = jnp.block([[c2, s2], [-s2, c2]]).astype(BF16)
    g_inv = jnp.block([[c2, -s2], [s2, c2]]).astype(BF16)
    angt = _angles(nk, n2, n)
    twr = jnp.broadcast_to(jnp.cos(angt)[:, :, None], (nk, n2, LANES))
    twi = jnp.broadcast_to(jnp.sin(angt)[:, :, None], (nk, n2, LANES))
    return nk, fa, fc, g_fwd, g_inv, twr, twi


def _fft_a_kernel(f_ref, u_ref, o_ref):
    o_ref[...] = _dot(f_ref[...], u_ref[...]).astype(BF16)


def fft_a(fa, u):
    b, k, w = u.shape
    m = fa.shape[0]
    tn = _tile(w, TN_DFT)
    return pl.pallas_call(
        _fft_a_kernel,
        grid=(b, w // tn),
        in_specs=[pl.BlockSpec((m, k), lambda bi, j: (0, 0)),
                  pl.BlockSpec((None, k, tn), lambda bi, j: (bi, 0, j))],
        out_specs=pl.BlockSpec((None, m, tn), lambda bi, j: (bi, 0, j)),
        out_shape=jax.ShapeDtypeStruct((b, m, w), BF16),
        compiler_params=_cparams("parallel", "parallel"),
        name="fft_a",
    )(fa, u)


def _lane_tile(t, c):
    return jnp.concatenate([t] * (c // LANES), axis=1) if c > LANES else t


def _fft_mf_kernel(a_ref, twr_ref, twi_ref, g_ref, o_ref, *, tk1):
    for j in range(tk1):
        ar = a_ref[0, j].astype(F32)
        ai = a_ref[1, j].astype(F32)
        c = ar.shape[1]
        twr = _lane_tile(twr_ref[j], c)
        twi = _lane_tile(twi_ref[j], c)
        br = ar * twr + ai * twi
        bi = ai * twr - ar * twi
        x = _dot(g_ref[...], jnp.concatenate([br, bi], axis=0).astype(BF16))
        n2 = ar.shape[0]
        o_ref[0, j] = x[:n2].astype(BF16)
        o_ref[1, j] = x[n2:].astype(BF16)


def fft_mf(a, twr, twi, g_fwd):
    _, n1, n2, c = a.shape
    tk1 = TK1_DFT
    kern = functools.partial(_fft_mf_kernel, tk1=tk1)
    return pl.pallas_call(
        kern,
        grid=(n1 // tk1,),
        in_specs=[pl.BlockSpec((2, tk1, n2, c), lambda i: (0, i, 0, 0)),
                  pl.BlockSpec((tk1, n2, LANES), lambda i: (i, 0, 0)),
                  pl.BlockSpec((tk1, n2, LANES), lambda i: (i, 0, 0)),
                  pl.BlockSpec((2 * n2, 2 * n2), lambda i: (0, 0))],
        out_specs=pl.BlockSpec((2, tk1, n2, c), lambda i: (0, i, 0, 0)),
        out_shape=jax.ShapeDtypeStruct((2, n1, n2, c), BF16),
        compiler_params=_cparams("parallel"),
        name="fft_mf",
    )(a, twr, twi, g_fwd)


def _fft_m_kernel(a_ref, kf_ref, twr_ref, twi_ref, gf_ref, gi_ref, o_ref, *, tk1):
    for j in range(tk1):
        ar = a_ref[0, j].astype(F32)
        ai = a_ref[1, j].astype(F32)
        n2, c = ar.shape
        twr = _lane_tile(twr_ref[j], c)
        twi = _lane_tile(twi_ref[j], c)
        br = ar * twr + ai * twi
        bi = ai * twr - ar * twi
        x = _dot(gf_ref[...], jnp.concatenate([br, bi], axis=0).astype(BF16))
        xr, xi = x[:n2], x[n2:]
        kr = kf_ref[0, j].astype(F32)
        ki = kf_ref[1, j].astype(F32)
        zr = xr * kr - xi * ki
        zi = xr * ki + xi * kr
        y = _dot(gi_ref[...], jnp.concatenate([zr, zi], axis=0).astype(BF16))
        yr, yi = y[:n2], y[n2:]
        o_ref[0, j] = (yr * twr - yi * twi).astype(BF16)
        o_ref[1, j] = (yi * twr + yr * twi).astype(BF16)


def fft_m(a, kf, twr, twi, g_fwd, g_inv):
    b, _, n1, n2, c = a.shape
    tk1 = TK1_DFT
    kern = functools.partial(_fft_m_kernel, tk1=tk1)
    return pl.pallas_call(
        kern,
        grid=(n1 // tk1, b),
        in_specs=[pl.BlockSpec((None, 2, tk1, n2, c), lambda i, bi: (bi, 0, i, 0, 0)),
                  pl.BlockSpec((2, tk1, n2, c), lambda i, bi: (0, i, 0, 0)),
                  pl.BlockSpec((tk1, n2, LANES), lambda i, bi: (i, 0, 0)),
                  pl.BlockSpec((tk1, n2, LANES), lambda i, bi: (i, 0, 0)),
                  pl.BlockSpec((2 * n2, 2 * n2), lambda i, bi: (0, 0)),
                  pl.BlockSpec((2 * n2, 2 * n2), lambda i, bi: (0, 0))],
        out_specs=pl.BlockSpec((None, 2, tk1, n2, c), lambda i, bi: (bi, 0, i, 0, 0)),
        out_shape=jax.ShapeDtypeStruct(a.shape, BF16),
        compiler_params=_cparams("parallel", "parallel"),
        name="fft_m",
    )(a, kf, twr, twi, g_fwd, g_inv)


def _fft_c_kernel(f_ref, a_ref, uu_ref, bias_ref, o_ref):
    y = _dot(f_ref[...], a_ref[...])
    o_ref[...] = (y + uu_ref[...].astype(F32) * bias_ref[...]).astype(BF16)


def fft_c(fc_half, a2, uu, bias_tiled):
    b, k, w = a2.shape
    m = fc_half.shape[0]
    tn = bias_tiled.shape[1]
    return pl.pallas_call(
        _fft_c_kernel,
        grid=(b, w // tn),
        in_specs=[pl.BlockSpec((m, k), lambda bi, j: (0, 0)),
                  pl.BlockSpec((None, k, tn), lambda bi, j: (bi, 0, j)),
                  pl.BlockSpec((None, m, tn), lambda bi, j: (bi, 0, j)),
                  pl.BlockSpec((1, tn), lambda bi, j: (0, 0))],
        out_specs=pl.BlockSpec((None, m, tn), lambda bi, j: (bi, 0, j)),
        out_shape=jax.ShapeDtypeStruct((b, m, w), BF16),
        compiler_params=_cparams("parallel", "parallel"),
        name="fft_c",
    )(fc_half, a2, uu, bias_tiled)


def hyena_long_conv(uu, k_circ, filt_bias):
    b, l, c = uu.shape
    n = 2 * l
    n1, n2 = _fft_split(n)
    nk, fa, fc, g_fwd, g_inv, twr, twi = _dft_tables(n1, n2)
    w = n2 * c
    ka = fft_a(fa, k_circ.reshape(1, n1, w))
    kf = fft_mf(ka.reshape(2, nk, n2, c), twr, twi, g_fwd)
    uu_v = uu.reshape(b, n1 // 2, w)
    a = fft_a(fa[:, :n1 // 2], uu_v)
    a2 = fft_m(a.reshape(b, 2, nk, n2, c), kf, twr, twi, g_fwd, g_inv)
    tn = _tile(w, TN_DFT)
    bias_tiled = jnp.tile(filt_bias.astype(F32), tn // c).reshape(1, tn)
    yc = fft_c(fc[:n1 // 2], a2.reshape(b, 2 * nk, w), uu_v, bias_tiled)
    return yc.reshape(b, l, c)


def _attn_kernel(q_ref, k_ref, v_ref, o_ref, qs_ref, m_ref, acc_ref, *, tq, tk, nsub):
    ki = pl.program_id(3)

    @pl.when(ki == 0)
    def _():
        qs_ref[0:tq, :] = q_ref[:, 0:HEAD_DIM]
        qs_ref[tq:2 * tq, :] = q_ref[:, HEAD_DIM:2 * HEAD_DIM]
        m_ref[...] = jnp.full(m_ref.shape, -jnp.inf, F32)
        acc_ref[...] = jnp.zeros(acc_ref.shape, F32)

    qs = qs_ref[...]
    for j in range(nsub):
        k = k_ref[j * tk:(j + 1) * tk, :]
        v = v_ref[j * tk:(j + 1) * tk, :]
        v_aug = jnp.concatenate([v, jnp.ones_like(v)], axis=1)
        s = lax.dot_general(qs, k, (((1,), (1,)), ((), ())), preferred_element_type=F32)
        m_prev = m_ref[...]
        m_new = jnp.maximum(m_prev, jnp.max(s, axis=-1, keepdims=True))
        alpha = jnp.exp2(m_prev - m_new)
        p = jnp.exp2((s - pltpu.repeat(m_new, tk // LANES, 1)).astype(BF16))
        acc_ref[...] = jnp.concatenate([alpha, alpha], axis=1) * acc_ref[...] + _dot(p, v_aug)
        m_ref[...] = m_new

    @pl.when(ki == pl.num_programs(3) - 1)
    def _():
        o = acc_ref[:, 0:HEAD_DIM] / acc_ref[:, HEAD_DIM:2 * HEAD_DIM]
        o_ref[:, 0:HEAD_DIM] = o[0:tq].astype(BF16)
        o_ref[:, HEAD_DIM:2 * HEAD_DIM] = o[tq:2 * tq].astype(BF16)


def attention(q, k, v):
    b, l, _ = q.shape
    tq = _tile(l, TQ_ATTN)
    tk = _tile(l, TK_ATTN)
    nsub = _tile(l // tk, NSUB_ATTN)
    g = (N_Q_HEADS // N_KV_HEADS) * HEAD_DIM
    kern = functools.partial(_attn_kernel, tq=tq, tk=tk, nsub=nsub)
    tkb = tk * nsub
    return pl.pallas_call(
        kern,
        grid=(b, N_KV_HEADS, l // tq, l // tkb),
        in_specs=[pl.BlockSpec((None, tq, g), lambda bi, h, i, j: (bi, i, h)),
                  pl.BlockSpec((None, tkb, HEAD_DIM), lambda bi, h, i, j: (bi, j, h)),
                  pl.BlockSpec((None, tkb, HEAD_DIM), lambda bi, h, i, j: (bi, j, h))],
        out_specs=pl.BlockSpec((None, tq, g), lambda bi, h, i, j: (bi, i, h)),
        out_shape=jax.ShapeDtypeStruct(q.shape, BF16),
        scratch_shapes=[pltpu.VMEM((2 * tq, HEAD_DIM), BF16),
                        pltpu.VMEM((2 * tq, LANES), F32),
                        pltpu.VMEM((2 * tq, 2 * HEAD_DIM), F32)],
        compiler_params=_cparams("parallel", "parallel", "parallel", "arbitrary"),
        name="attention",
    )(q, k, v)


def _mix_kernel(x_ref, x0_ref, yc_ref, ya_ref, gh_ref, ga_ref, mod_ref, woh_ref, woa_ref, wo_ref, g_ref, rw_ref, rb_ref,
                xn_ref, n2_ref, gates_ref, *, n_exp, n_chunks):
    rows = x_ref.shape[0] // n_chunks
    for ci in range(n_chunks):
        _mix_rows(slice(ci * rows, (ci + 1) * rows), x_ref, x0_ref, yc_ref, ya_ref, gh_ref, ga_ref, mod_ref, woh_ref,
                  woa_ref, wo_ref, g_ref, rw_ref, rb_ref, xn_ref, n2_ref, gates_ref, n_exp)


def _mix_rows(r, x_ref, x0_ref, yc_ref, ya_ref, gh_ref, ga_ref, mod_ref, woh_ref, woa_ref, wo_ref, g_ref, rw_ref, rb_ref,
              xn_ref, n2_ref, gates_ref, n_exp):
    yh = (x0_ref[r, :].astype(F32) * yc_ref[r, :].astype(F32)).astype(BF16)
    th = _dot(yh, woh_ref[...])
    ta = _dot(ya_ref[r, :], woa_ref[...])
    mixed = (jax.nn.sigmoid(gh_ref[r, :].astype(F32)) * th + jax.nn.sigmoid(ga_ref[r, :].astype(F32)) * ta)
    mix = _dot(mixed.astype(BF16), wo_ref[...])
    x = x_ref[r, :] + mod_ref[2:3, :] * mix
    xn_ref[r, :] = x

    y = x * lax.rsqrt(jnp.mean(x * x, axis=-1, keepdims=True) + EPS) * g_ref[...]
    n2 = y * (1.0 + mod_ref[4:5, :]) + mod_ref[3:4, :]
    n2_hi = n2.astype(BF16)
    n2_ref[r, :] = n2_hi

    n2_lo = (n2 - n2_hi.astype(F32)).astype(BF16)
    r_hi = _dot(n2_hi, rw_ref[...])
    r_lo = _dot(n2_lo, rw_ref[...])
    lane = lax.broadcasted_iota(jnp.int32, r_hi.shape, 1)
    logits = r_hi + (pltpu.roll(r_hi, LANES - n_exp, 1) + r_lo) + rb_ref[...]
    logits = jnp.where(lane < n_exp, logits, NEG_BIG)

    work = logits
    vals, hots = [], []
    for _ in range(TOP_K):
        m = jnp.max(work, axis=-1, keepdims=True)
        idx = jnp.min(jnp.where(work == m, lane, LANES), axis=-1, keepdims=True)
        hot = lane == idx
        vals.append(m)
        hots.append(hot)
        work = jnp.where(hot, -jnp.inf, work)
    exps = [jnp.exp(v - vals[0]) for v in vals]
    den = exps[0] + exps[1] + exps[2] + exps[3]
    gates = jnp.zeros(logits.shape, F32)
    for hot, e in zip(hots, exps):
        gates = jnp.where(hot, e / den, gates)
    gates_ref[r, :] = gates


def pack_router(router_w, router_b):
    d, ne = router_w.shape
    assert 2 * ne <= LANES
    w_hi, w_lo = _split_bf16(router_w)
    w_p = jnp.concatenate([w_hi, w_lo, jnp.zeros((d, LANES - 2 * ne), BF16)], axis=1)
    return w_p, jnp.pad(router_b, (0, LANES - ne)).reshape(1, LANES), ne


def mix_router(x, x0, yc, ya, gh, ga, mod, w_out_h, w_out_a, w_o, norm_g, router):
    router_w_p, router_b_p, n_exp = router
    b, l, d = x.shape
    tm = _tile(l, TM_MIX)
    n_chunks = CHUNKS_MIX if tm % (CHUNKS_MIX * 2 * SUBLANES) == 0 else 1
    c = yc.shape[2]
    da = ya.shape[2]

    def tok(w):
        return pl.BlockSpec((None, tm, w), lambda bi, i: (bi, i, 0))

    def const(shape):
        return pl.BlockSpec(shape, lambda bi, i: (0,) * len(shape))

    return pl.pallas_call(
        functools.partial(_mix_kernel, n_exp=n_exp, n_chunks=n_chunks),
        grid=(b, l // tm),
        in_specs=[tok(d), tok(c), tok(c), tok(da), tok(d), tok(d),
                  pl.BlockSpec((None, N_MOD, d), lambda bi, i: (bi, 0, 0)),
                  const((c, d)), const((da, d)), const((d, d)), const((1, d)),
                  const((d, LANES)), const((1, LANES))],
        out_specs=[tok(d), tok(d), tok(LANES)],
        out_shape=[jax.ShapeDtypeStruct((b, l, d), F32),
                   jax.ShapeDtypeStruct((b, l, d), BF16),
                   jax.ShapeDtypeStruct((b, l, LANES), F32)],
        compiler_params=_cparams("parallel", "parallel"),
        name="mix_router",
    )(x, x0, yc, ya, gh, ga, mod, w_out_h, w_out_a, w_o, norm_g.reshape(1, d), router_w_p, router_b_p)


def _deinterleave_kernel(w_ref, p_ref, o_ref):
    o_ref[...] = _dot(w_ref[...].astype(BF16), p_ref[...]).astype(BF16)


def deinterleave_up(w_up):
    ne, d, f2 = w_up.shape
    src = lax.broadcasted_iota(jnp.int32, (f2, f2), 0)
    dst = lax.broadcasted_iota(jnp.int32, (f2, f2), 1)
    perm = (src == jnp.where(dst < f2 // 2, 2 * dst, 2 * (dst - f2 // 2) + 1)).astype(BF16)
    return pl.pallas_call(
        _deinterleave_kernel,
        grid=(ne,),
        in_specs=[pl.BlockSpec((None, d, f2), lambda e: (e, 0, 0)),
                  pl.BlockSpec((f2, f2), lambda e: (0, 0))],
        out_specs=pl.BlockSpec((None, d, f2), lambda e: (e, 0, 0)),
        out_shape=jax.ShapeDtypeStruct((ne, d, f2), BF16),
        compiler_params=_cparams("parallel"),
        name="deinterleave_up",
    )(w_up, perm)


def _split_bf16(a):
    hi = a.astype(BF16)
    return hi, (a - hi.astype(F32)).astype(BF16)


def _moe_kernel(n2_ref, gates_ref, xn_ref, mod_ref, wu_ref, bu_ref, wd_ref, bd_ref, o_ref, acc_ref, *, e_step):
    eb = pl.program_id(2)
    gates = gates_ref[...]

    @pl.when(eb == 0)
    def _():
        b_hi, b_lo = _split_bf16(bd_ref[...])
        g_bf = gates.astype(BF16)
        acc_ref[...] = _dot(g_bf, b_hi) + _dot(g_bf, b_lo)

    n2 = n2_ref[...]
    f = wd_ref.shape[1]
    acts = []
    for j in range(e_step):
        h = _dot(n2, wu_ref[j]) + bu_ref[j]
        xg = jnp.minimum(h[:, 0:f], SWIGLU_LIMIT)
        xl = jnp.clip(h[:, f:2 * f], -SWIGLU_LIMIT, SWIGLU_LIMIT)
        act = xg * jax.nn.sigmoid(SWIGLU_ALPHA * xg) * (xl + 1.0)
        ge = pltpu.roll(gates, (LANES - (eb * e_step + j)) & (LANES - 1), 1)[:, 0:1]
        acts.append((act * ge).astype(BF16))
    acc_ref[...] += _dot(jnp.concatenate(acts, axis=1), wd_ref[...].reshape(e_step * f, wd_ref.shape[2]))

    @pl.when(eb == pl.num_programs(2) - 1)
    def _():
        o_ref[...] = xn_ref[...] + mod_ref[5:6, :] * acc_ref[...]


def moe(n2, gates, xn, mod, wu, bu, wd, bd_p):
    b, l, d = xn.shape
    ne, f, _ = wd.shape
    tm = _tile(l, TM_MOE)
    e_step = _tile(ne, E_STEP_MOE)

    def tok(w):
        return pl.BlockSpec((None, tm, w), lambda bi, i, e: (bi, i, 0))

    return pl.pallas_call(
        functools.partial(_moe_kernel, e_step=e_step),
        grid=(b, l // tm, ne // e_step),
        in_specs=[tok(d), tok(LANES), tok(d),
                  pl.BlockSpec((None, N_MOD, d), lambda bi, i, e: (bi, 0, 0)),
                  pl.BlockSpec((e_step, d, 2 * f), lambda bi, i, e: (e, 0, 0)),
                  pl.BlockSpec((e_step, 1, 2 * f), lambda bi, i, e: (e, 0, 0)),
                  pl.BlockSpec((e_step, f, d), lambda bi, i, e: (e, 0, 0)),
                  pl.BlockSpec((LANES, d), lambda bi, i, e: (0, 0))],
        out_specs=tok(d),
        out_shape=jax.ShapeDtypeStruct((b, l, d), F32),
        scratch_shapes=[pltpu.VMEM((tm, d), F32)],
        compiler_params=_cparams("parallel", "parallel", "arbitrary"),
        name="moe",
    )(n2, gates, xn, mod, wu, bu, wd, bd_p)


def _rope_tables(l):
    rows = l // GRID_W
    row = jnp.repeat(jnp.arange(rows, dtype=F32), GRID_W)
    col = jnp.tile(jnp.arange(GRID_W, dtype=F32), rows)
    n_freq = HEAD_DIM // 4
    freqs = ROPE_THETA ** (-jnp.arange(n_freq, dtype=F32) / n_freq)
    ang_r = row[:, None] * freqs
    ang_c = col[:, None] * freqs
    cos = jnp.concatenate([jnp.cos(ang_r), jnp.cos(ang_r), jnp.cos(ang_c), jnp.cos(ang_c)], axis=-1)
    sin = jnp.concatenate([-jnp.sin(ang_r), jnp.sin(ang_r), -jnp.sin(ang_c), jnp.sin(ang_c)], axis=-1)
    return cos, sin


def _encoder_layer(x, mod, p):
    b, l, d = x.shape
    c = p['filt_bias'].shape[0]
    cos, sin_signed = _rope_tables(l)
    x0, uu, q, k, v, gh, ga = inproj(x, mod, p['norm_mix'], p['w_in'], cos, sin_signed, p['q_norm'], p['k_norm'],
                                     p['conv_w'], p['conv_b'], 3 * c, N_Q_HEADS * HEAD_DIM, N_KV_HEADS * HEAD_DIM)
    k_circ = hyena_filter(l, p['filt_w1'], p['filt_b1'], p['filt_freq1'], p['filt_w2'], p['filt_b2'],
                          p['filt_freq2'], p['filt_w3'], p['filt_b3'])
    yc = hyena_long_conv(uu, k_circ, p['filt_bias'])
    ya = attention(q, k, v)
    xn, n2, gates = mix_router(x, x0, yc, ya, gh, ga, mod, p['w_out_h'], p['w_out_a'], p['w_o'], p['norm_ffn'],
                               p['router'])
    return moe(n2, gates, xn, mod, p['wu'], p['bu'], p['wd'], p['bd'])


def kernel(x_prompt, x_sample, c_prompt, c_sample, w_ada, b_ada, norm_mix, w_in, conv_w, conv_b, filt_w1, filt_b1, filt_freq1, filt_w2, filt_b2, filt_freq2, filt_w3, filt_b3, filt_bias, q_norm, k_norm, w_out_h, w_out_a, w_o, norm_ffn, router_w, router_b, w_up, b_up, w_down, b_down):
    depth = w_ada.shape[0]
    d = x_prompt.shape[-1]
    bp = c_prompt.shape[0]
    bs = c_sample.shape[0]
    ne = router_w.shape[-1]
    y_prompt, y_sample = x_prompt, x_sample
    for i in range(depth):
        rows = -(-(bp + bs) // 8) * 8
        c_all = jnp.pad(jnp.concatenate([c_prompt, c_sample], axis=0), ((0, rows - bp - bs), (0, 0)))
        mod = adaln(c_all, w_ada[i], b_ada[i]).reshape(rows, N_MOD, d)
        p = {
            'norm_mix': norm_mix[i], 'w_in': w_in[i].astype(BF16),
            'conv_w': conv_w[i], 'conv_b': conv_b[i],
            'filt_w1': filt_w1[i], 'filt_b1': filt_b1[i], 'filt_freq1': filt_freq1[i],
            'filt_w2': filt_w2[i], 'filt_b2': filt_b2[i], 'filt_freq2': filt_freq2[i],
            'filt_w3': filt_w3[i], 'filt_b3': filt_b3[i], 'filt_bias': filt_bias[i],
            'q_norm': q_norm[i], 'k_norm': k_norm[i],
            'w_out_h': w_out_h[i].astype(BF16), 'w_out_a': w_out_a[i].astype(BF16), 'w_o': w_o[i].astype(BF16),
            'norm_ffn': norm_ffn[i],
            'router': pack_router(router_w[i], router_b[i]),
            'wu': deinterleave_up(w_up[i]),
            'bu': jnp.concatenate([b_up[i][:, None, 0::2], b_up[i][:, None, 1::2]], axis=-1),
            'wd': w_down[i].astype(BF16),
            'bd': jnp.pad(b_down[i], ((0, LANES - ne), (0, 0))),
        }
        y_prompt = _encoder_layer(y_prompt, mod[:bp], p)
        y_sample = _encoder_layer(y_sample, mod[bp:bp + bs], p)
    return (y_prompt, y_sample)
```

```python
import functools
import math

import jax
import jax.numpy as jnp
from jax import lax
from jax.experimental import pallas as pl
from jax.experimental.pallas import tpu as pltpu

F32 = jnp.float32
BF16 = jnp.bfloat16

EPS = 1e-6
N_MOD = 6
GRID_W = 64
HEAD_DIM = 128
N_Q_HEADS = 4
N_KV_HEADS = 2
ROPE_THETA = 10000.0
TOP_K = 4
SWIGLU_ALPHA = 1.702
SWIGLU_LIMIT = 7.0
FILTER_EMB = 33
FILTER_BANDS = (FILTER_EMB - 1) // 2
DECAY_TARGET = 1e-2
MIN_DECAY = math.log(DECAY_TARGET) / 1.5
MAX_DECAY = math.log(DECAY_TARGET) / 0.3

LANES = 128
SUBLANES = 8
NEG_BIG = -1e30
VMEM_LIMIT = 60 * 1024 * 1024

TN_ADALN = 1536
TM_INPROJ = 512
TL_FILTER = 1024
TN_DFT = 4096
TK1_DFT = 8
TQ_ATTN = 512
TK_ATTN = 512
NSUB_ATTN = 16
TM_MIX = 1024
CHUNKS_MIX = 2
TM_MOE = 1024
E_STEP_MOE = 8


def _cparams(*sem):
    return pltpu.CompilerParams(dimension_semantics=sem, vmem_limit_bytes=VMEM_LIMIT)


def _dot(a, b):
    return jnp.dot(a, b, preferred_element_type=F32)


def _dot_hi(a, b):
    return jnp.dot(a, b, preferred_element_type=F32, precision=lax.Precision.HIGHEST)


def _tile(n, want):
    t = min(n, want)
    assert n % t == 0, (n, want)
    return t


def _adaln_kernel(c_ref, w_ref, b_ref, o_ref):
    c = c_ref[...]
    o_ref[...] = _dot_hi(c * jax.nn.sigmoid(c), w_ref[...]) + b_ref[...]


def adaln(c, w_ada, b_ada):
    r, d = c.shape
    n = w_ada.shape[1]
    tn = _tile(n, TN_ADALN)
    return pl.pallas_call(
        _adaln_kernel,
        grid=(n // tn,),
        in_specs=[pl.BlockSpec((r, d), lambda j: (0, 0)),
                  pl.BlockSpec((d, tn), lambda j: (0, j)),
                  pl.BlockSpec((1, tn), lambda j: (0, j))],
        out_specs=pl.BlockSpec((r, tn), lambda j: (0, j)),
        out_shape=jax.ShapeDtypeStruct((r, n), F32),
        compiler_params=_cparams("arbitrary"),
        name="adaln",
    )(c, w_ada, b_ada.reshape(1, n))


def _rope(xn, cos, sin_signed):
    q = HEAD_DIM // 4
    lane = lax.broadcasted_iota(jnp.int32, xn.shape, 1)
    first_half = (lane % (2 * q)) < q
    rot = jnp.where(first_half, pltpu.roll(xn, HEAD_DIM - q, 1), pltpu.roll(xn, q, 1))
    return xn * cos + rot * sin_signed


def _inproj_kernel(x_ref, xp_ref, xn_ref, mod_ref, g_ref, w_ref, cos_ref, sin_ref, qg_ref, kg_ref, cw_ref, cb_ref,
                   x0_ref, uu_ref, q_ref, k_ref, v_ref, gh_ref, ga_ref, *, d_hy3, d_attn, d_kv, d_model, q_scale):
    def norm_mod(x):
        y = x * lax.rsqrt(jnp.mean(x * x, axis=-1, keepdims=True) + EPS) * g_ref[...]
        return (y * (1.0 + mod_ref[1:2, :]) + mod_ref[0:1, :]).astype(BF16)

    n = norm_mod(x_ref[...])

    s0 = d_hy3
    s1 = s0 + d_attn
    s2 = s1 + d_kv
    s3 = s2 + d_kv
    s4 = s3 + d_model
    cos = cos_ref[...]
    sin = sin_ref[...]

    def norm_rope(z, gain):
        zn = z * lax.rsqrt(jnp.mean(z * z, axis=-1, keepdims=True) + EPS) * gain
        return _rope(zn, cos, sin)

    i = pl.program_id(1)
    tm = n.shape[0]
    n_edge = norm_mod(jnp.concatenate([xp_ref[...], xn_ref[...]], axis=0))
    u_ext = _dot(jnp.concatenate([n, n_edge], axis=0), w_ref[:, 0:s0])
    u = u_ext[0:tm]
    prev_row = u_ext[tm + 7:tm + 8, :] * jnp.where(i > 0, 1.0, 0.0)
    next_row = u_ext[tm + 8:tm + 9, :] * jnp.where(i < pl.num_programs(1) - 1, 1.0, 0.0)
    row = lax.broadcasted_iota(jnp.int32, u.shape, 0)
    up = jnp.where(row == 0, prev_row, pltpu.roll(u, 1, 0))
    un = jnp.where(row == tm - 1, next_row, pltpu.roll(u, tm - 1, 0))
    yc = up * cw_ref[0:1, :] + u * cw_ref[1:2, :] + un * cw_ref[2:3, :] + cb_ref[...]
    c = d_hy3 // 3
    x0_ref[...] = yc[:, 0:c].astype(BF16)
    uu_ref[...] = (yc[:, c:2 * c] * yc[:, 2 * c:3 * c]).astype(BF16)

    qf = _dot(n, w_ref[:, s0:s1])
    kf = _dot(n, w_ref[:, s1:s2])
    for h in range(d_attn // HEAD_DIM):
        sl = slice(h * HEAD_DIM, (h + 1) * HEAD_DIM)
        q_ref[:, sl] = (norm_rope(qf[:, sl], qg_ref[...]) * q_scale).astype(BF16)
    for h in range(d_kv // HEAD_DIM):
        sl = slice(h * HEAD_DIM, (h + 1) * HEAD_DIM)
        k_ref[:, sl] = norm_rope(kf[:, sl], kg_ref[...]).astype(BF16)

    v_ref[...] = _dot(n, w_ref[:, s2:s3]).astype(BF16)
    gh_ref[...] = _dot(n, w_ref[:, s3:s4]).astype(BF16)
    ga_ref[...] = _dot(n, w_ref[:, s4:s4 + d_model]).astype(BF16)


def inproj(x, mod, norm_g, w_in_bf, cos, sin_signed, q_norm, k_norm, conv_w, conv_b, d_hy3, d_attn, d_kv):
    b, l, d = x.shape
    tm = _tile(l, TM_INPROJ)
    r = tm // SUBLANES
    nrow8 = l // SUBLANES
    d_in = w_in_bf.shape[1]
    q_scale = math.log2(math.e) / math.sqrt(HEAD_DIM)
    kern = functools.partial(_inproj_kernel, d_hy3=d_hy3, d_attn=d_attn, d_kv=d_kv, d_model=d, q_scale=q_scale)

    def tok(w):
        return pl.BlockSpec((None, tm, w), lambda bi, i: (bi, i, 0))

    def const(shape):
        return pl.BlockSpec(shape, lambda bi, i: (0,) * len(shape))

    outs = [d_hy3 // 3, d_hy3 // 3, d_attn, d_kv, d_kv, d, d]
    return pl.pallas_call(
        kern,
        grid=(b, l // tm),
        in_specs=[tok(d),
                  pl.BlockSpec((None, SUBLANES, d), lambda bi, i: (bi, jnp.maximum(i * r - 1, 0), 0)),
                  pl.BlockSpec((None, SUBLANES, d), lambda bi, i: (bi, jnp.minimum((i + 1) * r, nrow8 - 1), 0)),
                  pl.BlockSpec((None, N_MOD, d), lambda bi, i: (bi, 0, 0)),
                  const((1, d)),
                  const((d, d_in)),
                  pl.BlockSpec((tm, HEAD_DIM), lambda bi, i: (i, 0)),
                  pl.BlockSpec((tm, HEAD_DIM), lambda bi, i: (i, 0)),
                  const((1, HEAD_DIM)),
                  const((1, HEAD_DIM)),
                  const((3, d_hy3)),
                  const((1, d_hy3))],
        out_specs=[tok(w) for w in outs],
        out_shape=[jax.ShapeDtypeStruct((b, l, w), BF16) for w in outs],
        compiler_params=_cparams("parallel", "parallel"),
        name="inproj",
    )(x, x, x, mod, norm_g.reshape(1, d), w_in_bf, cos, sin_signed, q_norm.reshape(1, HEAD_DIM),
      k_norm.reshape(1, HEAD_DIM), conv_w, conv_b.reshape(1, d_hy3))


HALF_LANES = LANES // 2


def _filter_kernel(w1_ref, b1_ref, f1_ref, w2_ref, b2_ref, f2_ref, w3a_ref, w3b_ref, b3_ref, dl_ref, o_ref, *, l):
    tl = o_ref.shape[0]
    half = tl // 2
    lane = lax.broadcasted_iota(jnp.int32, (half, LANES), 1)
    feat = lane % HALF_LANES
    row = (pl.program_id(0) * tl + lax.broadcasted_iota(jnp.int32, (half, LANES), 0)
           + jnp.where(lane < HALF_LANES, 0, half))
    pos = jnp.where(row < l, row, 2 * l - row).astype(F32)
    t = pos * (1.0 / (l - 1))
    band_idx = jnp.where(feat <= FILTER_BANDS, feat - 1, feat - 1 - FILTER_BANDS).astype(F32)
    band = 1e-4 + band_idx * ((FILTER_BANDS - 1 - 1e-4) / (FILTER_BANDS - 1))
    phase = jnp.where(feat <= FILTER_BANDS, 0.0, 0.5 * math.pi)
    trig = jnp.cos(band * (pos * (2.0 * math.pi / l)) + phase)
    z = jnp.where(feat == 0, t, jnp.where(feat < FILTER_EMB, trig, 0.0))
    h = jnp.sin(f1_ref[...] * (_dot_hi(z, w1_ref[...]) + b1_ref[...]))
    h = jnp.sin(f2_ref[...] * (_dot_hi(h, w2_ref[...]) + b2_ref[...]))
    for w3_ref, lane0, r0 in ((w3a_ref, 0, 0), (w3b_ref, HALF_LANES, half)):
        k = (_dot_hi(h, w3_ref[...]) + b3_ref[...]) * jnp.exp(-t[:, lane0:lane0 + 1] * dl_ref[...])
        o_ref[r0:r0 + half, :] = jnp.where(row[:, lane0:lane0 + 1] == l, 0.0, k).astype(BF16)


def hyena_filter(l, w1, b1, f1, w2, b2, f2, w3, b3):
    c2 = w3.shape[1]
    c = c2 // 2
    hid = w1.shape[1]
    assert w1.shape[0] == FILTER_EMB and FILTER_EMB <= HALF_LANES and hid <= HALF_LANES

    def pad_half(a, rows):
        return jnp.pad(a.astype(F32), ((0, rows - a.shape[0]), (0, HALF_LANES - a.shape[1])))

    def block_diag2(a):
        z = jnp.zeros_like(a)
        return jnp.block([[a, z], [z, a]])

    def padv(v):
        return jnp.tile(jnp.pad(v.astype(F32), (0, HALF_LANES - hid)), 2).reshape(1, LANES)

    w1p = block_diag2(pad_half(w1, HALF_LANES))
    w2p = block_diag2(pad_half(w2, HALF_LANES))
    w3h = jnp.pad(w3.astype(F32), ((0, HALF_LANES - hid), (0, 0)))
    w3a = jnp.concatenate([w3h, jnp.zeros_like(w3h)], axis=0)
    w3b = jnp.concatenate([jnp.zeros_like(w3h), w3h], axis=0)
    deltas = jnp.abs(jnp.linspace(MIN_DECAY, MAX_DECAY, c, dtype=F32)).reshape(1, c)
    tl = _tile(l, TL_FILTER)
    nfwd = l // tl

    def const(shape):
        return pl.BlockSpec(shape, lambda i: (0, 0))

    return pl.pallas_call(
        functools.partial(_filter_kernel, l=l),
        grid=(2 * nfwd,),
        in_specs=[const((LANES, LANES)), const((1, LANES)), const((1, LANES)),
                  const((LANES, LANES)), const((1, LANES)), const((1, LANES)),
                  pl.BlockSpec((LANES, c), lambda i: (0, i // nfwd)),
                  pl.BlockSpec((LANES, c), lambda i: (0, i // nfwd)),
                  pl.BlockSpec((1, c), lambda i: (0, i // nfwd)),
                  const((1, c))],
        out_specs=pl.BlockSpec((tl, c), lambda i: (i, 0)),
        out_shape=jax.ShapeDtypeStruct((2 * l, c), BF16),
        compiler_params=_cparams("parallel"),
        name="hyena_filter",
    )(w1p, padv(b1), padv(f1), w2p, padv(b2), padv(f2), w3a, w3b, b3.astype(F32).reshape(1, c2), deltas)


def _fft_split(n):
    n2 = 128
    n1 = n // n2
    assert n1 * n2 == n and n1 % 16 == 0, n
    return n1, n2


def _angles(rows, cols, period):
    prod = (jnp.arange(rows, dtype=jnp.int32)[:, None] * jnp.arange(cols, dtype=jnp.int32)[None, :]) % period
    return prod.astype(F32) * (2.0 * math.pi / period)


def _dft_tables(n1, n2):
    n = n1 * n2
    nk = n1 // 2 + 8
    ang1 = _angles(nk, n1, n1)
    c1, s1 = jnp.cos(ang1), jnp.sin(ang1)
    fa = jnp.concatenate([c1, -s1], axis=0).astype(BF16)
    k1 = jnp.arange(nk)
    wgt = jnp.where((k1 == 0) | (k1 == n1 // 2), 1.0, jnp.where(k1 < n1 // 2, 2.0, 0.0)) * (1.0 / n)
    fc = jnp.concatenate([c1.T * wgt[None, :], -s1.T * wgt[None, :]], axis=1).astype(BF16)
    ang2 = _angles(n2, n2, n2)
    c2, s2 = jnp.cos(ang2), jnp.sin(ang2)
    g_fwd = jnp.block([[c2, s2], [-s2, c2]]).astype(BF16)
    g_inv = jnp.block([[c2, -s2], [s2, c2]]).astype(BF16)
    angt = _angles(nk, n2, n)
    twr = jnp.broadcast_to(jnp.cos(angt)[:, :, None], (nk, n2, LANES))
    twi = jnp.broadcast_to(jnp.sin(angt)[:, :, None], (nk, n2, LANES))
    return nk, fa, fc, g_fwd, g_inv, twr, twi


def _fft_a_kernel(f_ref, u_ref, o_ref):
    o_ref[...] = _dot(f_ref[...], u_ref[...]).astype(BF16)


def fft_a(fa, u):
    b, k, w = u.shape
    m = fa.shape[0]
    tn = _tile(w, TN_DFT)
    return pl.pallas_call(
        _fft_a_kernel,
        grid=(b, w // tn),
        in_specs=[pl.BlockSpec((m, k), lambda bi, j: (0, 0)),
                  pl.BlockSpec((None, k, tn), lambda bi, j: (bi, 0, j))],
        out_specs=pl.BlockSpec((None, m, tn), lambda bi, j: (bi, 0, j)),
        out_shape=jax.ShapeDtypeStruct((b, m, w), BF16),
        compiler_params=_cparams("parallel", "parallel"),
        name="fft_a",
    )(fa, u)


def _lane_tile(t, c):
    return jnp.concatenate([t] * (c // LANES), axis=1) if c > LANES else t


def _fft_mf_kernel(a_ref, twr_ref, twi_ref, g_ref, o_ref, *, tk1):
    for j in range(tk1):
        ar = a_ref[0, j].astype(F32)
        ai = a_ref[1, j].astype(F32)
        c = ar.shape[1]
        twr = _lane_tile(twr_ref[j], c)
        twi = _lane_tile(twi_ref[j], c)
        br = ar * twr + ai * twi
        bi = ai * twr - ar * twi
        x = _dot(g_ref[...], jnp.concatenate([br, bi], axis=0).astype(BF16))
        n2 = ar.shape[0]
        o_ref[0, j] = x[:n2].astype(BF16)
        o_ref[1, j] = x[n2:].astype(BF16)


def fft_mf(a, twr, twi, g_fwd):
    _, n1, n2, c = a.shape
    tk1 = TK1_DFT
    kern = functools.partial(_fft_mf_kernel, tk1=tk1)
    return pl.pallas_call(
        kern,
        grid=(n1 // tk1,),
        in_specs=[pl.BlockSpec((2, tk1, n2, c), lambda i: (0, i, 0, 0)),
                  pl.BlockSpec((tk1, n2, LANES), lambda i: (i, 0, 0)),
                  pl.BlockSpec((tk1, n2, LANES), lambda i: (i, 0, 0)),
                  pl.BlockSpec((2 * n2, 2 * n2), lambda i: (0, 0))],
        out_specs=pl.BlockSpec((2, tk1, n2, c), lambda i: (0, i, 0, 0)),
        out_shape=jax.ShapeDtypeStruct((2, n1, n2, c), BF16),
        compiler_params=_cparams("parallel"),
        name="fft_mf",
    )(a, twr, twi, g_fwd)


def _fft_m_kernel(a_ref, kf_ref, twr_ref, twi_ref, gf_ref, gi_ref, o_ref, *, tk1):
    for j in range(tk1):
        ar = a_ref[0, j].astype(F32)
        ai = a_ref[1, j].astype(F32)
        n2, c = ar.shape
        twr = _lane_tile(twr_ref[j], c)
        twi = _lane_tile(twi_ref[j], c)
        br = ar * twr + ai * twi
        bi = ai * twr - ar * twi
        x = _dot(gf_ref[...], jnp.concatenate([br, bi], axis=0).astype(BF16))
        xr, xi = x[:n2], x[n2:]
        kr = kf_ref[0, j].astype(F32)
        ki = kf_ref[1, j].astype(F32)
        zr = xr * kr - xi * ki
        zi = xr * ki + xi * kr
        y = _dot(gi_ref[...], jnp.concatenate([zr, zi], axis=0).astype(BF16))
        yr, yi = y[:n2], y[n2:]
        o_ref[0, j] = (yr * twr - yi * twi).astype(BF16)
        o_ref[1, j] = (yi * twr + yr * twi).astype(BF16)


def fft_m(a, kf, twr, twi, g_fwd, g_inv):
    b, _, n1, n2, c = a.shape
    tk1 = TK1_DFT
    kern = functools.partial(_fft_m_kernel, tk1=tk1)
    return pl.pallas_call(
        kern,
        grid=(n1 // tk1, b),
        in_specs=[pl.BlockSpec((None, 2, tk1, n2, c), lambda i, bi: (bi, 0, i, 0, 0)),
                  pl.BlockSpec((2, tk1, n2, c), lambda i, bi: (0, i, 0, 0)),
                  pl.BlockSpec((tk1, n2, LANES), lambda i, bi: (i, 0, 0)),
                  pl.BlockSpec((tk1, n2, LANES), lambda i, bi: (i, 0, 0)),
                  pl.BlockSpec((2 * n2, 2 * n2), lambda i, bi: (0, 0)),
                  pl.BlockSpec((2 * n2, 2 * n2), lambda i, bi: (0, 0))],
        out_specs=pl.BlockSpec((None, 2, tk1, n2, c), lambda i, bi: (bi, 0, i, 0, 0)),
        out_shape=jax.ShapeDtypeStruct(a.shape, BF16),
        compiler_params=_cparams("parallel", "parallel"),
        name="fft_m",
    )(a, kf, twr, twi, g_fwd, g_inv)


def _fft_c_kernel(f_ref, a_ref, uu_ref, bias_ref, o_ref):
    y = _dot(f_ref[...], a_ref[...])
    o_ref[...] = (y + uu_ref[...].astype(F32) * bias_ref[...]).astype(BF16)


def fft_c(fc_half, a2, uu, bias_tiled):
    b, k, w = a2.shape
    m = fc_half.shape[0]
    tn = bias_tiled.shape[1]
    return pl.pallas_call(
        _fft_c_kernel,
        grid=(b, w // tn),
        in_specs=[pl.BlockSpec((m, k), lambda bi, j: (0, 0)),
                  pl.BlockSpec((None, k, tn), lambda bi, j: (bi, 0, j)),
                  pl.BlockSpec((None, m, tn), lambda bi, j: (bi, 0, j)),
                  pl.BlockSpec((1, tn), lambda bi, j: (0, 0))],
        out_specs=pl.BlockSpec((None, m, tn), lambda bi, j: (bi, 0, j)),
        out_shape=jax.ShapeDtypeStruct((b, m, w), BF16),
        compiler_params=_cparams("parallel", "parallel"),
        name="fft_c",
    )(fc_half, a2, uu, bias_tiled)


def hyena_long_conv(uu, k_circ, filt_bias):
    b, l, c = uu.shape
    n = 2 * l
    n1, n2 = _fft_split(n)
    nk, fa, fc, g_fwd, g_inv, twr, twi = _dft_tables(n1, n2)
    w = n2 * c
    ka = fft_a(fa, k_circ.reshape(1, n1, w))
    kf = fft_mf(ka.reshape(2, nk, n2, c), twr, twi, g_fwd)
    uu_v = uu.reshape(b, n1 // 2, w)
    a = fft_a(fa[:, :n1 // 2], uu_v)
    a2 = fft_m(a.reshape(b, 2, nk, n2, c), kf, twr, twi, g_fwd, g_inv)
    tn = _tile(w, TN_DFT)
    bias_tiled = jnp.tile(filt_bias.astype(F32), tn // c).reshape(1, tn)
    yc = fft_c(fc[:n1 // 2], a2.reshape(b, 2 * nk, w), uu_v, bias_tiled)
    return yc.reshape(b, l, c)


def _attn_kernel(q_ref, k_ref, v_ref, o_ref, qs_ref, m_ref, acc_ref, *, tq, tk, nsub):
    ki = pl.program_id(3)

    @pl.when(ki == 0)
    def _():
        qs_ref[0:tq, :] = q_ref[:, 0:HEAD_DIM]
        qs_ref[tq:2 * tq, :] = q_ref[:, HEAD_DIM:2 * HEAD_DIM]
        m_ref[...] = jnp.full(m_ref.shape, -jnp.inf, F32)
        acc_ref[...] = jnp.zeros(acc_ref.shape, F32)

    qs = qs_ref[...]
    for j in range(nsub):
        k = k_ref[j * tk:(j + 1) * tk, :]
        v = v_ref[j * tk:(j + 1) * tk, :]
        v_aug = jnp.concatenate([v, jnp.ones_like(v)], axis=1)
        s = lax.dot_general(qs, k, (((1,), (1,)), ((), ())), preferred_element_type=F32)
        m_prev = m_ref[...]
        m_new = jnp.maximum(m_prev, jnp.max(s, axis=-1, keepdims=True))
        alpha = jnp.exp2(m_prev - m_new)
        p = jnp.exp2((s - pltpu.repeat(m_new, tk // LANES, 1)).astype(BF16))
        acc_ref[...] = jnp.concatenate([alpha, alpha], axis=1) * acc_ref[...] + _dot(p, v_aug)
        m_ref[...] = m_new

    @pl.when(ki == pl.num_programs(3) - 1)
    def _():
        o = acc_ref[:, 0:HEAD_DIM] / acc_ref[:, HEAD_DIM:2 * HEAD_DIM]
        o_ref[:, 0:HEAD_DIM] = o[0:tq].astype(BF16)
        o_ref[:, HEAD_DIM:2 * HEAD_DIM] = o[tq:2 * tq].astype(BF16)


def attention(q, k, v):
    b, l, _ = q.shape
    tq = _tile(l, TQ_ATTN)
    tk = _tile(l, TK_ATTN)
    nsub = _tile(l // tk, NSUB_ATTN)
    g = (N_Q_HEADS // N_KV_HEADS) * HEAD_DIM
    kern = functools.partial(_attn_kernel, tq=tq, tk=tk, nsub=nsub)
    tkb = tk * nsub
    return pl.pallas_call(
        kern,
        grid=(b, N_KV_HEADS, l // tq, l // tkb),
        in_specs=[pl.BlockSpec((None, tq, g), lambda bi, h, i, j: (bi, i, h)),
                  pl.BlockSpec((None, tkb, HEAD_DIM), lambda bi, h, i, j: (bi, j, h)),
                  pl.BlockSpec((None, tkb, HEAD_DIM), lambda bi, h, i, j: (bi, j, h))],
        out_specs=pl.BlockSpec((None, tq, g), lambda bi, h, i, j: (bi, i, h)),
        out_shape=jax.ShapeDtypeStruct(q.shape, BF16),
        scratch_shapes=[pltpu.VMEM((2 * tq, HEAD_DIM), BF16),
                        pltpu.VMEM((2 * tq, LANES), F32),
                        pltpu.VMEM((2 * tq, 2 * HEAD_DIM), F32)],
        compiler_params=_cparams("parallel", "parallel", "parallel", "arbitrary"),
        name="attention",
    )(q, k, v)


def _mix_kernel(x_ref, x0_ref, yc_ref, ya_ref, gh_ref, ga_ref, mod_ref, woh_ref, woa_ref, wo_ref, g_ref, rw_ref, rb_ref,
                xn_ref, n2_ref, gates_ref, *, n_exp, n_chunks):
    rows = x_ref.shape[0] // n_chunks
    for ci in range(n_chunks):
        _mix_rows(slice(ci * rows, (ci + 1) * rows), x_ref, x0_ref, yc_ref, ya_ref, gh_ref, ga_ref, mod_ref, woh_ref,
                  woa_ref, wo_ref, g_ref, rw_ref, rb_ref, xn_ref, n2_ref, gates_ref, n_exp)


def _mix_rows(r, x_ref, x0_ref, yc_ref, ya_ref, gh_ref, ga_ref, mod_ref, woh_ref, woa_ref, wo_ref, g_ref, rw_ref, rb_ref,
              xn_ref, n2_ref, gates_ref, n_exp):
    yh = (x0_ref[r, :].astype(F32) * yc_ref[r, :].astype(F32)).astype(BF16)
    th = _dot(yh, woh_ref[...])
    ta = _dot(ya_ref[r, :], woa_ref[...])
    mixed = (jax.nn.sigmoid(gh_ref[r, :].astype(F32)) * th + jax.nn.sigmoid(ga_ref[r, :].astype(F32)) * ta)
    mix = _dot(mixed.astype(BF16), wo_ref[...])
    x = x_ref[r, :] + mod_ref[2:3, :] * mix
    xn_ref[r, :] = x

    y = x * lax.rsqrt(jnp.mean(x * x, axis=-1, keepdims=True) + EPS) * g_ref[...]
    n2 = y * (1.0 + mod_ref[4:5, :]) + mod_ref[3:4, :]
    n2_hi = n2.astype(BF16)
    n2_ref[r, :] = n2_hi

    n2_lo = (n2 - n2_hi.astype(F32)).astype(BF16)
    r_hi = _dot(n2_hi, rw_ref[...])
    r_lo = _dot(n2_lo, rw_ref[...])
    lane = lax.broadcasted_iota(jnp.int32, r_hi.shape, 1)
    logits = r_hi + (pltpu.roll(r_hi, LANES - n_exp, 1) + r_lo) + rb_ref[...]
    logits = jnp.where(lane < n_exp, logits, NEG_BIG)

    work = logits
    vals, hots = [], []
    for _ in range(TOP_K):
        m = jnp.max(work, axis=-1, keepdims=True)
        idx = jnp.min(jnp.where(work == m, lane, LANES), axis=-1, keepdims=True)
        hot = lane == idx
        vals.append(m)
        hots.append(hot)
        work = jnp.where(hot, -jnp.inf, work)
    exps = [jnp.exp(v - vals[0]) for v in vals]
    den = exps[0] + exps[1] + exps[2] + exps[3]
    gates = jnp.zeros(logits.shape, F32)
    for hot, e in zip(hots, exps):
        gates = jnp.where(hot, e / den, gates)
    gates_ref[r, :] = gates


def pack_router(router_w, router_b):
    d, ne = router_w.shape
    assert 2 * ne <= LANES
    w_hi, w_lo = _split_bf16(router_w)
    w_p = jnp.concatenate([w_hi, w_lo, jnp.zeros((d, LANES - 2 * ne), BF16)], axis=1)
    return w_p, jnp.pad(router_b, (0, LANES - ne)).reshape(1, LANES), ne


def mix_router(x, x0, yc, ya, gh, ga, mod, w_out_h, w_out_a, w_o, norm_g, router):
    router_w_p, router_b_p, n_exp = router
    b, l, d = x.shape
    tm = _tile(l, TM_MIX)
    n_chunks = CHUNKS_MIX if tm % (CHUNKS_MIX * 2 * SUBLANES) == 0 else 1
    c = yc.shape[2]
    da = ya.shape[2]

    def tok(w):
        return pl.BlockSpec((None, tm, w), lambda bi, i: (bi, i, 0))

    def const(shape):
        return pl.BlockSpec(shape, lambda bi, i: (0,) * len(shape))

    return pl.pallas_call(
        functools.partial(_mix_kernel, n_exp=n_exp, n_chunks=n_chunks),
        grid=(b, l // tm),
        in_specs=[tok(d), tok(c), tok(c), tok(da), tok(d), tok(d),
                  pl.BlockSpec((None, N_MOD, d), lambda bi, i: (bi, 0, 0)),
                  const((c, d)), const((da, d)), const((d, d)), const((1, d)),
                  const((d, LANES)), const((1, LANES))],
        out_specs=[tok(d), tok(d), tok(LANES)],
        out_shape=[jax.ShapeDtypeStruct((b, l, d), F32),
                   jax.ShapeDtypeStruct((b, l, d), BF16),
                   jax.ShapeDtypeStruct((b, l, LANES), F32)],
        compiler_params=_cparams("parallel", "parallel"),
        name="mix_router",
    )(x, x0, yc, ya, gh, ga, mod, w_out_h, w_out_a, w_o, norm_g.reshape(1, d), router_w_p, router_b_p)


def _deinterleave_kernel(w_ref, p_ref, o_ref):
    o_ref[...] = _dot(w_ref[...].astype(BF16), p_ref[...]).astype(BF16)


def deinterleave_up(w_up):
    ne, d, f2 = w_up.shape
    src = lax.broadcasted_iota(jnp.int32, (f2, f2), 0)
    dst = lax.broadcasted_iota(jnp.int32, (f2, f2), 1)
    perm = (src == jnp.where(dst < f2 // 2, 2 * dst, 2 * (dst - f2 // 2) + 1)).astype(BF16)
    return pl.pallas_call(
        _deinterleave_kernel,
        grid=(ne,),
        in_specs=[pl.BlockSpec((None, d, f2), lambda e: (e, 0, 0)),
                  pl.BlockSpec((f2, f2), lambda e: (0, 0))],
        out_specs=pl.BlockSpec((None, d, f2), lambda e: (e, 0, 0)),
        out_shape=jax.ShapeDtypeStruct((ne, d, f2), BF16),
        compiler_params=_cparams("parallel"),
        name="deinterleave_up",
    )(w_up, perm)


def _split_bf16(a):
    hi = a.astype(BF16)
    return hi, (a - hi.astype(F32)).astype(BF16)


def _moe_kernel(n2_ref, gates_ref, xn_ref, mod_ref, wu_ref, bu_ref, wd_ref, bd_ref, o_ref, acc_ref, *, e_step):
    eb = pl.program_id(2)
    gates = gates_ref[...]

    @pl.when(eb == 0)
    def _():
        b_hi, b_lo = _split_bf16(bd_ref[...])
        g_bf = gates.astype(BF16)
        acc_ref[...] = _dot(g_bf, b_hi) + _dot(g_bf, b_lo)

    n2 = n2_ref[...]
    f = wd_ref.shape[1]
    acts = []
    for j in range(e_step):
        h = _dot(n2, wu_ref[j]) + bu_ref[j]
        xg = jnp.minimum(h[:, 0:f], SWIGLU_LIMIT)
        xl = jnp.clip(h[:, f:2 * f], -SWIGLU_LIMIT, SWIGLU_LIMIT)
        act = xg * jax.nn.sigmoid(SWIGLU_ALPHA * xg) * (xl + 1.0)
        ge = pltpu.roll(gates, (LANES - (eb * e_step + j)) & (LANES - 1), 1)[:, 0:1]
        acts.append((act * ge).astype(BF16))
    acc_ref[...] += _dot(jnp.concatenate(acts, axis=1), wd_ref[...].reshape(e_step * f, wd_ref.shape[2]))

    @pl.when(eb == pl.num_programs(2) - 1)
    def _():
        o_ref[...] = xn_ref[...] + mod_ref[5:6, :] * acc_ref[...]


def moe(n2, gates, xn, mod, wu, bu, wd, bd_p):
    b, l, d = xn.shape
    ne, f, _ = wd.shape
    tm = _tile(l, TM_MOE)
    e_step = _tile(ne, E_STEP_MOE)

    def tok(w):
        return pl.BlockSpec((None, tm, w), lambda bi, i, e: (bi, i, 0))

    return pl.pallas_call(
        functools.partial(_moe_kernel, e_step=e_step),
        grid=(b, l // tm, ne // e_step),
        in_specs=[tok(d), tok(LANES), tok(d),
                  pl.BlockSpec((None, N_MOD, d), lambda bi, i, e: (bi, 0, 0)),
                  pl.BlockSpec((e_step, d, 2 * f), lambda bi, i, e: (e, 0, 0)),
                  pl.BlockSpec((e_step, 1, 2 * f), lambda bi, i, e: (e, 0, 0)),
                  pl.BlockSpec((e_step, f, d), lambda bi, i, e: (e, 0, 0)),
                  pl.BlockSpec((LANES, d), lambda bi, i, e: (0, 0))],
        out_specs=tok(d),
        out_shape=jax.ShapeDtypeStruct((b, l, d), F32),
        scratch_shapes=[pltpu.VMEM((tm, d), F32)],
        compiler_params=_cparams("parallel", "parallel", "arbitrary"),
        name="moe",
    )(n2, gates, xn, mod, wu, bu, wd, bd_p)


def _rope_tables(l):
    rows = l // GRID_W
    row = jnp.repeat(jnp.arange(rows, dtype=F32), GRID_W)
    col = jnp.tile(jnp.arange(GRID_W, dtype=F32), rows)
    n_freq = HEAD_DIM // 4
    freqs = ROPE_THETA ** (-jnp.arange(n_freq, dtype=F32) / n_freq)
    ang_r = row[:, None] * freqs
    ang_c = col[:, None] * freqs
    cos = jnp.concatenate([jnp.cos(ang_r), jnp.cos(ang_r), jnp.cos(ang_c), jnp.cos(ang_c)], axis=-1)
    sin = jnp.concatenate([-jnp.sin(ang_r), jnp.sin(ang_r), -jnp.sin(ang_c), jnp.sin(ang_c)], axis=-1)
    return cos, sin


def _encoder_layer(x, mod, p):
    b, l, d = x.shape
    c = p['filt_bias'].shape[0]
    cos, sin_signed = _rope_tables(l)
    x0, uu, q, k, v, gh, ga = inproj(x, mod, p['norm_mix'], p['w_in'], cos, sin_signed, p['q_norm'], p['k_norm'],
                                     p['conv_w'], p['conv_b'], 3 * c, N_Q_HEADS * HEAD_DIM, N_KV_HEADS * HEAD_DIM)
    k_circ = hyena_filter(l, p['filt_w1'], p['filt_b1'], p['filt_freq1'], p['filt_w2'], p['filt_b2'],
                          p['filt_freq2'], p['filt_w3'], p['filt_b3'])
    yc = hyena_long_conv(uu, k_circ, p['filt_bias'])
    ya = attention(q, k, v)
    xn, n2, gates = mix_router(x, x0, yc, ya, gh, ga, mod, p['w_out_h'], p['w_out_a'], p['w_o'], p['norm_ffn'],
                               p['router'])
    return moe(n2, gates, xn, mod, p['wu'], p['bu'], p['wd'], p['bd'])


def kernel(x_prompt, x_sample, c_prompt, c_sample, w_ada, b_ada, norm_mix, w_in, conv_w, conv_b, filt_w1, filt_b1, filt_freq1, filt_w2, filt_b2, filt_freq2, filt_w3, filt_b3, filt_bias, q_norm, k_norm, w_out_h, w_out_a, w_o, norm_ffn, router_w, router_b, w_up, b_up, w_down, b_down):
    depth = w_ada.shape[0]
    d = x_prompt.shape[-1]
    bp = c_prompt.shape[0]
    bs = c_sample.shape[0]
    ne = router_w.shape[-1]
    y_prompt, y_sample = x_prompt, x_sample
    for i in range(depth):
        rows = -(-(bp + bs) // 8) * 8
        c_all = jnp.pad(jnp.concatenate([c_prompt, c_sample], axis=0), ((0, rows - bp - bs), (0, 0)))
        mod = adaln(c_all, w_ada[i], b_ada[i]).reshape(rows, N_MOD, d)
        p = {
            'norm_mix': norm_mix[i], 'w_in': w_in[i].astype(BF16),
            'conv_w': conv_w[i], 'conv_b': conv_b[i],
            'filt_w1': filt_w1[i], 'filt_b1': filt_b1[i], 'filt_freq1': filt_freq1[i],
            'filt_w2': filt_w2[i], 'filt_b2': filt_b2[i], 'filt_freq2': filt_freq2[i],
            'filt_w3': filt_w3[i], 'filt_b3': filt_b3[i], 'filt_bias': filt_bias[i],
            'q_norm': q_norm[i], 'k_norm': k_norm[i],
            'w_out_h': w_out_h[i].astype(BF16), 'w_out_a': w_out_a[i].astype(BF16), 'w_o': w_o[i].astype(BF16),
            'norm_ffn': norm_ffn[i],
            'router': pack_router(router_w[i], router_b[i]),
            'wu': deinterleave_up(w_up[i]),
            'bu': jnp.concatenate([b_up[i][:, None, 0::2], b_up[i][:, None, 1::2]], axis=-1),
            'wd': w_down[i].astype(BF16),
            'bd': jnp.pad(b_down[i], ((0, LANES - ne), (0, 0))),
        }
        y_prompt = _encoder_layer(y_prompt, mod[:bp], p)
        y_sample = _encoder_layer(y_sample, mod[bp:bp + bs], p)
    return (y_prompt, y_sample)
```

```python
import functools
import math

import jax
import jax.numpy as jnp
from jax import lax
from jax.experimental import pallas as pl
from jax.experimental.pallas import tpu as pltpu

F32 = jnp.float32
BF16 = jnp.bfloat16

EPS = 1e-6
N_MOD = 6
GRID_W = 64
HEAD_DIM = 128
N_Q_HEADS = 4
N_KV_HEADS = 2
ROPE_THETA = 10000.0
TOP_K = 4
SWIGLU_ALPHA = 1.702
SWIGLU_LIMIT = 7.0
FILTER_EMB = 33
FILTER_BANDS = (FILTER_EMB - 1) // 2
DECAY_TARGET = 1e-2
MIN_DECAY = math.log(DECAY_TARGET) / 1.5
MAX_DECAY = math.log(DECAY_TARGET) / 0.3

LANES = 128
SUBLANES = 8
NEG_BIG = -1e30
VMEM_LIMIT = 60 * 1024 * 1024

TN_ADALN = 1536
TM_INPROJ = 1024
CHUNKS_INPROJ = 2
TL_FILTER = 1024
TN_DFT = 8192
TK1_DFT = 8
TQ_ATTN = 512
TK_ATTN = 512
NSUB_ATTN = 32
TM_MIX = 1024
CHUNKS_MIX = 2
TM_MOE = 1024
E_STEP_MOE = 8


def _cparams(*sem):
    return pltpu.CompilerParams(dimension_semantics=sem, vmem_limit_bytes=VMEM_LIMIT)


def _dot(a, b):
    return jnp.dot(a, b, preferred_element_type=F32)


def _dot_hi(a, b):
    return jnp.dot(a, b, preferred_element_type=F32, precision=lax.Precision.HIGHEST)


def _tile(n, want):
    t = min(n, want)
    assert n % t == 0, (n, want)
    return t


def _adaln_kernel(c_ref, w_ref, b_ref, o_ref):
    c = c_ref[...]
    o_ref[...] = _dot_hi(c * jax.nn.sigmoid(c), w_ref[...]) + b_ref[...]


def adaln(c, w_ada, b_ada):
    r, d = c.shape
    n = w_ada.shape[1]
    tn = _tile(n, TN_ADALN)
    return pl.pallas_call(
        _adaln_kernel,
        grid=(n // tn,),
        in_specs=[pl.BlockSpec((r, d), lambda j: (0, 0)),
                  pl.BlockSpec((d, tn), lambda j: (0, j)),
                  pl.BlockSpec((1, tn), lambda j: (0, j))],
        out_specs=pl.BlockSpec((r, tn), lambda j: (0, j)),
        out_shape=jax.ShapeDtypeStruct((r, n), F32),
        compiler_params=_cparams("arbitrary"),
        name="adaln",
    )(c, w_ada, b_ada.reshape(1, n))


def _rope(xn, cos, sin_signed):
    q = HEAD_DIM // 4
    lane = lax.broadcasted_iota(jnp.int32, xn.shape, 1)
    first_half = (lane % (2 * q)) < q
    rot = jnp.where(first_half, pltpu.roll(xn, HEAD_DIM - q, 1), pltpu.roll(xn, q, 1))
    return xn * cos + rot * sin_signed


def _inproj_kernel(x_ref, xp_ref, xn_ref, mod_ref, g_ref, w_ref, cos_ref, sin_ref, qg_ref, kg_ref, cw_ref, cb_ref,
                   x0_ref, uu_ref, q_ref, k_ref, v_ref, gh_ref, ga_ref, *, n_chunks, **dims):
    i = pl.program_id(1)
    rows = x_ref.shape[0] // n_chunks
    for ci in range(n_chunks):
        lo, hi = ci * rows, (ci + 1) * rows
        prev8 = xp_ref[...] if ci == 0 else x_ref[lo - SUBLANES:lo, :]
        next8 = xn_ref[...] if ci == n_chunks - 1 else x_ref[hi:hi + SUBLANES, :]
        has_prev = jnp.where(i > 0, 1.0, 0.0) if ci == 0 else 1.0
        has_next = jnp.where(i < pl.num_programs(1) - 1, 1.0, 0.0) if ci == n_chunks - 1 else 1.0
        _inproj_rows(slice(lo, hi), prev8, next8, has_prev, has_next, x_ref, mod_ref, g_ref, w_ref, cos_ref, sin_ref,
                     qg_ref, kg_ref, cw_ref, cb_ref, x0_ref, uu_ref, q_ref, k_ref, v_ref, gh_ref, ga_ref, **dims)


def _inproj_rows(r, prev8, next8, has_prev, has_next, x_ref, mod_ref, g_ref, w_ref, cos_ref, sin_ref, qg_ref, kg_ref,
                 cw_ref, cb_ref, x0_ref, uu_ref, q_ref, k_ref, v_ref, gh_ref, ga_ref, *, d_hy3, d_attn, d_kv, d_model,
                 q_scale):
    def norm_mod(x):
        y = x * lax.rsqrt(jnp.mean(x * x, axis=-1, keepdims=True) + EPS) * g_ref[...]
        return (y * (1.0 + mod_ref[1:2, :]) + mod_ref[0:1, :]).astype(BF16)

    n = norm_mod(x_ref[r, :])

    s0 = d_hy3
    s1 = s0 + d_attn
    s2 = s1 + d_kv
    s3 = s2 + d_kv
    s4 = s3 + d_model
    cos = cos_ref[r, :]
    sin = sin_ref[r, :]

    def norm_rope(z, gain):
        zn = z * lax.rsqrt(jnp.mean(z * z, axis=-1, keepdims=True) + EPS) * gain
        return _rope(zn, cos, sin)

    tm = n.shape[0]
    n_edge = norm_mod(jnp.concatenate([prev8, next8], axis=0))
    u_ext = _dot(jnp.concatenate([n, n_edge], axis=0), w_ref[:, 0:s0])
    u = u_ext[0:tm]
    prev_row = u_ext[tm + SUBLANES - 1:tm + SUBLANES, :] * has_prev
    next_row = u_ext[tm + SUBLANES:tm + SUBLANES + 1, :] * has_next
    row = lax.broadcasted_iota(jnp.int32, u.shape, 0)
    up = jnp.where(row == 0, prev_row, pltpu.roll(u, 1, 0))
    un = jnp.where(row == tm - 1, next_row, pltpu.roll(u, tm - 1, 0))
    yc = up * cw_ref[0:1, :] + u * cw_ref[1:2, :] + un * cw_ref[2:3, :] + cb_ref[...]
    c = d_hy3 // 3
    x0_ref[r, :] = yc[:, 0:c].astype(BF16)
    uu_ref[r, :] = (yc[:, c:2 * c] * yc[:, 2 * c:3 * c]).astype(BF16)

    qf = _dot(n, w_ref[:, s0:s1])
    kf = _dot(n, w_ref[:, s1:s2])
    for h in range(d_attn // HEAD_DIM):
        sl = slice(h * HEAD_DIM, (h + 1) * HEAD_DIM)
        q_ref[r, sl] = (norm_rope(qf[:, sl], qg_ref[...]) * q_scale).astype(BF16)
    for h in range(d_kv // HEAD_DIM):
        sl = slice(h * HEAD_DIM, (h + 1) * HEAD_DIM)
        k_ref[r, sl] = norm_rope(kf[:, sl], kg_ref[...]).astype(BF16)

    v_ref[r, :] = _dot(n, w_ref[:, s2:s3]).astype(BF16)
    gh_ref[r, :] = _dot(n, w_ref[:, s3:s4]).astype(BF16)
    ga_ref[r, :] = _dot(n, w_ref[:, s4:s4 + d_model]).astype(BF16)


def inproj(x, mod, norm_g, w_in_bf, cos, sin_signed, q_norm, k_norm, conv_w, conv_b, d_hy3, d_attn, d_kv):
    b, l, d = x.shape
    tm = _tile(l, TM_INPROJ)
    r = tm // SUBLANES
    nrow8 = l // SUBLANES
    d_in = w_in_bf.shape[1]
    q_scale = math.log2(math.e) / math.sqrt(HEAD_DIM)
    n_chunks = CHUNKS_INPROJ if tm % (CHUNKS_INPROJ * 2 * SUBLANES) == 0 else 1
    kern = functools.partial(_inproj_kernel, n_chunks=n_chunks, d_hy3=d_hy3, d_attn=d_attn, d_kv=d_kv, d_model=d,
                             q_scale=q_scale)

    def tok(w):
        return pl.BlockSpec((None, tm, w), lambda bi, i: (bi, i, 0))

    def const(shape):
        return pl.BlockSpec(shape, lambda bi, i: (0,) * len(shape))

    outs = [d_hy3 // 3, d_hy3 // 3, d_attn, d_kv, d_kv, d, d]
    return pl.pallas_call(
        kern,
        grid=(b, l // tm),
        in_specs=[tok(d),
                  pl.BlockSpec((None, SUBLANES, d), lambda bi, i: (bi, jnp.maximum(i * r - 1, 0), 0)),
                  pl.BlockSpec((None, SUBLANES, d), lambda bi, i: (bi, jnp.minimum((i + 1) * r, nrow8 - 1), 0)),
                  pl.BlockSpec((None, N_MOD, d), lambda bi, i: (bi, 0, 0)),
                  const((1, d)),
                  const((d, d_in)),
                  pl.BlockSpec((tm, HEAD_DIM), lambda bi, i: (i, 0)),
                  pl.BlockSpec((tm, HEAD_DIM), lambda bi, i: (i, 0)),
                  const((1, HEAD_DIM)),
                  const((1, HEAD_DIM)),
                  const((3, d_hy3)),
                  const((1, d_hy3))],
        out_specs=[tok(w) for w in outs],
        out_shape=[jax.ShapeDtypeStruct((b, l, w), BF16) for w in outs],
        compiler_params=_cparams("parallel", "parallel"),
        name="inproj",
    )(x, x, x, mod, norm_g.reshape(1, d), w_in_bf, cos, sin_signed, q_norm.reshape(1, HEAD_DIM),
      k_norm.reshape(1, HEAD_DIM), conv_w, conv_b.reshape(1, d_hy3))


HALF_LANES = LANES // 2


def _filter_kernel(w1_ref, b1_ref, f1_ref, w2_ref, b2_ref, f2_ref, w3a_ref, w3b_ref, b3_ref, dl_ref, o_ref, *, l):
    tl = o_ref.shape[0]
    half = tl // 2
    lane = lax.broadcasted_iota(jnp.int32, (half, LANES), 1)
    feat = lane % HALF_LANES
    row = (pl.program_id(0) * tl + lax.broadcasted_iota(jnp.int32, (half, LANES), 0)
           + jnp.where(lane < HALF_LANES, 0, half))
    pos = jnp.where(row < l, row, 2 * l - row).astype(F32)
    t = pos * (1.0 / (l - 1))
    band_idx = jnp.where(feat <= FILTER_BANDS, feat - 1, feat - 1 - FILTER_BANDS).astype(F32)
    band = 1e-4 + band_idx * ((FILTER_BANDS - 1 - 1e-4) / (FILTER_BANDS - 1))
    phase = jnp.where(feat <= FILTER_BANDS, 0.0, 0.5 * math.pi)
    trig = jnp.cos(band * (pos * (2.0 * math.pi / l)) + phase)
    z = jnp.where(feat == 0, t, jnp.where(feat < FILTER_EMB, trig, 0.0))
    h = jnp.sin(f1_ref[...] * (_dot_hi(z, w1_ref[...]) + b1_ref[...]))
    h = jnp.sin(f2_ref[...] * (_dot_hi(h, w2_ref[...]) + b2_ref[...]))
    for w3_ref, lane0, r0 in ((w3a_ref, 0, 0), (w3b_ref, HALF_LANES, half)):
        k = (_dot_hi(h, w3_ref[...]) + b3_ref[...]) * jnp.exp(-t[:, lane0:lane0 + 1] * dl_ref[...])
        o_ref[r0:r0 + half, :] = jnp.where(row[:, lane0:lane0 + 1] == l, 0.0, k).astype(BF16)


def hyena_filter(l, w1, b1, f1, w2, b2, f2, w3, b3):
    c2 = w3.shape[1]
    c = c2 // 2
    hid = w1.shape[1]
    assert w1.shape[0] == FILTER_EMB and FILTER_EMB <= HALF_LANES and hid <= HALF_LANES

    def pad_half(a, rows):
        return jnp.pad(a.astype(F32), ((0, rows - a.shape[0]), (0, HALF_LANES - a.shape[1])))

    def block_diag2(a):
        z = jnp.zeros_like(a)
        return jnp.block([[a, z], [z, a]])

    def padv(v):
        return jnp.tile(jnp.pad(v.astype(F32), (0, HALF_LANES - hid)), 2).reshape(1, LANES)

    w1p = block_diag2(pad_half(w1, HALF_LANES))
    w2p = block_diag2(pad_half(w2, HALF_LANES))
    w3h = jnp.pad(w3.astype(F32), ((0, HALF_LANES - hid), (0, 0)))
    w3a = jnp.concatenate([w3h, jnp.zeros_like(w3h)], axis=0)
    w3b = jnp.concatenate([jnp.zeros_like(w3h), w3h], axis=0)
    deltas = jnp.abs(jnp.linspace(MIN_DECAY, MAX_DECAY, c, dtype=F32)).reshape(1, c)
    tl = _tile(l, TL_FILTER)
    nfwd = l // tl

    def const(shape):
        return pl.BlockSpec(shape, lambda i: (0, 0))

    return pl.pallas_call(
        functools.partial(_filter_kernel, l=l),
        grid=(2 * nfwd,),
        in_specs=[const((LANES, LANES)), const((1, LANES)), const((1, LANES)),
                  const((LANES, LANES)), const((1, LANES)), const((1, LANES)),
                  pl.BlockSpec((LANES, c), lambda i: (0, i // nfwd)),
                  pl.BlockSpec((LANES, c), lambda i: (0, i // nfwd)),
                  pl.BlockSpec((1, c), lambda i: (0, i // nfwd)),
                  const((1, c))],
        out_specs=pl.BlockSpec((tl, c), lambda i: (i, 0)),
        out_shape=jax.ShapeDtypeStruct((2 * l, c), BF16),
        compiler_params=_cparams("parallel"),
        name="hyena_filter",
    )(w1p, padv(b1), padv(f1), w2p, padv(b2), padv(f2), w3a, w3b, b3.astype(F32).reshape(1, c2), deltas)


def _fft_split(n):
    n2 = 128
    n1 = n // n2
    assert n1 * n2 == n and n1 % 16 == 0, n
    return n1, n2


def _angles(rows, cols, period):
    prod = (jnp.arange(rows, dtype=jnp.int32)[:, None] * jnp.arange(cols, dtype=jnp.int32)[None, :]) % period
    return prod.astype(F32) * (2.0 * math.pi / period)


def _dft_tables(n1, n2):
    n = n1 * n2
    nk = n1 // 2 + 8
    ang1 = _angles(nk, n1, n1)
    c1, s1 = jnp.cos(ang1), jnp.sin(ang1)
    fa = jnp.concatenate([c1, -s1], axis=0).astype(BF16)
    k1 = jnp.arange(nk)
    wgt = jnp.where((k1 == 0) | (k1 == n1 // 2), 1.0, jnp.where(k1 < n1 // 2, 2.0, 0.0)) * (1.0 / n)
    fc = jnp.concatenate([c1.T * wgt[None, :], -s1.T * wgt[None, :]], axis=1).astype(BF16)
    ang2 = _angles(n2, n2, n2)
    c2, s2 = jnp.cos(ang2), jnp.sin(ang2)
    g_fwd = jnp.block([[c2, s2], [-s2, c2]]).astype(BF16)
    g_inv = jnp.block([[c2, -s2], [s2, c2]]).astype(BF16)
    angt = _angles(nk, n2, n)
    twr = jnp.broadcast_to(jnp.cos(angt)[:, :, None], (nk, n2, LANES))
    twi = jnp.broadcast_to(jnp.sin(angt)[:, :, None], (nk, n2, LANES))
    return nk, fa, fc, g_fwd, g_inv, twr, twi


def _fft_a_kernel(f_ref, u_ref, o_ref):
    o_ref[...] = _dot(f_ref[...], u_ref[...]).astype(BF16)


def fft_a(fa, u):
    b, k, w = u.shape
    m = fa.shape[0]
    tn = _tile(w, TN_DFT)
    return pl.pallas_call(
        _fft_a_kernel,
        grid=(b, w // tn),
        in_specs=[pl.BlockSpec((m, k), lambda bi, j: (0, 0)),
                  pl.BlockSpec((None, k, tn), lambda bi, j: (bi, 0, j))],
        out_specs=pl.BlockSpec((None, m, tn), lambda bi, j: (bi, 0, j)),
        out_shape=jax.ShapeDtypeStruct((b, m, w), BF16),
        compiler_params=_cparams("parallel", "parallel"),
        name="fft_a",
    )(fa, u)


def _lane_tile(t, c):
    return jnp.concatenate([t] * (c // LANES), axis=1) if c > LANES else t


def _fft_mf_kernel(a_ref, twr_ref, twi_ref, g_ref, o_ref, *, tk1):
    for j in range(tk1):
        ar = a_ref[0, j].astype(F32)
        ai = a_ref[1, j].astype(F32)
        c = ar.shape[1]
        twr = _lane_tile(twr_ref[j], c)
        twi = _lane_tile(twi_ref[j], c)
        br = ar * twr + ai * twi
        bi = ai * twr - ar * twi
        x = _dot(g_ref[...], jnp.concatenate([br, bi], axis=0).astype(BF16))
        n2 = ar.shape[0]
        o_ref[0, j] = x[:n2].astype(BF16)
        o_ref[1, j] = x[n2:].astype(BF16)


def fft_mf(a, twr, twi, g_fwd):
    _, n1, n2, c = a.shape
    tk1 = TK1_DFT
    kern = functools.partial(_fft_mf_kernel, tk1=tk1)
    return pl.pallas_call(
        kern,
        grid=(n1 // tk1,),
        in_specs=[pl.BlockSpec((2, tk1, n2, c), lambda i: (0, i, 0, 0)),
                  pl.BlockSpec((tk1, n2, LANES), lambda i: (i, 0, 0)),
                  pl.BlockSpec((tk1, n2, LANES), lambda i: (i, 0, 0)),
                  pl.BlockSpec((2 * n2, 2 * n2), lambda i: (0, 0))],
        out_specs=pl.BlockSpec((2, tk1, n2, c), lambda i: (0, i, 0, 0)),
        out_shape=jax.ShapeDtypeStruct((2, n1, n2, c), BF16),
        compiler_params=_cparams("parallel"),
        name="fft_mf",
    )(a, twr, twi, g_fwd)


def _fft_m_kernel(a_ref, kf_ref, twr_ref, twi_ref, gf_ref, gi_ref, o_ref, *, tk1):
    for j in range(tk1):
        ar = a_ref[0, j].astype(F32)
        ai = a_ref[1, j].astype(F32)
        n2, c = ar.shape
        twr = _lane_tile(twr_ref[j], c)
        twi = _lane_tile(twi_ref[j], c)
        br = ar * twr + ai * twi
        bi = ai * twr - ar * twi
        x = _dot(gf_ref[...], jnp.concatenate([br, bi], axis=0).astype(BF16))
        xr, xi = x[:n2], x[n2:]
        kr = kf_ref[0, j].astype(F32)
        ki = kf_ref[1, j].astype(F32)
        zr = xr * kr - xi * ki
        zi = xr * ki + xi * kr
        y = _dot(gi_ref[...], jnp.concatenate([zr, zi], axis=0).astype(BF16))
        yr, yi = y[:n2], y[n2:]
        o_ref[0, j] = (yr * twr - yi * twi).astype(BF16)
        o_ref[1, j] = (yi * twr + yr * twi).astype(BF16)


def fft_m(a, kf, twr, twi, g_fwd, g_inv):
    b, _, n1, n2, c = a.shape
    tk1 = TK1_DFT
    kern = functools.partial(_fft_m_kernel, tk1=tk1)
    return pl.pallas_call(
        kern,
        grid=(n1 // tk1, b),
        in_specs=[pl.BlockSpec((None, 2, tk1, n2, c), lambda i, bi: (bi, 0, i, 0, 0)),
                  pl.BlockSpec((2, tk1, n2, c), lambda i, bi: (0, i, 0, 0)),
                  pl.BlockSpec((tk1, n2, LANES), lambda i, bi: (i, 0, 0)),
                  pl.BlockSpec((tk1, n2, LANES), lambda i, bi: (i, 0, 0)),
                  pl.BlockSpec((2 * n2, 2 * n2), lambda i, bi: (0, 0)),
                  pl.BlockSpec((2 * n2, 2 * n2), lambda i, bi: (0, 0))],
        out_specs=pl.BlockSpec((None, 2, tk1, n2, c), lambda i, bi: (bi, 0, i, 0, 0)),
        out_shape=jax.ShapeDtypeStruct(a.shape, BF16),
        compiler_params=_cparams("parallel", "parallel"),
        name="fft_m",
    )(a, kf, twr, twi, g_fwd, g_inv)


def _fft_c_kernel(f_ref, a_ref, uu_ref, bias_ref, o_ref):
    y = _dot(f_ref[...], a_ref[...])
    o_ref[...] = (y + uu_ref[...].astype(F32) * bias_ref[...]).astype(BF16)


def fft_c(fc_half, a2, uu, bias_tiled):
    b, k, w = a2.shape
    m = fc_half.shape[0]
    tn = bias_tiled.shape[1]
    return pl.pallas_call(
        _fft_c_kernel,
        grid=(b, w // tn),
        in_specs=[pl.BlockSpec((m, k), lambda bi, j: (0, 0)),
                  pl.BlockSpec((None, k, tn), lambda bi, j: (bi, 0, j)),
                  pl.BlockSpec((None, m, tn), lambda bi, j: (bi, 0, j)),
                  pl.BlockSpec((1, tn), lambda bi, j: (0, 0))],
        out_specs=pl.BlockSpec((None, m, tn), lambda bi, j: (bi, 0, j)),
        out_shape=jax.ShapeDtypeStruct((b, m, w), BF16),
        compiler_params=_cparams("parallel", "parallel"),
        name="fft_c",
    )(fc_half, a2, uu, bias_tiled)


def hyena_long_conv(uu, k_circ, filt_bias):
    b, l, c = uu.shape
    n = 2 * l
    n1, n2 = _fft_split(n)
    nk, fa, fc, g_fwd, g_inv, twr, twi = _dft_tables(n1, n2)
    w = n2 * c
    ka = fft_a(fa, k_circ.reshape(1, n1, w))
    kf = fft_mf(ka.reshape(2, nk, n2, c), twr, twi, g_fwd)
    uu_v = uu.reshape(b, n1 // 2, w)
    a = fft_a(fa[:, :n1 // 2], uu_v)
    a2 = fft_m(a.reshape(b, 2, nk, n2, c), kf, twr, twi, g_fwd, g_inv)
    tn = _tile(w, TN_DFT)
    bias_tiled = jnp.tile(filt_bias.astype(F32), tn // c).reshape(1, tn)
    yc = fft_c(fc[:n1 // 2], a2.reshape(b, 2 * nk, w), uu_v, bias_tiled)
    return yc.reshape(b, l, c)


def _attn_kernel(q_ref, k_ref, v_ref, o_ref, qs_ref, m_ref, acc_ref, *, tq, tk, nsub):
    ki = pl.program_id(3)

    @pl.when(ki == 0)
    def _():
        qs_ref[0:tq, :] = q_ref[:, 0:HEAD_DIM]
        qs_ref[tq:2 * tq, :] = q_ref[:, HEAD_DIM:2 * HEAD_DIM]
        m_ref[...] = jnp.full(m_ref.shape, -jnp.inf, F32)
        acc_ref[...] = jnp.zeros(acc_ref.shape, F32)

    qs = qs_ref[...]
    for j in range(nsub):
        k = k_ref[j * tk:(j + 1) * tk, :]
        v = v_ref[j * tk:(j + 1) * tk, :]
        v_aug = jnp.concatenate([v, jnp.ones_like(v)], axis=1)
        s = lax.dot_general(qs, k, (((1,), (1,)), ((), ())), preferred_element_type=F32)
        m_prev = m_ref[...]
        m_new = jnp.maximum(m_prev, jnp.max(s, axis=-1, keepdims=True))
        alpha = jnp.exp2(m_prev - m_new)
        p = jnp.exp2((s - pltpu.repeat(m_new, tk // LANES, 1)).astype(BF16))
        acc_ref[...] = jnp.concatenate([alpha, alpha], axis=1) * acc_ref[...] + _dot(p, v_aug)
        m_ref[...] = m_new

    @pl.when(ki == pl.num_programs(3) - 1)
    def _():
        o = acc_ref[:, 0:HEAD_DIM] / acc_ref[:, HEAD_DIM:2 * HEAD_DIM]
        o_ref[:, 0:HEAD_DIM] = o[0:tq].astype(BF16)
        o_ref[:, HEAD_DIM:2 * HEAD_DIM] = o[tq:2 * tq].astype(BF16)


def attention(q, k, v):
    b, l, _ = q.shape
    tq = _tile(l, TQ_ATTN)
    tk = _tile(l, TK_ATTN)
    nsub = _tile(l // tk, NSUB_ATTN)
    g = (N_Q_HEADS // N_KV_HEADS) * HEAD_DIM
    kern = functools.partial(_attn_kernel, tq=tq, tk=tk, nsub=nsub)
    tkb = tk * nsub
    return pl.pallas_call(
        kern,
        grid=(b, N_KV_HEADS, l // tq, l // tkb),
        in_specs=[pl.BlockSpec((None, tq, g), lambda bi, h, i, j: (bi, i, h)),
                  pl.BlockSpec((None, tkb, HEAD_DIM), lambda bi, h, i, j: (bi, j, h)),
                  pl.BlockSpec((None, tkb, HEAD_DIM), lambda bi, h, i, j: (bi, j, h))],
        out_specs=pl.BlockSpec((None, tq, g), lambda bi, h, i, j: (bi, i, h)),
        out_shape=jax.ShapeDtypeStruct(q.shape, BF16),
        scratch_shapes=[pltpu.VMEM((2 * tq, HEAD_DIM), BF16),
                        pltpu.VMEM((2 * tq, LANES), F32),
                        pltpu.VMEM((2 * tq, 2 * HEAD_DIM), F32)],
        compiler_params=_cparams("parallel", "parallel", "parallel", "arbitrary"),
        name="attention",
    )(q, k, v)


def _mix_kernel(x_ref, x0_ref, yc_ref, ya_ref, gh_ref, ga_ref, mod_ref, woh_ref, woa_ref, wo_ref, g_ref, rw_ref, rb_ref,
                xn_ref, n2_ref, gates_ref, *, n_exp, n_chunks):
    rows = x_ref.shape[0] // n_chunks
    for ci in range(n_chunks):
        _mix_rows(slice(ci * rows, (ci + 1) * rows), x_ref, x0_ref, yc_ref, ya_ref, gh_ref, ga_ref, mod_ref, woh_ref,
                  woa_ref, wo_ref, g_ref, rw_ref, rb_ref, xn_ref, n2_ref, gates_ref, n_exp)


def _mix_rows(r, x_ref, x0_ref, yc_ref, ya_ref, gh_ref, ga_ref, mod_ref, woh_ref, woa_ref, wo_ref, g_ref, rw_ref, rb_ref,
              xn_ref, n2_ref, gates_ref, n_exp):
    yh = (x0_ref[r, :].astype(F32) * yc_ref[r, :].astype(F32)).astype(BF16)
    th = _dot(yh, woh_ref[...])
    ta = _dot(ya_ref[r, :], woa_ref[...])
    mixed = (jax.nn.sigmoid(gh_ref[r, :].astype(F32)) * th + jax.nn.sigmoid(ga_ref[r, :].astype(F32)) * ta)
    mix = _dot(mixed.astype(BF16), wo_ref[...])
    x = x_ref[r, :] + mod_ref[2:3, :] * mix
    xn_ref[r, :] = x

    y = x * lax.rsqrt(jnp.mean(x * x, axis=-1, keepdims=True) + EPS) * g_ref[...]
    n2 = y * (1.0 + mod_ref[4:5, :]) + mod_ref[3:4, :]
    n2_hi = n2.astype(BF16)
    n2_ref[r, :] = n2_hi

    n2_lo = (n2 - n2_hi.astype(F32)).astype(BF16)
    r_hi = _dot(n2_hi, rw_ref[...])
    r_lo = _dot(n2_lo, rw_ref[...])
    lane = lax.broadcasted_iota(jnp.int32, r_hi.shape, 1)
    logits = r_hi + (pltpu.roll(r_hi, LANES - n_exp, 1) + r_lo) + rb_ref[...]
    logits = jnp.where(lane < n_exp, logits, NEG_BIG)

    work = logits
    vals, hots = [], []
    for _ in range(TOP_K):
        m = jnp.max(work, axis=-1, keepdims=True)
        idx = jnp.min(jnp.where(work == m, lane, LANES), axis=-1, keepdims=True)
        hot = lane == idx
        vals.append(m)
        hots.append(hot)
        work = jnp.where(hot, -jnp.inf, work)
    exps = [jnp.exp(v - vals[0]) for v in vals]
    den = exps[0] + exps[1] + exps[2] + exps[3]
    gates = jnp.zeros(logits.shape, F32)
    for hot, e in zip(hots, exps):
        gates = jnp.where(hot, e / den, gates)
    gates_ref[r, :] = gates


def pack_router(router_w, router_b):
    d, ne = router_w.shape
    assert 2 * ne <= LANES
    w_hi, w_lo = _split_bf16(router_w)
    w_p = jnp.concatenate([w_hi, w_lo, jnp.zeros((d, LANES - 2 * ne), BF16)], axis=1)
    return w_p, jnp.pad(router_b, (0, LANES - ne)).reshape(1, LANES), ne


def mix_router(x, x0, yc, ya, gh, ga, mod, w_out_h, w_out_a, w_o, norm_g, router):
    router_w_p, router_b_p, n_exp = router
    b, l, d = x.shape
    tm = _tile(l, TM_MIX)
    n_chunks = CHUNKS_MIX if tm % (CHUNKS_MIX * 2 * SUBLANES) == 0 else 1
    c = yc.shape[2]
    da = ya.shape[2]

    def tok(w):
        return pl.BlockSpec((None, tm, w), lambda bi, i: (bi, i, 0))

    def const(shape):
        return pl.BlockSpec(shape, lambda bi, i: (0,) * len(shape))

    return pl.pallas_call(
        functools.partial(_mix_kernel, n_exp=n_exp, n_chunks=n_chunks),
        grid=(b, l // tm),
        in_specs=[tok(d), tok(c), tok(c), tok(da), tok(d), tok(d),
                  pl.BlockSpec((None, N_MOD, d), lambda bi, i: (bi, 0, 0)),
                  const((c, d)), const((da, d)), const((d, d)), const((1, d)),
                  const((d, LANES)), const((1, LANES))],
        out_specs=[tok(d), tok(d), tok(LANES)],
        out_shape=[jax.ShapeDtypeStruct((b, l, d), F32),
                   jax.ShapeDtypeStruct((b, l, d), BF16),
                   jax.ShapeDtypeStruct((b, l, LANES), F32)],
        compiler_params=_cparams("parallel", "parallel"),
        name="mix_router",
    )(x, x0, yc, ya, gh, ga, mod, w_out_h, w_out_a, w_o, norm_g.reshape(1, d), router_w_p, router_b_p)


def _deinterleave_kernel(w_ref, p_ref, o_ref):
    o_ref[...] = _dot(w_ref[...].astype(BF16), p_ref[...]).astype(BF16)


def deinterleave_up(w_up):
    ne, d, f2 = w_up.shape
    src = lax.broadcasted_iota(jnp.int32, (f2, f2), 0)
    dst = lax.broadcasted_iota(jnp.int32, (f2, f2), 1)
    perm = (src == jnp.where(dst < f2 // 2, 2 * dst, 2 * (dst - f2 // 2) + 1)).astype(BF16)
    return pl.pallas_call(
        _deinterleave_kernel,
        grid=(ne,),
        in_specs=[pl.BlockSpec((None, d, f2), lambda e: (e, 0, 0)),
                  pl.BlockSpec((f2, f2), lambda e: (0, 0))],
        out_specs=pl.BlockSpec((None, d, f2), lambda e: (e, 0, 0)),
        out_shape=jax.ShapeDtypeStruct((ne, d, f2), BF16),
        compiler_params=_cparams("parallel"),
        name="deinterleave_up",
    )(w_up, perm)


def _split_bf16(a):
    hi = a.astype(BF16)
    return hi, (a - hi.astype(F32)).astype(BF16)


def _moe_kernel(n2_ref, gates_ref, xn_ref, mod_ref, wu_ref, bu_ref, wd_ref, bd_ref, o_ref, acc_ref, *, e_step):
    eb = pl.program_id(2)
    gates = gates_ref[...]

    @pl.when(eb == 0)
    def _():
        b_hi, b_lo = _split_bf16(bd_ref[...])
        g_bf = gates.astype(BF16)
        acc_ref[...] = _dot(g_bf, b_hi) + _dot(g_bf, b_lo)

    n2 = n2_ref[...]
    f = wd_ref.shape[1]
    acts = []
    for j in range(e_step):
        h = _dot(n2, wu_ref[j]) + bu_ref[j]
        xg = jnp.minimum(h[:, 0:f], SWIGLU_LIMIT)
        xl = jnp.clip(h[:, f:2 * f], -SWIGLU_LIMIT, SWIGLU_LIMIT)
        act = xg * jax.nn.sigmoid(SWIGLU_ALPHA * xg) * (xl + 1.0)
        ge = pltpu.roll(gates, (LANES - (eb * e_step + j)) & (LANES - 1), 1)[:, 0:1]
        acts.append((act * ge).astype(BF16))
    acc_ref[...] += _dot(jnp.concatenate(acts, axis=1), wd_ref[...].reshape(e_step * f, wd_ref.shape[2]))

    @pl.when(eb == pl.num_programs(2) - 1)
    def _():
        o_ref[...] = xn_ref[...] + mod_ref[5:6, :] * acc_ref[...]


def moe(n2, gates, xn, mod, wu, bu, wd, bd_p):
    b, l, d = xn.shape
    ne, f, _ = wd.shape
    tm = _tile(l, TM_MOE)
    e_step = _tile(ne, E_STEP_MOE)

    def tok(w):
        return pl.BlockSpec((None, tm, w), lambda bi, i, e: (bi, i, 0))

    return pl.pallas_call(
        functools.partial(_moe_kernel, e_step=e_step),
        grid=(b, l // tm, ne // e_step),
        in_specs=[tok(d), tok(LANES), tok(d),
                  pl.BlockSpec((None, N_MOD, d), lambda bi, i, e: (bi, 0, 0)),
                  pl.BlockSpec((e_step, d, 2 * f), lambda bi, i, e: (e, 0, 0)),
                  pl.BlockSpec((e_step, 1, 2 * f), lambda bi, i, e: (e, 0, 0)),
                  pl.BlockSpec((e_step, f, d), lambda bi, i, e: (e, 0, 0)),
                  pl.BlockSpec((LANES, d), lambda bi, i, e: (0, 0))],
        out_specs=tok(d),
        out_shape=jax.ShapeDtypeStruct((b, l, d), F32),
        scratch_shapes=[pltpu.VMEM((tm, d), F32)],
        compiler_params=_cparams("parallel", "parallel", "arbitrary"),
        name="moe",
    )(n2, gates, xn, mod, wu, bu, wd, bd_p)


def _rope_tables(l):
    rows = l // GRID_W
    row = jnp.repeat(jnp.arange(rows, dtype=F32), GRID_W)
    col = jnp.tile(jnp.arange(GRID_W, dtype=F32), rows)
    n_freq = HEAD_DIM // 4
    freqs = ROPE_THETA ** (-jnp.arange(n_freq, dtype=F32) / n_freq)
    ang_r = row[:, None] * freqs
    ang_c = col[:, None] * freqs
    cos = jnp.concatenate([jnp.cos(ang_r), jnp.cos(ang_r), jnp.cos(ang_c), jnp.cos(ang_c)], axis=-1)
    sin = jnp.concatenate([-jnp.sin(ang_r), jnp.sin(ang_r), -jnp.sin(ang_c), jnp.sin(ang_c)], axis=-1)
    return cos, sin


def _encoder_layer(x, mod, p):
    b, l, d = x.shape
    c = p['filt_bias'].shape[0]
    cos, sin_signed = _rope_tables(l)
    x0, uu, q, k, v, gh, ga = inproj(x, mod, p['norm_mix'], p['w_in'], cos, sin_signed, p['q_norm'], p['k_norm'],
                                     p['conv_w'], p['conv_b'], 3 * c, N_Q_HEADS * HEAD_DIM, N_KV_HEADS * HEAD_DIM)
    k_circ = hyena_filter(l, p['filt_w1'], p['filt_b1'], p['filt_freq1'], p['filt_w2'], p['filt_b2'],
                          p['filt_freq2'], p['filt_w3'], p['filt_b3'])
    yc = hyena_long_conv(uu, k_circ, p['filt_bias'])
    ya = attention(q, k, v)
    xn, n2, gates = mix_router(x, x0, yc, ya, gh, ga, mod, p['w_out_h'], p['w_out_a'], p['w_o'], p['norm_ffn'],
                               p['router'])
    return moe(n2, gates, xn, mod, p['wu'], p['bu'], p['wd'], p['bd'])


def kernel(x_prompt, x_sample, c_prompt, c_sample, w_ada, b_ada, norm_mix, w_in, conv_w, conv_b, filt_w1, filt_b1, filt_freq1, filt_w2, filt_b2, filt_freq2, filt_w3, filt_b3, filt_bias, q_norm, k_norm, w_out_h, w_out_a, w_o, norm_ffn, router_w, router_b, w_up, b_up, w_down, b_down):
    depth = w_ada.shape[0]
    d = x_prompt.shape[-1]
    bp = c_prompt.shape[0]
    bs = c_sample.shape[0]
    ne = router_w.shape[-1]
    y_prompt, y_sample = x_prompt, x_sample
    for i in range(depth):
        rows = -(-(bp + bs) // 8) * 8
        c_all = jnp.pad(jnp.concatenate([c_prompt, c_sample], axis=0), ((0, rows - bp - bs), (0, 0)))
        mod = adaln(c_all, w_ada[i], b_ada[i]).reshape(rows, N_MOD, d)
        p = {
            'norm_mix': norm_mix[i], 'w_in': w_in[i].astype(BF16),
            'conv_w': conv_w[i], 'conv_b': conv_b[i],
            'filt_w1': filt_w1[i], 'filt_b1': filt_b1[i], 'filt_freq1': filt_freq1[i],
            'filt_w2': filt_w2[i], 'filt_b2': filt_b2[i], 'filt_freq2': filt_freq2[i],
            'filt_w3': filt_w3[i], 'filt_b3': filt_b3[i], 'filt_bias': filt_bias[i],
            'q_norm': q_norm[i], 'k_norm': k_norm[i],
            'w_out_h': w_out_h[i].astype(BF16), 'w_out_a': w_out_a[i].astype(BF16), 'w_o': w_o[i].astype(BF16),
            'norm_ffn': norm_ffn[i],
            'router': pack_router(router_w[i], router_b[i]),
            'wu': deinterleave_up(w_up[i]),
            'bu': jnp.concatenate([b_up[i][:, None, 0::2], b_up[i][:, None, 1::2]], axis=-1),
            'wd': w_down[i].astype(BF16),
            'bd': jnp.pad(b_down[i], ((0, LANES - ne), (0, 0))),
        }
        y_prompt = _encoder_layer(y_prompt, mod[:bp], p)
        y_sample = _encoder_layer(y_sample, mod[bp:bp + bs], p)
    return (y_prompt, y_sample)
```

```python
import functools
import math

import jax
import jax.numpy as jnp
import numpy as np
from jax import lax
from jax.experimental import pallas as pl
from jax.experimental.pallas import tpu as pltpu

F32 = jnp.float32
BF16 = jnp.bfloat16

EPS = 1e-6
N_MOD = 6
GRID_W = 64
HEAD_DIM = 128
N_Q_HEADS = 4
N_KV_HEADS = 2
ROPE_THETA = 10000.0
TOP_K = 4
SWIGLU_ALPHA = 1.702
SWIGLU_LIMIT = 7.0
FILTER_EMB = 33
FILTER_BANDS = (FILTER_EMB - 1) // 2
DECAY_TARGET = 1e-2
MIN_DECAY = math.log(DECAY_TARGET) / 1.5
MAX_DECAY = math.log(DECAY_TARGET) / 0.3

LANES = 128
SUBLANES = 8
NEG_BIG = -1e30
VMEM_LIMIT = 60 * 1024 * 1024

TN_ADALN = 1536
TM_INPROJ = 1024
CHUNKS_INPROJ = 2
TL_FILTER = 1024
TN_DFT = 8192
TK1_DFT = 8
TQ_ATTN = 512
TK_ATTN = 512
NSUB_ATTN = 32
TM_MIX = 1024
CHUNKS_MIX = 2
TM_MOE = 1024
E_STEP_MOE = 8


def _cparams(*sem):
    return pltpu.CompilerParams(dimension_semantics=sem, vmem_limit_bytes=VMEM_LIMIT)


def _dot(a, b):
    return jnp.dot(a, b, preferred_element_type=F32)


def _dot_hi(a, b):
    return jnp.dot(a, b, preferred_element_type=F32, precision=lax.Precision.HIGHEST)


def _tile(n, want):
    t = min(n, want)
    assert n % t == 0, (n, want)
    return t


def _adaln_kernel(c_ref, w_ref, b_ref, o_ref):
    c = c_ref[...]
    o_ref[...] = _dot_hi(c * jax.nn.sigmoid(c), w_ref[...]) + b_ref[...]


def adaln(c, w_ada, b_ada):
    r, d = c.shape
    n = w_ada.shape[1]
    tn = _tile(n, TN_ADALN)
    return pl.pallas_call(
        _adaln_kernel,
        grid=(n // tn,),
        in_specs=[pl.BlockSpec((r, d), lambda j: (0, 0)),
                  pl.BlockSpec((d, tn), lambda j: (0, j)),
                  pl.BlockSpec((1, tn), lambda j: (0, j))],
        out_specs=pl.BlockSpec((r, tn), lambda j: (0, j)),
        out_shape=jax.ShapeDtypeStruct((r, n), F32),
        compiler_params=_cparams("arbitrary"),
        name="adaln",
    )(c, w_ada, b_ada.reshape(1, n))


def _rope(xn, cos, sin_signed):
    q = HEAD_DIM // 4
    lane = lax.broadcasted_iota(jnp.int32, xn.shape, 1)
    first_half = (lane % (2 * q)) < q
    rot = jnp.where(first_half, pltpu.roll(xn, HEAD_DIM - q, 1), pltpu.roll(xn, q, 1))
    return xn * cos + rot * sin_signed


def _inproj_kernel(x_ref, xp_ref, xn_ref, mod_ref, g_ref, w_ref, cos_ref, sin_ref, qg_ref, kg_ref, cw_ref, cb_ref,
                   x0_ref, uu_ref, q_ref, k_ref, v_ref, gh_ref, ga_ref, *, n_chunks, **dims):
    i = pl.program_id(1)
    rows = x_ref.shape[0] // n_chunks
    for ci in range(n_chunks):
        lo, hi = ci * rows, (ci + 1) * rows
        prev8 = xp_ref[...] if ci == 0 else x_ref[lo - SUBLANES:lo, :]
        next8 = xn_ref[...] if ci == n_chunks - 1 else x_ref[hi:hi + SUBLANES, :]
        has_prev = jnp.where(i > 0, 1.0, 0.0) if ci == 0 else 1.0
        has_next = jnp.where(i < pl.num_programs(1) - 1, 1.0, 0.0) if ci == n_chunks - 1 else 1.0
        _inproj_rows(slice(lo, hi), prev8, next8, has_prev, has_next, x_ref, mod_ref, g_ref, w_ref, cos_ref, sin_ref,
                     qg_ref, kg_ref, cw_ref, cb_ref, x0_ref, uu_ref, q_ref, k_ref, v_ref, gh_ref, ga_ref, **dims)


def _inproj_rows(r, prev8, next8, has_prev, has_next, x_ref, mod_ref, g_ref, w_ref, cos_ref, sin_ref, qg_ref, kg_ref,
                 cw_ref, cb_ref, x0_ref, uu_ref, q_ref, k_ref, v_ref, gh_ref, ga_ref, *, d_hy3, d_attn, d_kv, d_model,
                 q_scale):
    def norm_mod(x):
        y = x * lax.rsqrt(jnp.mean(x * x, axis=-1, keepdims=True) + EPS) * g_ref[...]
        return (y * (1.0 + mod_ref[1:2, :]) + mod_ref[0:1, :]).astype(BF16)

    n = norm_mod(x_ref[r, :])

    s0 = d_hy3
    s1 = s0 + d_attn
    s2 = s1 + d_kv
    s3 = s2 + d_kv
    s4 = s3 + d_model
    cos = cos_ref[r, :]
    sin = sin_ref[r, :]

    def norm_rope(z, gain):
        zn = z * lax.rsqrt(jnp.mean(z * z, axis=-1, keepdims=True) + EPS) * gain
        return _rope(zn, cos, sin)

    tm = n.shape[0]
    n_edge = norm_mod(jnp.concatenate([prev8, next8], axis=0))
    u_ext = _dot(jnp.concatenate([n, n_edge], axis=0), w_ref[:, 0:s0])
    u = u_ext[0:tm]
    prev_row = u_ext[tm + SUBLANES - 1:tm + SUBLANES, :] * has_prev
    next_row = u_ext[tm + SUBLANES:tm + SUBLANES + 1, :] * has_next
    row = lax.broadcasted_iota(jnp.int32, u.shape, 0)
    up = jnp.where(row == 0, prev_row, pltpu.roll(u, 1, 0))
    un = jnp.where(row == tm - 1, next_row, pltpu.roll(u, tm - 1, 0))
    yc = up * cw_ref[0:1, :] + u * cw_ref[1:2, :] + un * cw_ref[2:3, :] + cb_ref[...]
    c = d_hy3 // 3
    x0_ref[r, :] = yc[:, 0:c].astype(BF16)
    uu_ref[r, :] = (yc[:, c:2 * c] * yc[:, 2 * c:3 * c]).astype(BF16)

    qf = _dot(n, w_ref[:, s0:s1])
    kf = _dot(n, w_ref[:, s1:s2])
    for h in range(d_attn // HEAD_DIM):
        sl = slice(h * HEAD_DIM, (h + 1) * HEAD_DIM)
        q_ref[r, sl] = (norm_rope(qf[:, sl], qg_ref[...]) * q_scale).astype(BF16)
    for h in range(d_kv // HEAD_DIM):
        sl = slice(h * HEAD_DIM, (h + 1) * HEAD_DIM)
        k_ref[r, sl] = norm_rope(kf[:, sl], kg_ref[...]).astype(BF16)

    v_ref[r, :] = _dot(n, w_ref[:, s2:s3]).astype(BF16)
    gh_ref[r, :] = _dot(n, w_ref[:, s3:s4]).astype(BF16)
    ga_ref[r, :] = _dot(n, w_ref[:, s4:s4 + d_model]).astype(BF16)


def inproj(x, mod, norm_g, w_in_bf, cos, sin_signed, q_norm, k_norm, conv_w, conv_b, d_hy3, d_attn, d_kv):
    b, l, d = x.shape
    tm = _tile(l, TM_INPROJ)
    r = tm // SUBLANES
    nrow8 = l // SUBLANES
    d_in = w_in_bf.shape[1]
    q_scale = math.log2(math.e) / math.sqrt(HEAD_DIM)
    n_chunks = CHUNKS_INPROJ if tm % (CHUNKS_INPROJ * 2 * SUBLANES) == 0 else 1
    kern = functools.partial(_inproj_kernel, n_chunks=n_chunks, d_hy3=d_hy3, d_attn=d_attn, d_kv=d_kv, d_model=d,
                             q_scale=q_scale)

    def tok(w):
        return pl.BlockSpec((None, tm, w), lambda bi, i: (bi, i, 0))

    def const(shape):
        return pl.BlockSpec(shape, lambda bi, i: (0,) * len(shape))

    outs = [d_hy3 // 3, d_hy3 // 3, d_attn, d_kv, d_kv, d, d]
    return pl.pallas_call(
        kern,
        grid=(b, l // tm),
        in_specs=[tok(d),
                  pl.BlockSpec((None, SUBLANES, d), lambda bi, i: (bi, jnp.maximum(i * r - 1, 0), 0)),
                  pl.BlockSpec((None, SUBLANES, d), lambda bi, i: (bi, jnp.minimum((i + 1) * r, nrow8 - 1), 0)),
                  pl.BlockSpec((None, N_MOD, d), lambda bi, i: (bi, 0, 0)),
                  const((1, d)),
                  const((d, d_in)),
                  pl.BlockSpec((tm, HEAD_DIM), lambda bi, i: (i, 0)),
                  pl.BlockSpec((tm, HEAD_DIM), lambda bi, i: (i, 0)),
                  const((1, HEAD_DIM)),
                  const((1, HEAD_DIM)),
                  const((3, d_hy3)),
                  const((1, d_hy3))],
        out_specs=[tok(w) for w in outs],
        out_shape=[jax.ShapeDtypeStruct((b, l, w), BF16) for w in outs],
        compiler_params=_cparams("parallel", "parallel"),
        name="inproj",
    )(x, x, x, mod, norm_g.reshape(1, d), w_in_bf, cos, sin_signed, q_norm.reshape(1, HEAD_DIM),
      k_norm.reshape(1, HEAD_DIM), conv_w, conv_b.reshape(1, d_hy3))


HALF_LANES = LANES // 2


def _filter_kernel(w1_ref, b1_ref, f1_ref, w2_ref, b2_ref, f2_ref, w3a_ref, w3b_ref, b3_ref, dl_ref, o_ref, *, l):
    tl = o_ref.shape[0]
    half = tl // 2
    lane = lax.broadcasted_iota(jnp.int32, (half, LANES), 1)
    feat = lane % HALF_LANES
    row = (pl.program_id(0) * tl + lax.broadcasted_iota(jnp.int32, (half, LANES), 0)
           + jnp.where(lane < HALF_LANES, 0, half))
    pos = jnp.where(row < l, row, 2 * l - row).astype(F32)
    t = pos * (1.0 / (l - 1))
    band_idx = jnp.where(feat <= FILTER_BANDS, feat - 1, feat - 1 - FILTER_BANDS).astype(F32)
    band = 1e-4 + band_idx * ((FILTER_BANDS - 1 - 1e-4) / (FILTER_BANDS - 1))
    phase = jnp.where(feat <= FILTER_BANDS, 0.0, 0.5 * math.pi)
    trig = jnp.cos(band * (pos * (2.0 * math.pi / l)) + phase)
    z = jnp.where(feat == 0, t, jnp.where(feat < FILTER_EMB, trig, 0.0))
    h = jnp.sin(f1_ref[...] * (_dot_hi(z, w1_ref[...]) + b1_ref[...]))
    h = jnp.sin(f2_ref[...] * (_dot_hi(h, w2_ref[...]) + b2_ref[...]))
    for w3_ref, lane0, r0 in ((w3a_ref, 0, 0), (w3b_ref, HALF_LANES, half)):
        k = (_dot_hi(h, w3_ref[...]) + b3_ref[...]) * jnp.exp(-t[:, lane0:lane0 + 1] * dl_ref[...])
        o_ref[r0:r0 + half, :] = jnp.where(row[:, lane0:lane0 + 1] == l, 0.0, k).astype(BF16)


def hyena_filter(l, w1, b1, f1, w2, b2, f2, w3, b3):
    c2 = w3.shape[1]
    c = c2 // 2
    hid = w1.shape[1]
    assert w1.shape[0] == FILTER_EMB and FILTER_EMB <= HALF_LANES and hid <= HALF_LANES

    def pad_half(a, rows):
        return jnp.pad(a.astype(F32), ((0, rows - a.shape[0]), (0, HALF_LANES - a.shape[1])))

    def block_diag2(a):
        z = jnp.zeros_like(a)
        return jnp.block([[a, z], [z, a]])

    def padv(v):
        return jnp.tile(jnp.pad(v.astype(F32), (0, HALF_LANES - hid)), 2).reshape(1, LANES)

    w1p = block_diag2(pad_half(w1, HALF_LANES))
    w2p = block_diag2(pad_half(w2, HALF_LANES))
    w3h = jnp.pad(w3.astype(F32), ((0, HALF_LANES - hid), (0, 0)))
    w3a = jnp.concatenate([w3h, jnp.zeros_like(w3h)], axis=0)
    w3b = jnp.concatenate([jnp.zeros_like(w3h), w3h], axis=0)
    deltas = jnp.abs(jnp.linspace(MIN_DECAY, MAX_DECAY, c, dtype=F32)).reshape(1, c)
    tl = _tile(l, TL_FILTER)
    nfwd = l // tl

    def const(shape):
        return pl.BlockSpec(shape, lambda i: (0, 0))

    return pl.pallas_call(
        functools.partial(_filter_kernel, l=l),
        grid=(2 * nfwd,),
        in_specs=[const((LANES, LANES)), const((1, LANES)), const((1, LANES)),
                  const((LANES, LANES)), const((1, LANES)), const((1, LANES)),
                  pl.BlockSpec((LANES, c), lambda i: (0, i // nfwd)),
                  pl.BlockSpec((LANES, c), lambda i: (0, i // nfwd)),
                  pl.BlockSpec((1, c), lambda i: (0, i // nfwd)),
                  const((1, c))],
        out_specs=pl.BlockSpec((tl, c), lambda i: (i, 0)),
        out_shape=jax.ShapeDtypeStruct((2 * l, c), BF16),
        compiler_params=_cparams("parallel"),
        name="hyena_filter",
    )(w1p, padv(b1), padv(f1), w2p, padv(b2), padv(f2), w3a, w3b, b3.astype(F32).reshape(1, c2), deltas)


def _fft_split(n):
    n2 = 128
    n1 = n // n2
    assert n1 * n2 == n and n1 % 16 == 0, n
    return n1, n2


def _angles(rows, cols, period):
    prod = (jnp.arange(rows, dtype=jnp.int32)[:, None] * jnp.arange(cols, dtype=jnp.int32)[None, :]) % period
    return prod.astype(F32) * (2.0 * math.pi / period)


def _dft_tables(n1, n2):
    n = n1 * n2
    nk = n1 // 2 + 8
    ang1 = _angles(nk, n1, n1)
    c1, s1 = jnp.cos(ang1), jnp.sin(ang1)
    fa = jnp.concatenate([c1, -s1], axis=0).astype(BF16)
    k1 = jnp.arange(nk)
    wgt = jnp.where((k1 == 0) | (k1 == n1 // 2), 1.0, jnp.where(k1 < n1 // 2, 2.0, 0.0)) * (1.0 / n)
    fc = jnp.concatenate([c1.T * wgt[None, :], -s1.T * wgt[None, :]], axis=1).astype(BF16)
    ang2 = _angles(n2, n2, n2)
    c2, s2 = jnp.cos(ang2), jnp.sin(ang2)
    g_fwd = jnp.block([[c2, s2], [-s2, c2]]).astype(BF16)
    angt = _angles(nk, n2, n)
    tr, ti = jnp.cos(angt), jnp.sin(angt)
    twr = jnp.broadcast_to(tr[:, :, None], (nk, n2, LANES))
    twi = jnp.broadcast_to(ti[:, :, None], (nk, n2, LANES))
    cr, ci = tr[:, None, :], ti[:, None, :]
    gfk = jnp.concatenate([jnp.concatenate([c2 * cr - s2 * ci, c2 * ci + s2 * cr], axis=2),
                           jnp.concatenate([-s2 * cr - c2 * ci, c2 * cr - s2 * ci], axis=2)], axis=1).astype(BF16)
    rr, ri = tr[:, :, None], ti[:, :, None]
    gik = jnp.concatenate([jnp.concatenate([rr * c2 - ri * s2, -rr * s2 - ri * c2], axis=2),
                           jnp.concatenate([ri * c2 + rr * s2, rr * c2 - ri * s2], axis=2)], axis=1).astype(BF16)
    return nk, fa, fc, g_fwd, twr, twi, gfk, gik


def _fft_a_kernel(f_ref, u_ref, o_ref):
    k, tn2, c = u_ref.shape
    u = u_ref[...].reshape(k, tn2 * c)
    o_ref[...] = _dot(f_ref[...], u).astype(BF16).reshape(o_ref.shape)


def fft_a(fa, u):
    b, k, n2, c = u.shape
    m = fa.shape[0]
    tn2 = _tile(n2, max(TN_DFT // c, 1))
    return pl.pallas_call(
        _fft_a_kernel,
        grid=(b, n2 // tn2),
        in_specs=[pl.BlockSpec((m, k), lambda bi, j: (0, 0)),
                  pl.BlockSpec((None, k, tn2, c), lambda bi, j: (bi, 0, j, 0))],
        out_specs=pl.BlockSpec((None, m, tn2, c), lambda bi, j: (bi, 0, j, 0)),
        out_shape=jax.ShapeDtypeStruct((b, m, n2, c), BF16),
        compiler_params=_cparams("parallel", "parallel"),
        name="fft_a",
    )(fa, u)


def _lane_tile(t, c):
    return jnp.concatenate([t] * (c // LANES), axis=1) if c > LANES else t


def _fft_mf_kernel(a_ref, twr_ref, twi_ref, g_ref, o_ref, *, tk1):
    for j in range(tk1):
        ar = a_ref[0, j].astype(F32)
        ai = a_ref[1, j].astype(F32)
        c = ar.shape[1]
        twr = _lane_tile(twr_ref[j], c)
        twi = _lane_tile(twi_ref[j], c)
        br = ar * twr + ai * twi
        bi = ai * twr - ar * twi
        x = _dot(g_ref[...], jnp.concatenate([br, bi], axis=0).astype(BF16))
        n2 = ar.shape[0]
        o_ref[0, j] = x[:n2].astype(BF16)
        o_ref[1, j] = x[n2:].astype(BF16)


def fft_mf(a, twr, twi, g_fwd):
    _, n1, n2, c = a.shape
    tk1 = TK1_DFT
    kern = functools.partial(_fft_mf_kernel, tk1=tk1)
    return pl.pallas_call(
        kern,
        grid=(n1 // tk1,),
        in_specs=[pl.BlockSpec((2, tk1, n2, c), lambda i: (0, i, 0, 0)),
                  pl.BlockSpec((tk1, n2, LANES), lambda i: (i, 0, 0)),
                  pl.BlockSpec((tk1, n2, LANES), lambda i: (i, 0, 0)),
                  pl.BlockSpec((2 * n2, 2 * n2), lambda i: (0, 0))],
        out_specs=pl.BlockSpec((2, tk1, n2, c), lambda i: (0, i, 0, 0)),
        out_shape=jax.ShapeDtypeStruct((2, n1, n2, c), BF16),
        compiler_params=_cparams("parallel"),
        name="fft_mf",
    )(a, twr, twi, g_fwd)


def _fft_m_kernel(a_ref, kf_ref, gf_ref, gi_ref, o_ref, *, tk1, piece):
    out_r, out_i = [], []
    for j in range(tk1):
        n2, c = a_ref.shape[2], a_ref.shape[3]
        x = _dot(gf_ref[j], jnp.concatenate([a_ref[0, j], a_ref[1, j]], axis=0))
        xr, xi = x[:n2], x[n2:]
        kr = kf_ref[0, j].astype(F32)
        ki = kf_ref[1, j].astype(F32)
        zr = xr * kr - xi * ki
        zi = xr * ki + xi * kr
        y = _dot(gi_ref[j], jnp.concatenate([zr, zi], axis=0).astype(BF16))
        out_r.append(y[:n2].astype(BF16))
        out_i.append(y[n2:].astype(BF16))
    y_all = jnp.stack(out_r + out_i, axis=0)
    for s in range(n2 // piece):
        o_ref[:, s * piece * c:(s + 1) * piece * c] = (
            y_all[:, s * piece:(s + 1) * piece, :].reshape(2 * tk1, piece * c))


def fft_m(a, kf, gfk, gik):
    b, _, n1, n2, c = a.shape
    tk1 = TK1_DFT
    piece = _tile(n2, max(TN_DFT // c, 1))
    kern = functools.partial(_fft_m_kernel, tk1=tk1, piece=piece)
    return pl.pallas_call(
        kern,
        grid=(n1 // tk1, b),
        in_specs=[pl.BlockSpec((None, 2, tk1, n2, c), lambda i, bi: (bi, 0, i, 0, 0)),
                  pl.BlockSpec((2, tk1, n2, c), lambda i, bi: (0, i, 0, 0)),
                  pl.BlockSpec((tk1, 2 * n2, 2 * n2), lambda i, bi: (i, 0, 0)),
                  pl.BlockSpec((tk1, 2 * n2, 2 * n2), lambda i, bi: (i, 0, 0))],
        out_specs=pl.BlockSpec((None, 2 * tk1, n2 * c), lambda i, bi: (bi, i, 0)),
        out_shape=jax.ShapeDtypeStruct((b, 2 * n1, n2 * c), BF16),
        compiler_params=_cparams("parallel", "parallel"),
        name="fft_m",
    )(a, kf, gfk, gik)


def _fft_c_kernel(f_ref, a_ref, uu_ref, bias_ref, o_ref):
    m, tn2, c = uu_ref.shape
    y = _dot(f_ref[...], a_ref[...])
    uu = uu_ref[...].reshape(m, tn2 * c).astype(F32)
    o_ref[...] = (y + uu * bias_ref[...]).astype(BF16).reshape(o_ref.shape)


def fft_c(fc_half, a2, uu, bias_tiled):
    b, k, w = a2.shape
    _, m, n2, c = uu.shape
    tn = bias_tiled.shape[1]
    tn2 = tn // c
    return pl.pallas_call(
        _fft_c_kernel,
        grid=(b, n2 // tn2),
        in_specs=[pl.BlockSpec((m, k), lambda bi, j: (0, 0)),
                  pl.BlockSpec((None, k, tn), lambda bi, j: (bi, 0, j)),
                  pl.BlockSpec((None, m, tn2, c), lambda bi, j: (bi, 0, j, 0)),
                  pl.BlockSpec((1, tn), lambda bi, j: (0, 0))],
        out_specs=pl.BlockSpec((None, m, tn2, c), lambda bi, j: (bi, 0, j, 0)),
        out_shape=jax.ShapeDtypeStruct((b, m, n2, c), BF16),
        compiler_params=_cparams("parallel", "parallel"),
        name="fft_c",
    )(fc_half, a2, uu, bias_tiled)


def hyena_long_conv(uu, k_circ, filt_bias):
    b, l, c = uu.shape
    n = 2 * l
    n1, n2 = _fft_split(n)
    nk, fa, fc, g_fwd, twr, twi, gfk, gik = _dft_tables(n1, n2)
    ka = fft_a(fa, k_circ.reshape(1, n1, n2, c))
    kf = fft_mf(ka.reshape(2, nk, n2, c), twr, twi, g_fwd)
    u4 = uu.reshape(b, n1 // 2, n2, c)
    a = fft_a(fa[:, :n1 // 2], u4)
    a2 = fft_m(a.reshape(b, 2, nk, n2, c), kf, gfk, gik)
    cols = np.arange(2 * nk).reshape(2, nk // TK1_DFT, TK1_DFT).transpose(1, 0, 2).reshape(-1)
    tn = _tile(n2, max(TN_DFT // c, 1)) * c
    bias_tiled = jnp.tile(filt_bias.astype(F32), tn // c).reshape(1, tn)
    yc = fft_c(fc[:n1 // 2][:, cols], a2, u4, bias_tiled)
    return yc.reshape(b, l, c)


def _attn_kernel(q_ref, k_ref, v_ref, o_ref, qs_ref, m_ref, acc_ref, *, tq, tk, nsub):
    ki = pl.program_id(3)

    @pl.when(ki == 0)
    def _():
        qs_ref[0:tq, :] = q_ref[:, 0:HEAD_DIM]
        qs_ref[tq:2 * tq, :] = q_ref[:, HEAD_DIM:2 * HEAD_DIM]
        m_ref[...] = jnp.full(m_ref.shape, -jnp.inf, F32)
        acc_ref[...] = jnp.zeros(acc_ref.shape, F32)

    qs = qs_ref[...]
    for j in range(nsub):
        k = k_ref[j * tk:(j + 1) * tk, :]
        v = v_ref[j * tk:(j + 1) * tk, :]
        v_aug = jnp.concatenate([v, jnp.ones_like(v)], axis=1)
        s = lax.dot_general(qs, k, (((1,), (1,)), ((), ())), preferred_element_type=F32)
        m_prev = m_ref[...]
        m_new = jnp.maximum(m_prev, jnp.max(s, axis=-1, keepdims=True))
        alpha = jnp.exp2(m_prev - m_new)
        p = jnp.exp2((s - jnp.concatenate([m_new] * (tk // LANES), axis=1)).astype(BF16))
        acc_ref[...] = jnp.concatenate([alpha, alpha], axis=1) * acc_ref[...] + _dot(p, v_aug)
        m_ref[...] = m_new

    @pl.when(ki == pl.num_programs(3) - 1)
    def _():
        o = acc_ref[:, 0:HEAD_DIM] / acc_ref[:, HEAD_DIM:2 * HEAD_DIM]
        o_ref[:, 0:HEAD_DIM] = o[0:tq].astype(BF16)
        o_ref[:, HEAD_DIM:2 * HEAD_DIM] = o[tq:2 * tq].astype(BF16)


def attention(q, k, v):
    b, l, _ = q.shape
    tq = _tile(l, TQ_ATTN)
    tk = _tile(l, TK_ATTN)
    nsub = _tile(l // tk, NSUB_ATTN)
    g = (N_Q_HEADS // N_KV_HEADS) * HEAD_DIM
    kern = functools.partial(_attn_kernel, tq=tq, tk=tk, nsub=nsub)
    tkb = tk * nsub
    return pl.pallas_call(
        kern,
        grid=(b, N_KV_HEADS, l // tq, l // tkb),
        in_specs=[pl.BlockSpec((None, tq, g), lambda bi, h, i, j: (bi, i, h)),
                  pl.BlockSpec((None, tkb, HEAD_DIM), lambda bi, h, i, j: (bi, j, h)),
                  pl.BlockSpec((None, tkb, HEAD_DIM), lambda bi, h, i, j: (bi, j, h))],
        out_specs=pl.BlockSpec((None, tq, g), lambda bi, h, i, j: (bi, i, h)),
        out_shape=jax.ShapeDtypeStruct(q.shape, BF16),
        scratch_shapes=[pltpu.VMEM((2 * tq, HEAD_DIM), BF16),
                        pltpu.VMEM((2 * tq, LANES), F32),
                        pltpu.VMEM((2 * tq, 2 * HEAD_DIM), F32)],
        compiler_params=_cparams("parallel", "parallel", "parallel", "arbitrary"),
        name="attention",
    )(q, k, v)


def _mix_kernel(x_ref, x0_ref, yc_ref, ya_ref, gh_ref, ga_ref, mod_ref, woh_ref, woa_ref, wo_ref, g_ref, rw_ref, rb_ref,
                xn_ref, n2_ref, gates_ref, *, n_exp, n_chunks):
    rows = x_ref.shape[0] // n_chunks
    for ci in range(n_chunks):
        _mix_rows(slice(ci * rows, (ci + 1) * rows), x_ref, x0_ref, yc_ref, ya_ref, gh_ref, ga_ref, mod_ref, woh_ref,
                  woa_ref, wo_ref, g_ref, rw_ref, rb_ref, xn_ref, n2_ref, gates_ref, n_exp)


def _mix_rows(r, x_ref, x0_ref, yc_ref, ya_ref, gh_ref, ga_ref, mod_ref, woh_ref, woa_ref, wo_ref, g_ref, rw_ref, rb_ref,
              xn_ref, n2_ref, gates_ref, n_exp):
    yh = (x0_ref[r, :].astype(F32) * yc_ref[r, :].astype(F32)).astype(BF16)
    th = _dot(yh, woh_ref[...])
    ta = _dot(ya_ref[r, :], woa_ref[...])
    mixed = (jax.nn.sigmoid(gh_ref[r, :].astype(F32)) * th + jax.nn.sigmoid(ga_ref[r, :].astype(F32)) * ta)
    mix = _dot(mixed.astype(BF16), wo_ref[...])
    x = x_ref[r, :] + mod_ref[2:3, :] * mix
    xn_ref[r, :] = x

    y = x * lax.rsqrt(jnp.mean(x * x, axis=-1, keepdims=True) + EPS) * g_ref[...]
    n2 = y * (1.0 + mod_ref[4:5, :]) + mod_ref[3:4, :]
    n2_hi = n2.astype(BF16)
    n2_ref[r, :] = n2_hi

    n2_lo = (n2 - n2_hi.astype(F32)).astype(BF16)
    r_hi = _dot(n2_hi, rw_ref[...])
    r_lo = _dot(n2_lo, rw_ref[...])
    lane = lax.broadcasted_iota(jnp.int32, r_hi.shape, 1)
    logits = r_hi + (pltpu.roll(r_hi, LANES - n_exp, 1) + r_lo) + rb_ref[...]
    logits = jnp.where(lane < n_exp, logits, NEG_BIG)

    work = logits
    vals, hots = [], []
    for _ in range(TOP_K):
        m = jnp.max(work, axis=-1, keepdims=True)
        idx = jnp.min(jnp.where(work == m, lane, LANES), axis=-1, keepdims=True)
        hot = lane == idx
        vals.append(m)
        hots.append(hot)
        work = jnp.where(hot, -jnp.inf, work)
    exps = [jnp.exp(v - vals[0]) for v in vals]
    den = exps[0] + exps[1] + exps[2] + exps[3]
    gates = jnp.zeros(logits.shape, F32)
    for hot, e in zip(hots, exps):
        gates = jnp.where(hot, e / den, gates)
    gates_ref[r, :] = gates


def pack_router(router_w, router_b):
    d, ne = router_w.shape
    assert 2 * ne <= LANES
    w_hi, w_lo = _split_bf16(router_w)
    w_p = jnp.concatenate([w_hi, w_lo, jnp.zeros((d, LANES - 2 * ne), BF16)], axis=1)
    return w_p, jnp.pad(router_b, (0, LANES - ne)).reshape(1, LANES), ne


def mix_router(x, x0, yc, ya, gh, ga, mod, w_out_h, w_out_a, w_o, norm_g, router):
    router_w_p, router_b_p, n_exp = router
    b, l, d = x.shape
    tm = _tile(l, TM_MIX)
    n_chunks = CHUNKS_MIX if tm % (CHUNKS_MIX * 2 * SUBLANES) == 0 else 1
    c = yc.shape[2]
    da = ya.shape[2]

    def tok(w):
        return pl.BlockSpec((None, tm, w), lambda bi, i: (bi, i, 0))

    def const(shape):
        return pl.BlockSpec(shape, lambda bi, i: (0,) * len(shape))

    return pl.pallas_call(
        functools.partial(_mix_kernel, n_exp=n_exp, n_chunks=n_chunks),
        grid=(b, l // tm),
        in_specs=[tok(d), tok(c), tok(c), tok(da), tok(d), tok(d),
                  pl.BlockSpec((None, N_MOD, d), lambda bi, i: (bi, 0, 0)),
                  const((c, d)), const((da, d)), const((d, d)), const((1, d)),
                  const((d, LANES)), const((1, LANES))],
        out_specs=[tok(d), tok(d), tok(LANES)],
        out_shape=[jax.ShapeDtypeStruct((b, l, d), F32),
                   jax.ShapeDtypeStruct((b, l, d), BF16),
                   jax.ShapeDtypeStruct((b, l, LANES), F32)],
        compiler_params=_cparams("parallel", "parallel"),
        name="mix_router",
    )(x, x0, yc, ya, gh, ga, mod, w_out_h, w_out_a, w_o, norm_g.reshape(1, d), router_w_p, router_b_p)


def _deinterleave_kernel(w_ref, p_ref, o_ref):
    o_ref[...] = _dot(w_ref[...].astype(BF16), p_ref[...]).astype(BF16)


def deinterleave_up(w_up):
    ne, d, f2 = w_up.shape
    src = lax.broadcasted_iota(jnp.int32, (f2, f2), 0)
    dst = lax.broadcasted_iota(jnp.int32, (f2, f2), 1)
    perm = (src == jnp.where(dst < f2 // 2, 2 * dst, 2 * (dst - f2 // 2) + 1)).astype(BF16)
    return pl.pallas_call(
        _deinterleave_kernel,
        grid=(ne,),
        in_specs=[pl.BlockSpec((None, d, f2), lambda e: (e, 0, 0)),
                  pl.BlockSpec((f2, f2), lambda e: (0, 0))],
        out_specs=pl.BlockSpec((None, d, f2), lambda e: (e, 0, 0)),
        out_shape=jax.ShapeDtypeStruct((ne, d, f2), BF16),
        compiler_params=_cparams("parallel"),
        name="deinterleave_up",
    )(w_up, perm)


def _split_bf16(a):
    hi = a.astype(BF16)
    return hi, (a - hi.astype(F32)).astype(BF16)


def _moe_kernel(n2_ref, gates_ref, xn_ref, mod_ref, wu_ref, bu_ref, wd_ref, bd_ref, o_ref, acc_ref, *, e_step):
    eb = pl.program_id(2)
    gates = gates_ref[...]

    @pl.when(eb == 0)
    def _():
        b_hi, b_lo = _split_bf16(bd_ref[...])
        g_bf = gates.astype(BF16)
        acc_ref[...] = _dot(g_bf, b_hi) + _dot(g_bf, b_lo)

    n2 = n2_ref[...]
    f = wd_ref.shape[1]
    acts = []
    for j in range(e_step):
        h = _dot(n2, wu_ref[j]) + bu_ref[j]
        xg = jnp.minimum(h[:, 0:f], SWIGLU_LIMIT)
        xl = jnp.clip(h[:, f:2 * f], -SWIGLU_LIMIT, SWIGLU_LIMIT)
        act = xg * jax.nn.sigmoid(SWIGLU_ALPHA * xg) * (xl + 1.0)
        ge = pltpu.roll(gates, (LANES - (eb * e_step + j)) & (LANES - 1), 1)[:, 0:1]
        acts.append((act * ge).astype(BF16))
    acc_ref[...] += _dot(jnp.concatenate(acts, axis=1), wd_ref[...].reshape(e_step * f, wd_ref.shape[2]))

    @pl.when(eb == pl.num_programs(2) - 1)
    def _():
        o_ref[...] = xn_ref[...] + mod_ref[5:6, :] * acc_ref[...]


def moe(n2, gates, xn, mod, wu, bu, wd, bd_p):
    b, l, d = xn.shape
    ne, f, _ = wd.shape
    tm = _tile(l, TM_MOE)
    e_step = _tile(ne, E_STEP_MOE)

    def tok(w):
        return pl.BlockSpec((None, tm, w), lambda bi, i, e: (bi, i, 0))

    return pl.pallas_call(
        functools.partial(_moe_kernel, e_step=e_step),
        grid=(b, l // tm, ne // e_step),
        in_specs=[tok(d), tok(LANES), tok(d),
                  pl.BlockSpec((None, N_MOD, d), lambda bi, i, e: (bi, 0, 0)),
                  pl.BlockSpec((e_step, d, 2 * f), lambda bi, i, e: (e, 0, 0)),
                  pl.BlockSpec((e_step, 1, 2 * f), lambda bi, i, e: (e, 0, 0)),
                  pl.BlockSpec((e_step, f, d), lambda bi, i, e: (e, 0, 0)),
                  pl.BlockSpec((LANES, d), lambda bi, i, e: (0, 0))],
        out_specs=tok(d),
        out_shape=jax.ShapeDtypeStruct((b, l, d), F32),
        scratch_shapes=[pltpu.VMEM((tm, d), F32)],
        compiler_params=_cparams("parallel", "parallel", "arbitrary"),
        name="moe",
    )(n2, gates, xn, mod, wu, bu, wd, bd_p)


def _rope_tables(l):
    rows = l // GRID_W
    row = jnp.repeat(jnp.arange(rows, dtype=F32), GRID_W)
    col = jnp.tile(jnp.arange(GRID_W, dtype=F32), rows)
    n_freq = HEAD_DIM // 4
    freqs = ROPE_THETA ** (-jnp.arange(n_freq, dtype=F32) / n_freq)
    ang_r = row[:, None] * freqs
    ang_c = col[:, None] * freqs
    cos = jnp.concatenate([jnp.cos(ang_r), jnp.cos(ang_r), jnp.cos(ang_c), jnp.cos(ang_c)], axis=-1)
    sin = jnp.concatenate([-jnp.sin(ang_r), jnp.sin(ang_r), -jnp.sin(ang_c), jnp.sin(ang_c)], axis=-1)
    return cos, sin


def _encoder_layer(x, mod, p):
    b, l, d = x.shape
    c = p['filt_bias'].shape[0]
    cos, sin_signed = _rope_tables(l)
    x0, uu, q, k, v, gh, ga = inproj(x, mod, p['norm_mix'], p['w_in'], cos, sin_signed, p['q_norm'], p['k_norm'],
                                     p['conv_w'], p['conv_b'], 3 * c, N_Q_HEADS * HEAD_DIM, N_KV_HEADS * HEAD_DIM)
    k_circ = hyena_filter(l, p['filt_w1'], p['filt_b1'], p['filt_freq1'], p['filt_w2'], p['filt_b2'],
                          p['filt_freq2'], p['filt_w3'], p['filt_b3'])
    yc = hyena_long_conv(uu, k_circ, p['filt_bias'])
    ya = attention(q, k, v)
    xn, n2, gates = mix_router(x, x0, yc, ya, gh, ga, mod, p['w_out_h'], p['w_out_a'], p['w_o'], p['norm_ffn'],
                               p['router'])
    return moe(n2, gates, xn, mod, p['wu'], p['bu'], p['wd'], p['bd'])


def kernel(x_prompt, x_sample, c_prompt, c_sample, w_ada, b_ada, norm_mix, w_in, conv_w, conv_b, filt_w1, filt_b1, filt_freq1, filt_w2, filt_b2, filt_freq2, filt_w3, filt_b3, filt_bias, q_norm, k_norm, w_out_h, w_out_a, w_o, norm_ffn, router_w, router_b, w_up, b_up, w_down, b_down):
    depth = w_ada.shape[0]
    d = x_prompt.shape[-1]
    bp = c_prompt.shape[0]
    bs = c_sample.shape[0]
    ne = router_w.shape[-1]
    y_prompt, y_sample = x_prompt, x_sample
    for i in range(depth):
        rows = -(-(bp + bs) // 8) * 8
        c_all = jnp.pad(jnp.concatenate([c_prompt, c_sample], axis=0), ((0, rows - bp - bs), (0, 0)))
        mod = adaln(c_all, w_ada[i], b_ada[i]).reshape(rows, N_MOD, d)
        p = {
            'norm_mix': norm_mix[i], 'w_in': w_in[i].astype(BF16),
            'conv_w': conv_w[i], 'conv_b': conv_b[i],
            'filt_w1': filt_w1[i], 'filt_b1': filt_b1[i], 'filt_freq1': filt_freq1[i],
            'filt_w2': filt_w2[i], 'filt_b2': filt_b2[i], 'filt_freq2': filt_freq2[i],
            'filt_w3': filt_w3[i], 'filt_b3': filt_b3[i], 'filt_bias': filt_bias[i],
            'q_norm': q_norm[i], 'k_norm': k_norm[i],
            'w_out_h': w_out_h[i].astype(BF16), 'w_out_a': w_out_a[i].astype(BF16), 'w_o': w_o[i].astype(BF16),
            'norm_ffn': norm_ffn[i],
            'router': pack_router(router_w[i], router_b[i]),
            'wu': deinterleave_up(w_up[i]),
            'bu': jnp.concatenate([b_up[i][:, None, 0::2], b_up[i][:, None, 1::2]], axis=-1),
            'wd': w_down[i].astype(BF16),
            'bd': jnp.pad(b_down[i], ((0, LANES - ne), (0, 0))),
        }
        y_prompt = _encoder_layer(y_prompt, mod[:bp], p)
        y_sample = _encoder_layer(y_sample, mod[bp:bp + bs], p)
    return (y_prompt, y_sample)
```

```python
import functools
import math

import jax
import jax.numpy as jnp
import numpy as np
from jax import lax
from jax.experimental import pallas as pl
from jax.experimental.pallas import tpu as pltpu

F32 = jnp.float32
BF16 = jnp.bfloat16

EPS = 1e-6
N_MOD = 6
GRID_W = 64
HEAD_DIM = 128
N_Q_HEADS = 4
N_KV_HEADS = 2
ROPE_THETA = 10000.0
TOP_K = 4
SWIGLU_ALPHA = 1.702
SWIGLU_LIMIT = 7.0
FILTER_EMB = 33
FILTER_BANDS = (FILTER_EMB - 1) // 2
DECAY_TARGET = 1e-2
MIN_DECAY = math.log(DECAY_TARGET) / 1.5
MAX_DECAY = math.log(DECAY_TARGET) / 0.3

LANES = 128
SUBLANES = 8
NEG_BIG = -1e30
VMEM_LIMIT = 60 * 1024 * 1024

TN_ADALN = 1536
TM_INPROJ = 1024
CHUNKS_INPROJ = 2
TL_FILTER = 1024
TN_DFT = 8192
TK1_DFT = 8
TQ_ATTN = 512
TK_ATTN = 512
NSUB_ATTN = 32
TM_MIX = 1024
CHUNKS_MIX = 2
TM_MOE = 1024
E_STEP_MOE = 8


def _cparams(*sem):
    return pltpu.CompilerParams(dimension_semantics=sem, vmem_limit_bytes=VMEM_LIMIT)


def _dot(a, b):
    return jnp.dot(a, b, preferred_element_type=F32)


def _dot_hi(a, b):
    return jnp.dot(a, b, preferred_element_type=F32, precision=lax.Precision.HIGHEST)


def _tile(n, want):
    t = min(n, want)
    assert n % t == 0, (n, want)
    return t


def _adaln_kernel(c_ref, w_ref, b_ref, o_ref):
    c = c_ref[...]
    o_ref[...] = _dot_hi(c * jax.nn.sigmoid(c), w_ref[...]) + b_ref[...]


def adaln(c, w_ada, b_ada):
    r, d = c.shape
    n = w_ada.shape[1]
    tn = _tile(n, TN_ADALN)
    return pl.pallas_call(
        _adaln_kernel,
        grid=(n // tn,),
        in_specs=[pl.BlockSpec((r, d), lambda j: (0, 0)),
                  pl.BlockSpec((d, tn), lambda j: (0, j)),
                  pl.BlockSpec((1, tn), lambda j: (0, j))],
        out_specs=pl.BlockSpec((r, tn), lambda j: (0, j)),
        out_shape=jax.ShapeDtypeStruct((r, n), F32),
        compiler_params=_cparams("arbitrary"),
        name="adaln",
    )(c, w_ada, b_ada.reshape(1, n))


def _rope(xn, cos, sin_signed):
    q = HEAD_DIM // 4
    lane = lax.broadcasted_iota(jnp.int32, xn.shape, 1)
    first_half = (lane % (2 * q)) < q
    rot = jnp.where(first_half, pltpu.roll(xn, HEAD_DIM - q, 1), pltpu.roll(xn, q, 1))
    return xn * cos + rot * sin_signed


def _inproj_kernel(x_ref, xp_ref, xn_ref, mod_ref, g_ref, w_ref, cos_ref, sin_ref, qg_ref, kg_ref, cw_ref, cb_ref,
                   x0_ref, uu_ref, q_ref, k_ref, v_ref, gh_ref, ga_ref, *, n_chunks, **dims):
    i = pl.program_id(1)
    rows = x_ref.shape[0] // n_chunks
    for ci in range(n_chunks):
        lo, hi = ci * rows, (ci + 1) * rows
        prev8 = xp_ref[...] if ci == 0 else x_ref[lo - SUBLANES:lo, :]
        next8 = xn_ref[...] if ci == n_chunks - 1 else x_ref[hi:hi + SUBLANES, :]
        has_prev = jnp.where(i > 0, 1.0, 0.0) if ci == 0 else 1.0
        has_next = jnp.where(i < pl.num_programs(1) - 1, 1.0, 0.0) if ci == n_chunks - 1 else 1.0
        _inproj_rows(slice(lo, hi), prev8, next8, has_prev, has_next, x_ref, mod_ref, g_ref, w_ref, cos_ref, sin_ref,
                     qg_ref, kg_ref, cw_ref, cb_ref, x0_ref, uu_ref, q_ref, k_ref, v_ref, gh_ref, ga_ref, **dims)


def _inproj_rows(r, prev8, next8, has_prev, has_next, x_ref, mod_ref, g_ref, w_ref, cos_ref, sin_ref, qg_ref, kg_ref,
                 cw_ref, cb_ref, x0_ref, uu_ref, q_ref, k_ref, v_ref, gh_ref, ga_ref, *, d_hy3, d_attn, d_kv, d_model,
                 q_scale):
    def norm_mod(x):
        y = x * lax.rsqrt(jnp.mean(x * x, axis=-1, keepdims=True) + EPS) * g_ref[...]
        return (y * (1.0 + mod_ref[1:2, :]) + mod_ref[0:1, :]).astype(BF16)

    n = norm_mod(x_ref[r, :])

    s0 = d_hy3
    s1 = s0 + d_attn
    s2 = s1 + d_kv
    s3 = s2 + d_kv
    s4 = s3 + d_model
    cos = cos_ref[r, :]
    sin = sin_ref[r, :]

    def norm_rope(z, gain):
        zn = z * lax.rsqrt(jnp.mean(z * z, axis=-1, keepdims=True) + EPS) * gain
        return _rope(zn, cos, sin)

    tm = n.shape[0]
    n_edge = norm_mod(jnp.concatenate([prev8, next8], axis=0))
    u_ext = _dot(jnp.concatenate([n, n_edge], axis=0), w_ref[:, 0:s0])
    u = u_ext[0:tm]
    prev_row = u_ext[tm + SUBLANES - 1:tm + SUBLANES, :] * has_prev
    next_row = u_ext[tm + SUBLANES:tm + SUBLANES + 1, :] * has_next
    row = lax.broadcasted_iota(jnp.int32, u.shape, 0)
    up = jnp.where(row == 0, prev_row, pltpu.roll(u, 1, 0))
    un = jnp.where(row == tm - 1, next_row, pltpu.roll(u, tm - 1, 0))
    yc = up * cw_ref[0:1, :] + u * cw_ref[1:2, :] + un * cw_ref[2:3, :] + cb_ref[...]
    c = d_hy3 // 3
    x0_ref[r, :] = yc[:, 0:c].astype(BF16)
    uu_ref[r, :] = (yc[:, c:2 * c] * yc[:, 2 * c:3 * c]).astype(BF16)

    qf = _dot(n, w_ref[:, s0:s1])
    kf = _dot(n, w_ref[:, s1:s2])
    for h in range(d_attn // HEAD_DIM):
        sl = slice(h * HEAD_DIM, (h + 1) * HEAD_DIM)
        q_ref[r, sl] = (norm_rope(qf[:, sl], qg_ref[...]) * q_scale).astype(BF16)
    for h in range(d_kv // HEAD_DIM):
        sl = slice(h * HEAD_DIM, (h + 1) * HEAD_DIM)
        k_ref[r, sl] = norm_rope(kf[:, sl], kg_ref[...]).astype(BF16)

    v_ref[r, :] = _dot(n, w_ref[:, s2:s3]).astype(BF16)
    gh_ref[r, :] = _dot(n, w_ref[:, s3:s4]).astype(BF16)
    ga_ref[r, :] = _dot(n, w_ref[:, s4:s4 + d_model]).astype(BF16)


def inproj(x, mod, norm_g, w_in_bf, cos, sin_signed, q_norm, k_norm, conv_w, conv_b, d_hy3, d_attn, d_kv):
    b, l, d = x.shape
    tm = _tile(l, TM_INPROJ)
    r = tm // SUBLANES
    nrow8 = l // SUBLANES
    d_in = w_in_bf.shape[1]
    q_scale = math.log2(math.e) / math.sqrt(HEAD_DIM)
    n_chunks = CHUNKS_INPROJ if tm % (CHUNKS_INPROJ * 2 * SUBLANES) == 0 else 1
    kern = functools.partial(_inproj_kernel, n_chunks=n_chunks, d_hy3=d_hy3, d_attn=d_attn, d_kv=d_kv, d_model=d,
                             q_scale=q_scale)

    def tok(w):
        return pl.BlockSpec((None, tm, w), lambda bi, i: (bi, i, 0))

    def const(shape):
        return pl.BlockSpec(shape, lambda bi, i: (0,) * len(shape))

    outs = [d_hy3 // 3, d_hy3 // 3, d_attn, d_kv, d_kv, d, d]
    return pl.pallas_call(
        kern,
        grid=(b, l // tm),
        in_specs=[tok(d),
                  pl.BlockSpec((None, SUBLANES, d), lambda bi, i: (bi, jnp.maximum(i * r - 1, 0), 0)),
                  pl.BlockSpec((None, SUBLANES, d), lambda bi, i: (bi, jnp.minimum((i + 1) * r, nrow8 - 1), 0)),
                  pl.BlockSpec((None, N_MOD, d), lambda bi, i: (bi, 0, 0)),
                  const((1, d)),
                  const((d, d_in)),
                  pl.BlockSpec((tm, HEAD_DIM), lambda bi, i: (i, 0)),
                  pl.BlockSpec((tm, HEAD_DIM), lambda bi, i: (i, 0)),
                  const((1, HEAD_DIM)),
                  const((1, HEAD_DIM)),
                  const((3, d_hy3)),
                  const((1, d_hy3))],
        out_specs=[tok(w) for w in outs],
        out_shape=[jax.ShapeDtypeStruct((b, l, w), BF16) for w in outs],
        compiler_params=_cparams("parallel", "parallel"),
        name="inproj",
    )(x, x, x, mod, norm_g.reshape(1, d), w_in_bf, cos, sin_signed, q_norm.reshape(1, HEAD_DIM),
      k_norm.reshape(1, HEAD_DIM), conv_w, conv_b.reshape(1, d_hy3))


HALF_LANES = LANES // 2


def _filter_kernel(w1_ref, b1_ref, f1_ref, w2_ref, b2_ref, f2_ref, w3a_ref, w3b_ref, b3_ref, dl_ref, o_ref, *, l):
    tl = o_ref.shape[0]
    half = tl // 2
    lane = lax.broadcasted_iota(jnp.int32, (half, LANES), 1)
    feat = lane % HALF_LANES
    row = (pl.program_id(0) * tl + lax.broadcasted_iota(jnp.int32, (half, LANES), 0)
           + jnp.where(lane < HALF_LANES, 0, half))
    pos = jnp.where(row < l, row, 2 * l - row).astype(F32)
    t = pos * (1.0 / (l - 1))
    band_idx = jnp.where(feat <= FILTER_BANDS, feat - 1, feat - 1 - FILTER_BANDS).astype(F32)
    band = 1e-4 + band_idx * ((FILTER_BANDS - 1 - 1e-4) / (FILTER_BANDS - 1))
    phase = jnp.where(feat <= FILTER_BANDS, 0.0, 0.5 * math.pi)
    trig = jnp.cos(band * (pos * (2.0 * math.pi / l)) + phase)
    z = jnp.where(feat == 0, t, jnp.where(feat < FILTER_EMB, trig, 0.0))
    h = jnp.sin(f1_ref[...] * (_dot_hi(z, w1_ref[...]) + b1_ref[...]))
    h = jnp.sin(f2_ref[...] * (_dot_hi(h, w2_ref[...]) + b2_ref[...]))
    for w3_ref, lane0, r0 in ((w3a_ref, 0, 0), (w3b_ref, HALF_LANES, half)):
        k = (_dot_hi(h, w3_ref[...]) + b3_ref[...]) * jnp.exp(-t[:, lane0:lane0 + 1] * dl_ref[...])
        o_ref[r0:r0 + half, :] = jnp.where(row[:, lane0:lane0 + 1] == l, 0.0, k).astype(BF16)


def hyena_filter(l, w1, b1, f1, w2, b2, f2, w3, b3):
    c2 = w3.shape[1]
    c = c2 // 2
    hid = w1.shape[1]
    assert w1.shape[0] == FILTER_EMB and FILTER_EMB <= HALF_LANES and hid <= HALF_LANES

    def pad_half(a, rows):
        return jnp.pad(a.astype(F32), ((0, rows - a.shape[0]), (0, HALF_LANES - a.shape[1])))

    def block_diag2(a):
        z = jnp.zeros_like(a)
        return jnp.block([[a, z], [z, a]])

    def padv(v):
        return jnp.tile(jnp.pad(v.astype(F32), (0, HALF_LANES - hid)), 2).reshape(1, LANES)

    w1p = block_diag2(pad_half(w1, HALF_LANES))
    w2p = block_diag2(pad_half(w2, HALF_LANES))
    w3h = jnp.pad(w3.astype(F32), ((0, HALF_LANES - hid), (0, 0)))
    w3a = jnp.concatenate([w3h, jnp.zeros_like(w3h)], axis=0)
    w3b = jnp.concatenate([jnp.zeros_like(w3h), w3h], axis=0)
    deltas = jnp.abs(jnp.linspace(MIN_DECAY, MAX_DECAY, c, dtype=F32)).reshape(1, c)
    tl = _tile(l, TL_FILTER)
    nfwd = l // tl

    def const(shape):
        return pl.BlockSpec(shape, lambda i: (0, 0))

    return pl.pallas_call(
        functools.partial(_filter_kernel, l=l),
        grid=(2 * nfwd,),
        in_specs=[const((LANES, LANES)), const((1, LANES)), const((1, LANES)),
                  const((LANES, LANES)), const((1, LANES)), const((1, LANES)),
                  pl.BlockSpec((LANES, c), lambda i: (0, i // nfwd)),
                  pl.BlockSpec((LANES, c), lambda i: (0, i // nfwd)),
                  pl.BlockSpec((1, c), lambda i: (0, i // nfwd)),
                  const((1, c))],
        out_specs=pl.BlockSpec((tl, c), lambda i: (i, 0)),
        out_shape=jax.ShapeDtypeStruct((2 * l, c), BF16),
        compiler_params=_cparams("parallel"),
        name="hyena_filter",
    )(w1p, padv(b1), padv(f1), w2p, padv(b2), padv(f2), w3a, w3b, b3.astype(F32).reshape(1, c2), deltas)


def _fft_split(n):
    n2 = 128
    n1 = n // n2
    assert n1 * n2 == n and n1 % 16 == 0, n
    return n1, n2


def _angles(rows, cols, period):
    prod = (jnp.arange(rows, dtype=jnp.int32)[:, None] * jnp.arange(cols, dtype=jnp.int32)[None, :]) % period
    return prod.astype(F32) * (2.0 * math.pi / period)


def _dft_tables(n1, n2):
    n = n1 * n2
    nk = n1 // 2 + 8
    ang1 = _angles(nk, n1, n1)
    c1, s1 = jnp.cos(ang1), jnp.sin(ang1)
    fa = jnp.concatenate([c1, -s1], axis=0).astype(BF16)
    k1 = jnp.arange(nk)
    wgt = jnp.where((k1 == 0) | (k1 == n1 // 2), 1.0, jnp.where(k1 < n1 // 2, 2.0, 0.0)) * (1.0 / n)
    fc = jnp.concatenate([c1.T * wgt[None, :], -s1.T * wgt[None, :]], axis=1).astype(BF16)
    ang2 = _angles(n2, n2, n2)
    c2, s2 = jnp.cos(ang2), jnp.sin(ang2)
    angt = _angles(nk, n2, n)
    tr, ti = jnp.cos(angt), jnp.sin(angt)
    cr, ci = tr[:, None, :], ti[:, None, :]
    gfk = jnp.concatenate([jnp.concatenate([c2 * cr - s2 * ci, c2 * ci + s2 * cr], axis=2),
                           jnp.concatenate([-s2 * cr - c2 * ci, c2 * cr - s2 * ci], axis=2)], axis=1).astype(BF16)
    rr, ri = tr[:, :, None], ti[:, :, None]
    gik = jnp.concatenate([jnp.concatenate([rr * c2 - ri * s2, -rr * s2 - ri * c2], axis=2),
                           jnp.concatenate([ri * c2 + rr * s2, rr * c2 - ri * s2], axis=2)], axis=1).astype(BF16)
    return nk, fa, fc, gfk, gik


def _fft_a_kernel(f_ref, u_ref, o_ref):
    k, tn2, c = u_ref.shape
    u = u_ref[...].reshape(k, tn2 * c)
    o_ref[...] = _dot(f_ref[...], u).astype(BF16).reshape(o_ref.shape)


def fft_a(fa, u):
    b, k, n2, c = u.shape
    m = fa.shape[0]
    tn2 = _tile(n2, max(TN_DFT // c, 1))
    return pl.pallas_call(
        _fft_a_kernel,
        grid=(b, n2 // tn2),
        in_specs=[pl.BlockSpec((m, k), lambda bi, j: (0, 0)),
                  pl.BlockSpec((None, k, tn2, c), lambda bi, j: (bi, 0, j, 0))],
        out_specs=pl.BlockSpec((None, m, tn2, c), lambda bi, j: (bi, 0, j, 0)),
        out_shape=jax.ShapeDtypeStruct((b, m, n2, c), BF16),
        compiler_params=_cparams("parallel", "parallel"),
        name="fft_a",
    )(fa, u)


def _lane_tile(t, c):
    return jnp.concatenate([t] * (c // LANES), axis=1) if c > LANES else t


def _fft_mf_kernel(a_ref, g_ref, o_ref, *, tk1):
    for j in range(tk1):
        n2 = a_ref.shape[2]
        x = _dot(g_ref[j], jnp.concatenate([a_ref[0, j], a_ref[1, j]], axis=0))
        o_ref[0, j] = x[:n2].astype(BF16)
        o_ref[1, j] = x[n2:].astype(BF16)


def fft_mf(a, gfk):
    _, n1, n2, c = a.shape
    tk1 = TK1_DFT
    kern = functools.partial(_fft_mf_kernel, tk1=tk1)
    return pl.pallas_call(
        kern,
        grid=(n1 // tk1,),
        in_specs=[pl.BlockSpec((2, tk1, n2, c), lambda i: (0, i, 0, 0)),
                  pl.BlockSpec((tk1, 2 * n2, 2 * n2), lambda i: (i, 0, 0))],
        out_specs=pl.BlockSpec((2, tk1, n2, c), lambda i: (0, i, 0, 0)),
        out_shape=jax.ShapeDtypeStruct((2, n1, n2, c), BF16),
        compiler_params=_cparams("parallel"),
        name="fft_mf",
    )(a, gfk)


def _fft_m_kernel(a_ref, kf_ref, gf_ref, gi_ref, o_ref, *, tk1, piece):
    out_r, out_i = [], []
    for j in range(tk1):
        n2, c = a_ref.shape[2], a_ref.shape[3]
        x = _dot(gf_ref[j], jnp.concatenate([a_ref[0, j], a_ref[1, j]], axis=0))
        xr, xi = x[:n2], x[n2:]
        kr = kf_ref[0, j].astype(F32)
        ki = kf_ref[1, j].astype(F32)
        zr = xr * kr - xi * ki
        zi = xr * ki + xi * kr
        y = _dot(gi_ref[j], jnp.concatenate([zr, zi], axis=0).astype(BF16))
        out_r.append(y[:n2].astype(BF16))
        out_i.append(y[n2:].astype(BF16))
    y_all = jnp.stack(out_r + out_i, axis=0)
    for s in range(n2 // piece):
        o_ref[:, s * piece * c:(s + 1) * piece * c] = (
            y_all[:, s * piece:(s + 1) * piece, :].reshape(2 * tk1, piece * c))


def fft_m(a, kf, gfk, gik):
    b, _, n1, n2, c = a.shape
    tk1 = TK1_DFT
    piece = _tile(n2, max(TN_DFT // c, 1))
    kern = functools.partial(_fft_m_kernel, tk1=tk1, piece=piece)
    return pl.pallas_call(
        kern,
        grid=(n1 // tk1, b),
        in_specs=[pl.BlockSpec((None, 2, tk1, n2, c), lambda i, bi: (bi, 0, i, 0, 0)),
                  pl.BlockSpec((2, tk1, n2, c), lambda i, bi: (0, i, 0, 0)),
                  pl.BlockSpec((tk1, 2 * n2, 2 * n2), lambda i, bi: (i, 0, 0)),
                  pl.BlockSpec((tk1, 2 * n2, 2 * n2), lambda i, bi: (i, 0, 0))],
        out_specs=pl.BlockSpec((None, 2 * tk1, n2 * c), lambda i, bi: (bi, i, 0)),
        out_shape=jax.ShapeDtypeStruct((b, 2 * n1, n2 * c), BF16),
        compiler_params=_cparams("parallel", "parallel"),
        name="fft_m",
    )(a, kf, gfk, gik)


def _fft_c_kernel(f_ref, a_ref, uu_ref, bias_ref, o_ref):
    m, tn2, c = uu_ref.shape
    y = _dot(f_ref[...], a_ref[...])
    uu = uu_ref[...].reshape(m, tn2 * c).astype(F32)
    o_ref[...] = (y + uu * bias_ref[...]).astype(BF16).reshape(o_ref.shape)


def fft_c(fc_half, a2, uu, bias_tiled):
    b, k, w = a2.shape
    _, m, n2, c = uu.shape
    tn = bias_tiled.shape[1]
    tn2 = tn // c
    return pl.pallas_call(
        _fft_c_kernel,
        grid=(b, n2 // tn2),
        in_specs=[pl.BlockSpec((m, k), lambda bi, j: (0, 0)),
                  pl.BlockSpec((None, k, tn), lambda bi, j: (bi, 0, j)),
                  pl.BlockSpec((None, m, tn2, c), lambda bi, j: (bi, 0, j, 0)),
                  pl.BlockSpec((1, tn), lambda bi, j: (0, 0))],
        out_specs=pl.BlockSpec((None, m, tn2, c), lambda bi, j: (bi, 0, j, 0)),
        out_shape=jax.ShapeDtypeStruct((b, m, n2, c), BF16),
        compiler_params=_cparams("parallel", "parallel"),
        name="fft_c",
    )(fc_half, a2, uu, bias_tiled)


def hyena_long_conv(uu, k_circ, filt_bias):
    b, l, c = uu.shape
    n = 2 * l
    n1, n2 = _fft_split(n)
    nk, fa, fc, gfk, gik = _dft_tables(n1, n2)
    ka = fft_a(fa, k_circ.reshape(1, n1, n2, c))
    kf = fft_mf(ka.reshape(2, nk, n2, c), gfk)
    u4 = uu.reshape(b, n1 // 2, n2, c)
    a = fft_a(fa[:, :n1 // 2], u4)
    a2 = fft_m(a.reshape(b, 2, nk, n2, c), kf, gfk, gik)
    cols = np.arange(2 * nk).reshape(2, nk // TK1_DFT, TK1_DFT).transpose(1, 0, 2).reshape(-1)
    tn = _tile(n2, max(TN_DFT // c, 1)) * c
    bias_tiled = jnp.tile(filt_bias.astype(F32), tn // c).reshape(1, tn)
    yc = fft_c(fc[:n1 // 2][:, cols], a2, u4, bias_tiled)
    return yc.reshape(b, l, c)


def _attn_kernel(q_ref, k_ref, v_ref, o_ref, qs_ref, m_ref, acc_ref, *, tq, tk, nsub):
    ki = pl.program_id(3)

    @pl.when(ki == 0)
    def _():
        qs_ref[0:tq, :] = q_ref[:, 0:HEAD_DIM]
        qs_ref[tq:2 * tq, :] = q_ref[:, HEAD_DIM:2 * HEAD_DIM]
        m_ref[...] = jnp.full(m_ref.shape, -jnp.inf, F32)
        acc_ref[...] = jnp.zeros(acc_ref.shape, F32)

    qs = qs_ref[...]
    for j in range(nsub):
        k = k_ref[j * tk:(j + 1) * tk, :]
        v = v_ref[j * tk:(j + 1) * tk, :]
        v_aug = jnp.concatenate([v, jnp.ones_like(v)], axis=1)
        s = lax.dot_general(qs, k, (((1,), (1,)), ((), ())), preferred_element_type=F32)
        m_prev = m_ref[...]
        m_new = jnp.maximum(m_prev, jnp.max(s, axis=-1, keepdims=True))
        alpha = jnp.exp2(m_prev - m_new)
        p = jnp.exp2((s - jnp.concatenate([m_new] * (tk // LANES), axis=1)).astype(BF16))
        acc_ref[...] = jnp.concatenate([alpha, alpha], axis=1) * acc_ref[...] + _dot(p, v_aug)
        m_ref[...] = m_new

    @pl.when(ki == pl.num_programs(3) - 1)
    def _():
        o = acc_ref[:, 0:HEAD_DIM] / acc_ref[:, HEAD_DIM:2 * HEAD_DIM]
        o_ref[:, 0:HEAD_DIM] = o[0:tq].astype(BF16)
        o_ref[:, HEAD_DIM:2 * HEAD_DIM] = o[tq:2 * tq].astype(BF16)


def attention(q, k, v):
    b, l, _ = q.shape
    tq = _tile(l, TQ_ATTN)
    tk = _tile(l, TK_ATTN)
    nsub = _tile(l // tk, NSUB_ATTN)
    g = (N_Q_HEADS // N_KV_HEADS) * HEAD_DIM
    kern = functools.partial(_attn_kernel, tq=tq, tk=tk, nsub=nsub)
    tkb = tk * nsub
    return pl.pallas_call(
        kern,
        grid=(b, N_KV_HEADS, l // tq, l // tkb),
        in_specs=[pl.BlockSpec((None, tq, g), lambda bi, h, i, j: (bi, i, h)),
                  pl.BlockSpec((None, tkb, HEAD_DIM), lambda bi, h, i, j: (bi, j, h)),
                  pl.BlockSpec((None, tkb, HEAD_DIM), lambda bi, h, i, j: (bi, j, h))],
        out_specs=pl.BlockSpec((None, tq, g), lambda bi, h, i, j: (bi, i, h)),
        out_shape=jax.ShapeDtypeStruct(q.shape, BF16),
        scratch_shapes=[pltpu.VMEM((2 * tq, HEAD_DIM), BF16),
                        pltpu.VMEM((2 * tq, LANES), F32),
                        pltpu.VMEM((2 * tq, 2 * HEAD_DIM), F32)],
        compiler_params=_cparams("parallel", "parallel", "parallel", "arbitrary"),
        name="attention",
    )(q, k, v)


def _mix_kernel(x_ref, x0_ref, yc_ref, ya_ref, gh_ref, ga_ref, mod_ref, woh_ref, woa_ref, wo_ref, g_ref, rw_ref, rb_ref,
                xn_ref, n2_ref, gates_ref, *, n_exp, n_chunks):
    rows = x_ref.shape[0] // n_chunks
    for ci in range(n_chunks):
        _mix_rows(slice(ci * rows, (ci + 1) * rows), x_ref, x0_ref, yc_ref, ya_ref, gh_ref, ga_ref, mod_ref, woh_ref,
                  woa_ref, wo_ref, g_ref, rw_ref, rb_ref, xn_ref, n2_ref, gates_ref, n_exp)


def _mix_rows(r, x_ref, x0_ref, yc_ref, ya_ref, gh_ref, ga_ref, mod_ref, woh_ref, woa_ref, wo_ref, g_ref, rw_ref, rb_ref,
              xn_ref, n2_ref, gates_ref, n_exp):
    yh = (x0_ref[r, :].astype(F32) * yc_ref[r, :].astype(F32)).astype(BF16)
    th = _dot(yh, woh_ref[...])
    ta = _dot(ya_ref[r, :], woa_ref[...])
    mixed = (jax.nn.sigmoid(gh_ref[r, :].astype(F32)) * th + jax.nn.sigmoid(ga_ref[r, :].astype(F32)) * ta)
    mix = _dot(mixed.astype(BF16), wo_ref[...])
    x = x_ref[r, :] + mod_ref[2:3, :] * mix
    xn_ref[r, :] = x

    y = x * lax.rsqrt(jnp.mean(x * x, axis=-1, keepdims=True) + EPS) * g_ref[...]
    n2 = y * (1.0 + mod_ref[4:5, :]) + mod_ref[3:4, :]
    n2_hi = n2.astype(BF16)
    n2_ref[r, :] = n2_hi

    n2_lo = (n2 - n2_hi.astype(F32)).astype(BF16)
    r_hi = _dot(n2_hi, rw_ref[...])
    r_lo = _dot(n2_lo, rw_ref[...])
    lane = lax.broadcasted_iota(jnp.int32, r_hi.shape, 1)
    logits = r_hi + (pltpu.roll(r_hi, LANES - n_exp, 1) + r_lo) + rb_ref[...]
    logits = jnp.where(lane < n_exp, logits, NEG_BIG)

    work = logits
    vals, hots = [], []
    for _ in range(TOP_K):
        m = jnp.max(work, axis=-1, keepdims=True)
        idx = jnp.min(jnp.where(work == m, lane, LANES), axis=-1, keepdims=True)
        hot = lane == idx
        vals.append(m)
        hots.append(hot)
        work = jnp.where(hot, -jnp.inf, work)
    exps = [jnp.exp(v - vals[0]) for v in vals]
    den = exps[0] + exps[1] + exps[2] + exps[3]
    gates = jnp.zeros(logits.shape, F32)
    for hot, e in zip(hots, exps):
        gates = jnp.where(hot, e / den, gates)
    gates_ref[r, :] = gates


def pack_router(router_w, router_b):
    d, ne = router_w.shape
    assert 2 * ne <= LANES
    w_hi, w_lo = _split_bf16(router_w)
    w_p = jnp.concatenate([w_hi, w_lo, jnp.zeros((d, LANES - 2 * ne), BF16)], axis=1)
    return w_p, jnp.pad(router_b, (0, LANES - ne)).reshape(1, LANES), ne


def mix_router(x, x0, yc, ya, gh, ga, mod, w_out_h, w_out_a, w_o, norm_g, router):
    router_w_p, router_b_p, n_exp = router
    b, l, d = x.shape
    tm = _tile(l, TM_MIX)
    n_chunks = CHUNKS_MIX if tm % (CHUNKS_MIX * 2 * SUBLANES) == 0 else 1
    c = yc.shape[2]
    da = ya.shape[2]

    def tok(w):
        return pl.BlockSpec((None, tm, w), lambda bi, i: (bi, i, 0))

    def const(shape):
        return pl.BlockSpec(shape, lambda bi, i: (0,) * len(shape))

    return pl.pallas_call(
        functools.partial(_mix_kernel, n_exp=n_exp, n_chunks=n_chunks),
        grid=(b, l // tm),
        in_specs=[tok(d), tok(c), tok(c), tok(da), tok(d), tok(d),
                  pl.BlockSpec((None, N_MOD, d), lambda bi, i: (bi, 0, 0)),
                  const((c, d)), const((da, d)), const((d, d)), const((1, d)),
                  const((d, LANES)), const((1, LANES))],
        out_specs=[tok(d), tok(d), tok(LANES)],
        out_shape=[jax.ShapeDtypeStruct((b, l, d), F32),
                   jax.ShapeDtypeStruct((b, l, d), BF16),
                   jax.ShapeDtypeStruct((b, l, LANES), F32)],
        compiler_params=_cparams("parallel", "parallel"),
        name="mix_router",
    )(x, x0, yc, ya, gh, ga, mod, w_out_h, w_out_a, w_o, norm_g.reshape(1, d), router_w_p, router_b_p)


def _deinterleave_kernel(w_ref, p_ref, o_ref):
    o_ref[...] = _dot(w_ref[...].astype(BF16), p_ref[...]).astype(BF16)


def deinterleave_up(w_up):
    ne, d, f2 = w_up.shape
    src = lax.broadcasted_iota(jnp.int32, (f2, f2), 0)
    dst = lax.broadcasted_iota(jnp.int32, (f2, f2), 1)
    perm = (src == jnp.where(dst < f2 // 2, 2 * dst, 2 * (dst - f2 // 2) + 1)).astype(BF16)
    return pl.pallas_call(
        _deinterleave_kernel,
        grid=(ne,),
        in_specs=[pl.BlockSpec((None, d, f2), lambda e: (e, 0, 0)),
                  pl.BlockSpec((f2, f2), lambda e: (0, 0))],
        out_specs=pl.BlockSpec((None, d, f2), lambda e: (e, 0, 0)),
        out_shape=jax.ShapeDtypeStruct((ne, d, f2), BF16),
        compiler_params=_cparams("parallel"),
        name="deinterleave_up",
    )(w_up, perm)


def _split_bf16(a):
    hi = a.astype(BF16)
    return hi, (a - hi.astype(F32)).astype(BF16)


def _moe_kernel(n2_ref, gates_ref, xn_ref, mod_ref, wu_ref, bu_ref, wd_ref, bd_ref, o_ref, acc_ref, *, e_step):
    eb = pl.program_id(2)
    gates = gates_ref[...]

    @pl.when(eb == 0)
    def _():
        b_hi, b_lo = _split_bf16(bd_ref[...])
        g_bf = gates.astype(BF16)
        acc_ref[...] = _dot(g_bf, b_hi) + _dot(g_bf, b_lo)

    n2 = n2_ref[...]
    f = wd_ref.shape[1]
    acts = []
    for j in range(e_step):
        h = _dot(n2, wu_ref[j]) + bu_ref[j]
        xg = jnp.minimum(h[:, 0:f], SWIGLU_LIMIT)
        xl = jnp.clip(h[:, f:2 * f], -SWIGLU_LIMIT, SWIGLU_LIMIT)
        act = xg * jax.nn.sigmoid(SWIGLU_ALPHA * xg) * (xl + 1.0)
        ge = pltpu.roll(gates, (LANES - (eb * e_step + j)) & (LANES - 1), 1)[:, 0:1]
        acts.append((act * ge).astype(BF16))
    acc_ref[...] += _dot(jnp.concatenate(acts, axis=1), wd_ref[...].reshape(e_step * f, wd_ref.shape[2]))

    @pl.when(eb == pl.num_programs(2) - 1)
    def _():
        o_ref[...] = xn_ref[...] + mod_ref[5:6, :] * acc_ref[...]


def moe(n2, gates, xn, mod, wu, bu, wd, bd_p):
    b, l, d = xn.shape
    ne, f, _ = wd.shape
    tm = _tile(l, TM_MOE)
    e_step = _tile(ne, E_STEP_MOE)

    def tok(w):
        return pl.BlockSpec((None, tm, w), lambda bi, i, e: (bi, i, 0))

    return pl.pallas_call(
        functools.partial(_moe_kernel, e_step=e_step),
        grid=(b, l // tm, ne // e_step),
        in_specs=[tok(d), tok(LANES), tok(d),
                  pl.BlockSpec((None, N_MOD, d), lambda bi, i, e: (bi, 0, 0)),
                  pl.BlockSpec((e_step, d, 2 * f), lambda bi, i, e: (e, 0, 0)),
                  pl.BlockSpec((e_step, 1, 2 * f), lambda bi, i, e: (e, 0, 0)),
                  pl.BlockSpec((e_step, f, d), lambda bi, i, e: (e, 0, 0)),
                  pl.BlockSpec((LANES, d), lambda bi, i, e: (0, 0))],
        out_specs=tok(d),
        out_shape=jax.ShapeDtypeStruct((b, l, d), F32),
        scratch_shapes=[pltpu.VMEM((tm, d), F32)],
        compiler_params=_cparams("parallel", "parallel", "arbitrary"),
        name="moe",
    )(n2, gates, xn, mod, wu, bu, wd, bd_p)


def _rope_tables(l):
    rows = l // GRID_W
    row = jnp.repeat(jnp.arange(rows, dtype=F32), GRID_W)
    col = jnp.tile(jnp.arange(GRID_W, dtype=F32), rows)
    n_freq = HEAD_DIM // 4
    freqs = ROPE_THETA ** (-jnp.arange(n_freq, dtype=F32) / n_freq)
    ang_r = row[:, None] * freqs
    ang_c = col[:, None] * freqs
    cos = jnp.concatenate([jnp.cos(ang_r), jnp.cos(ang_r), jnp.cos(ang_c), jnp.cos(ang_c)], axis=-1)
    sin = jnp.concatenate([-jnp.sin(ang_r), jnp.sin(ang_r), -jnp.sin(ang_c), jnp.sin(ang_c)], axis=-1)
    return cos, sin


def _encoder_layer(x, mod, p):
    b, l, d = x.shape
    c = p['filt_bias'].shape[0]
    cos, sin_signed = _rope_tables(l)
    x0, uu, q, k, v, gh, ga = inproj(x, mod, p['norm_mix'], p['w_in'], cos, sin_signed, p['q_norm'], p['k_norm'],
                                     p['conv_w'], p['conv_b'], 3 * c, N_Q_HEADS * HEAD_DIM, N_KV_HEADS * HEAD_DIM)
    k_circ = hyena_filter(l, p['filt_w1'], p['filt_b1'], p['filt_freq1'], p['filt_w2'], p['filt_b2'],
                          p['filt_freq2'], p['filt_w3'], p['filt_b3'])
    yc = hyena_long_conv(uu, k_circ, p['filt_bias'])
    ya = attention(q, k, v)
    xn, n2, gates = mix_router(x, x0, yc, ya, gh, ga, mod, p['w_out_h'], p['w_out_a'], p['w_o'], p['norm_ffn'],
                               p['router'])
    return moe(n2, gates, xn, mod, p['wu'], p['bu'], p['wd'], p['bd'])


def kernel(x_prompt, x_sample, c_prompt, c_sample, w_ada, b_ada, norm_mix, w_in, conv_w, conv_b, filt_w1, filt_b1, filt_freq1, filt_w2, filt_b2, filt_freq2, filt_w3, filt_b3, filt_bias, q_norm, k_norm, w_out_h, w_out_a, w_o, norm_ffn, router_w, router_b, w_up, b_up, w_down, b_down):
    depth = w_ada.shape[0]
    d = x_prompt.shape[-1]
    bp = c_prompt.shape[0]
    bs = c_sample.shape[0]
    ne = router_w.shape[-1]
    y_prompt, y_sample = x_prompt, x_sample
    for i in range(depth):
        rows = -(-(bp + bs) // 8) * 8
        c_all = jnp.pad(jnp.concatenate([c_prompt, c_sample], axis=0), ((0, rows - bp - bs), (0, 0)))
        mod = adaln(c_all, w_ada[i], b_ada[i]).reshape(rows, N_MOD, d)
        p = {
            'norm_mix': norm_mix[i], 'w_in': w_in[i].astype(BF16),
            'conv_w': conv_w[i], 'conv_b': conv_b[i],
            'filt_w1': filt_w1[i], 'filt_b1': filt_b1[i], 'filt_freq1': filt_freq1[i],
            'filt_w2': filt_w2[i], 'filt_b2': filt_b2[i], 'filt_freq2': filt_freq2[i],
            'filt_w3': filt_w3[i], 'filt_b3': filt_b3[i], 'filt_bias': filt_bias[i],
            'q_norm': q_norm[i], 'k_norm': k_norm[i],
            'w_out_h': w_out_h[i].astype(BF16), 'w_out_a': w_out_a[i].astype(BF16), 'w_o': w_o[i].astype(BF16),
            'norm_ffn': norm_ffn[i],
            'router': pack_router(router_w[i], router_b[i]),
            'wu': deinterleave_up(w_up[i]),
            'bu': jnp.concatenate([b_up[i][:, None, 0::2], b_up[i][:, None, 1::2]], axis=-1),
            'wd': w_down[i].astype(BF16),
            'bd': jnp.pad(b_down[i], ((0, LANES - ne), (0, 0))),
        }
        y_prompt = _encoder_layer(y_prompt, mod[:bp], p)
        y_sample = _encoder_layer(y_sample, mod[bp:bp + bs], p)
    return (y_prompt, y_sample)
```
